```python
import jax, jax.numpy as jnp
from jax import lax
import numpy as np

D_MODEL = 1024
BATCH = 4
SEQ = 8192
DEPTH = 1
DEC_BATCH = 32
DEC_SEQ = 32
PAST_LEN = 2048

CHUNK = 64
Q_BLOCK = 2 * CHUNK
HEAD_DIM = 64
POOL_GROUPS = 4
POOL_WINDOWS = (2, 4, 8, 16)
POOL_GROUP_W = D_MODEL // 16
POOL_W = POOL_GROUPS * POOL_GROUP_W
POOL_HIST = max(POOL_WINDOWS) - 1
FOX_H = D_MODEL // 128
FOX_W = FOX_H * HEAD_DIM
CROSS_H = 4
CROSS_W = CROSS_H * HEAD_DIM
N_MEM = 256
N_BRANCH = 3
N_EXPERTS = 32
TOP_K = 4
D_FF = D_MODEL
SWIGLU_LIMIT = 7.0
SWIGLU_ALPHA = 1.702
RMS_EPS = 1e-5
FORGET_BIAS_INIT = 3.0
OFF_POOL = 0
OFF_Q = OFF_POOL + POOL_W
OFF_K = OFF_Q + FOX_W
OFF_V = OFF_K + FOX_W
OFF_F = OFF_V + FOX_W
OFF_CQ = OFF_F + FOX_H
OFF_GATE = OFF_CQ + CROSS_W
N_IN = OFF_GATE + N_BRANCH * D_MODEL

kernel_name = 'hybrid_stream_encoder_step'


def rmsnorm(x, g):
    xf = x.astype(jnp.float32)
    y = xf * lax.rsqrt(jnp.mean(xf * xf, axis=-1, keepdims=True) + RMS_EPS)
    return (y * g.astype(jnp.float32)).astype(x.dtype)


def pool_mixer(u, hist, pos0, w_pool, s_pool):
    B, L, _ = u.shape
    u_ext = jnp.concatenate([hist.astype(u.dtype), u], axis=1)
    uf = u_ext.astype(jnp.float32)
    c0 = jnp.concatenate([jnp.zeros_like(uf[:, :1]), jnp.cumsum(uf, axis=1)], axis=1)
    pos = pos0 + jnp.arange(L)
    groups = []
    for g, w in enumerate(POOL_WINDOWS):
        ch = slice(g * POOL_GROUP_W, (g + 1) * POOL_GROUP_W)
        win = c0[:, POOL_HIST + 1:POOL_HIST + 1 + L, ch] - c0[:, POOL_HIST + 1 - w:POOL_HIST + 1 - w + L, ch]
        cnt = jnp.minimum(pos + 1, w).astype(jnp.float32)
        groups.append(win / cnt[None, :, None] - uf[:, POOL_HIST:, ch])
    pooled = jnp.stack(groups, axis=2)
    mixed = jnp.einsum('blgc,gcd->blgd', pooled, w_pool.astype(jnp.float32))
    out = mixed.reshape(B, L, POOL_W) * s_pool.astype(jnp.float32)
    return out.astype(u.dtype), u_ext[:, -POOL_HIST:]


def fox_block(q, k, v, cq, ck, qpos, kpos):
    s = jnp.einsum('bqhd,bkhd->bhqk', q.astype(jnp.float32), k.astype(jnp.float32)) * (HEAD_DIM ** -0.5)
    s = s + jnp.moveaxis(cq, 1, 2)[..., :, None] - jnp.moveaxis(ck, 1, 2)[..., None, :]
    s = jnp.where((kpos[None, :] <= qpos[:, None])[None, None], s, -jnp.inf)
    p = jax.nn.softmax(s, axis=-1)
    return jnp.einsum('bhqk,bkhd->bqhd', p.astype(v.dtype), v)


def fox_mixer(q, k, v, logf, k_hist, v_hist, logf_hist):
    B, L = q.shape[:2]
    n_hist = k_hist.shape[1]
    k_all = jnp.concatenate([k_hist.astype(k.dtype), k], axis=1)
    v_all = jnp.concatenate([v_hist.astype(v.dtype), v], axis=1)
    c = jnp.cumsum(jnp.concatenate([logf_hist.astype(jnp.float32), logf], axis=1), axis=1)
    cq = c[:, n_hist:]
    kpos = jnp.arange(n_hist + L)
    lb = min(L, Q_BLOCK)
    n_blk = L // lb

    def one_block(i):
        start = i * lb
        q_b = lax.dynamic_slice_in_dim(q, start, lb, axis=1)
        cq_b = lax.dynamic_slice_in_dim(cq, start, lb, axis=1)
        qpos = n_hist + start + jnp.arange(lb)
        return fox_block(q_b, k_all, v_all, cq_b, c, qpos, kpos)

    out = lax.map(one_block, jnp.arange(n_blk))
    return jnp.moveaxis(out, 0, 1).reshape(B, L, FOX_W)


def cross_attend(qc, mem_k, mem_v):
    B, L = qc.shape[:2]
    s = jnp.einsum('blhd,bmhd->bhlm', qc.astype(jnp.float32), mem_k.astype(jnp.float32)) * (HEAD_DIM ** -0.5)
    p = jax.nn.softmax(s, axis=-1)
    return jnp.einsum('bhlm,bmhd->blhd', p.astype(mem_v.dtype), mem_v).reshape(B, L, CROSS_W)


def memory_kv(mem, g_mem, w_mem_kv):
    B, M, _ = mem.shape
    kv = rmsnorm(mem, g_mem) @ w_mem_kv
    return kv[..., :CROSS_W].reshape(B, M, CROSS_H, HEAD_DIM), kv[..., CROSS_W:].reshape(B, M, CROSS_H, HEAD_DIM)


def moe(h, w_router, b_router, w_gate_up, b_gate_up, w_down, b_down):
    B, L, D = h.shape
    t = h.reshape(B * L, D)
    logits = (t @ w_router).astype(jnp.float32) + b_router.astype(jnp.float32)
    top_v, top_i = lax.top_k(logits, TOP_K)
    wts = jax.nn.softmax(top_v, axis=-1)
    comb = jnp.sum(jax.nn.one_hot(top_i, N_EXPERTS, dtype=jnp.float32) * wts[..., None], axis=1)
    out = jnp.zeros((B * L, D), jnp.float32)
    for e in range(N_EXPERTS):
        gu = t @ w_gate_up[e] + b_gate_up[e]
        gate = jnp.minimum(gu[:, 0::2], SWIGLU_LIMIT)
        up = jnp.clip(gu[:, 1::2], -SWIGLU_LIMIT, SWIGLU_LIMIT)
        act = (up + 1.0) * gate * jax.nn.sigmoid(SWIGLU_ALPHA * gate)
        out = out + comb[:, e:e + 1] * (act @ w_down[e] + b_down[e])
    return out.astype(h.dtype).reshape(B, L, D)


def trunk_layer(x, mem_k, mem_v, pool_hist, k_hist, v_hist, logf_hist, pos0, w):
    B, L, _ = x.shape
    h = rmsnorm(x, w['g_mix'])
    proj = h @ w['w_in']
    u = proj[..., OFF_POOL:OFF_Q]
    q = proj[..., OFF_Q:OFF_K].reshape(B, L, FOX_H, HEAD_DIM)
    k = proj[..., OFF_K:OFF_V].reshape(B, L, FOX_H, HEAD_DIM)
    v = proj[..., OFF_V:OFF_F].reshape(B, L, FOX_H, HEAD_DIM)
    logf = jax.nn.log_sigmoid(proj[..., OFF_F:OFF_CQ].astype(jnp.float32) + w['b_f'].astype(jnp.float32))
    qc = proj[..., OFF_CQ:OFF_GATE].reshape(B, L, CROSS_H, HEAD_DIM)
    gates = jax.nn.sigmoid(proj[..., OFF_GATE:].astype(jnp.float32) + w['b_gates'].astype(jnp.float32))
    gates = gates.reshape(B, L, N_BRANCH, D_MODEL).astype(x.dtype)
    pool_out, new_pool = pool_mixer(u, pool_hist, pos0, w['w_pool'], w['s_pool'])
    fox_out = fox_mixer(q, k, v, logf, k_hist, v_hist, logf_hist)
    cross_out = cross_attend(qc, mem_k, mem_v)
    merged = (gates[:, :, 0] * (pool_out @ w['w_br_pool'])
              + gates[:, :, 1] * (fox_out @ w['w_br_fox'])
              + gates[:, :, 2] * (cross_out @ w['w_br_cross']))
    x = x + merged @ w['w_out']
    x = x + moe(rmsnorm(x, w['g_ffn']), w['w_router'], w['b_router'], w['w_gate_up'],
                w['b_gate_up'], w['w_down'], w['b_down'])
    return x, new_pool, k, v, logf


def setup_inputs(seed: int = 0) -> dict:
    key = jax.random.key(seed)
    ks = jax.random.split(key, 29)
    f32 = jnp.float32

    def nrm(k, shape, scale=1.0):
        return scale * jax.random.normal(k, shape, f32)

    L = DEPTH
    return {
        'x_prompt': nrm(ks[0], (BATCH, SEQ, D_MODEL)),
        'x_sample': nrm(ks[1], (DEC_BATCH, DEC_SEQ, D_MODEL)),
        'mem_prompt': nrm(ks[2], (BATCH, N_MEM, D_MODEL)),
        'cache_fox_k': nrm(ks[3], (L, DEC_BATCH, PAST_LEN, FOX_H, HEAD_DIM)),
        'cache_fox_v': nrm(ks[4], (L, DEC_BATCH, PAST_LEN, FOX_H, HEAD_DIM)),
        'cache_fox_logf': jax.nn.log_sigmoid(FORGET_BIAS_INIT + nrm(ks[5], (L, DEC_BATCH, PAST_LEN, FOX_H), 0.5)),
        'state_pool': nrm(ks[6], (L, DEC_BATCH, POOL_HIST, POOL_W)),
        'cache_mem_k': nrm(ks[7], (L, DEC_BATCH, N_MEM, CROSS_H, HEAD_DIM)),
        'cache_mem_v': nrm(ks[8], (L, DEC_BATCH, N_MEM, CROSS_H, HEAD_DIM)),
        'g_mix': 1.0 + nrm(ks[9], (L, D_MODEL), 0.02),
        'w_in': nrm(ks[10], (L, D_MODEL, N_IN), D_MODEL ** -0.5),
        'b_f': FORGET_BIAS_INIT + nrm(ks[11], (L, FOX_H), 0.1),
        'w_pool': nrm(ks[12], (L, POOL_GROUPS, POOL_GROUP_W, POOL_GROUP_W), POOL_GROUP_W ** -0.5),
        's_pool': 0.5 + nrm(ks[13], (L, POOL_W), 0.1),
        'w_br_pool': nrm(ks[14], (L, POOL_W, D_MODEL), POOL_W ** -0.5),
        'w_br_fox': nrm(ks[15], (L, FOX_W, D_MODEL), FOX_W ** -0.5),
        'w_br_cross': nrm(ks[16], (L, CROSS_W, D_MODEL), CROSS_W ** -0.5),
        'b_gates': nrm(ks[17], (L, N_BRANCH * D_MODEL), 0.02),
        'w_out': nrm(ks[18], (L, D_MODEL, D_MODEL), D_MODEL ** -0.5),
        'g_mem': 1.0 + nrm(ks[19], (L, D_MODEL), 0.02),
        'w_mem_kv': nrm(ks[20], (L, D_MODEL, 2 * CROSS_W), D_MODEL ** -0.5),
        'g_ffn': 1.0 + nrm(ks[21], (L, D_MODEL), 0.02),
        'w_router': nrm(ks[22], (L, D_MODEL, N_EXPERTS), D_MODEL ** -0.5),
        'b_router': nrm(ks[23], (L, N_EXPERTS), 0.01),
        'w_gate_up': nrm(ks[24], (L, N_EXPERTS, D_MODEL, 2 * D_FF), D_MODEL ** -0.5),
        'b_gate_up': nrm(ks[25], (L, N_EXPERTS, 2 * D_FF), 0.02),
        'w_down': nrm(ks[26], (L, N_EXPERTS, D_FF, D_MODEL), D_FF ** -0.5),
        'b_down': nrm(ks[27], (L, N_EXPERTS, D_MODEL), 0.02),
        'g_final': 1.0 + nrm(ks[28], (D_MODEL,), 0.02),
    }


def reference(x_prompt, x_sample, mem_prompt, cache_fox_k, cache_fox_v, cache_fox_logf, state_pool,
              cache_mem_k, cache_mem_v, g_mix, w_in, b_f, w_pool, s_pool, w_br_pool, w_br_fox, w_br_cross,
              b_gates, w_out, g_mem, w_mem_kv, g_ffn, w_router, b_router, w_gate_up, b_gate_up, w_down,
              b_down, g_final):
    xp, xs = x_prompt, x_sample
    B = xp.shape[0]
    past = cache_fox_k.shape[2]
    p_k, p_v, p_f, p_pool, p_mk, p_mv = [], [], [], [], [], []
    s_k, s_v, s_f, s_pool_new = [], [], [], []
    for l in range(DEPTH):
        w = {'g_mix': g_mix[l], 'w_in': w_in[l], 'b_f': b_f[l], 'w_pool': w_pool[l], 's_pool': s_pool[l],
             'w_br_pool': w_br_pool[l], 'w_br_fox': w_br_fox[l], 'w_br_cross': w_br_cross[l],
             'b_gates': b_gates[l], 'w_out': w_out[l], 'g_ffn': g_ffn[l], 'w_router': w_router[l],
             'b_router': b_router[l], 'w_gate_up': w_gate_up[l], 'b_gate_up': b_gate_up[l],
             'w_down': w_down[l], 'b_down': b_down[l]}
        mk, mv = memory_kv(mem_prompt, g_mem[l], w_mem_kv[l])
        empty_kv = jnp.zeros((B, 0, FOX_H, HEAD_DIM), xp.dtype)
        empty_f = jnp.zeros((B, 0, FOX_H), jnp.float32)
        zero_pool = jnp.zeros((B, POOL_HIST, POOL_W), xp.dtype)
        xp, pp, kp, vp, fp = trunk_layer(xp, mk, mv, zero_pool, empty_kv, empty_kv, empty_f, 0, w)
        p_k.append(kp); p_v.append(vp); p_f.append(fp); p_pool.append(pp); p_mk.append(mk); p_mv.append(mv)
        xs, ps, ks_, vs, fs = trunk_layer(xs, cache_mem_k[l], cache_mem_v[l], state_pool[l], cache_fox_k[l],
                                          cache_fox_v[l], cache_fox_logf[l], past, w)
        s_k.append(ks_); s_v.append(vs); s_f.append(fs); s_pool_new.append(ps)
    y_prompt = rmsnorm(xp, g_final)
    y_sample = rmsnorm(xs, g_final)
    return (y_prompt, y_sample, jnp.stack(p_k), jnp.stack(p_v), jnp.stack(p_f), jnp.stack(p_pool),
            jnp.stack(p_mk), jnp.stack(p_mv), jnp.stack(s_k), jnp.stack(s_v), jnp.stack(s_f),
            jnp.stack(s_pool_new))
```

```python
import functools

import jax
import jax.numpy as jnp
import numpy as np
from jax import lax
from jax.experimental import pallas as pl
from jax.experimental.pallas import tpu as pltpu

F32 = jnp.float32
BF16 = jnp.bfloat16

HEAD_DIM = 64
FOX_H = 8
CROSS_H = 4
POOL_WINDOWS = (2, 4, 8, 16)
POOL_HIST = 15
N_BRANCH = 3
TOP_K = 4
SWIGLU_LIMIT = 7.0
SWIGLU_ALPHA = 1.702
RMS_EPS = 1e-5
NEG = -1e30

LANES = 128
HIST_ROWS = 16
VMEM_LIMIT = 56 * 1024 * 1024


def _cparams(n_axes=1, vmem=VMEM_LIMIT):
    return pltpu.CompilerParams(dimension_semantics=("arbitrary",) * n_axes, vmem_limit_bytes=vmem)


def _const_spec(shape):
    nd = len(shape)
    return pl.BlockSpec(shape, lambda *_: (0,) * nd)


def _split3(x):
    hi = x.astype(BF16)
    r = x - hi.astype(F32)
    mid = r.astype(BF16)
    lo = (r - mid.astype(F32)).astype(BF16)
    return hi, mid, lo


def _rms(x, g):
    ms = jnp.mean(x * x, axis=-1, keepdims=True)
    return x * lax.rsqrt(ms + RMS_EPS) * g


def _log_sigmoid(z):
    return jnp.minimum(z, 0.0) - jnp.log1p(jnp.exp(-jnp.abs(z)))


def _dot(a, b):
    return jnp.dot(a, b, preferred_element_type=F32)


def _dot_nt(a, b):
    return lax.dot_general(a, b, (((1,), (1,)), ((), ())), preferred_element_type=F32)


def _tri(n, strict):
    r = np.arange(n)
    m = (r[None, :] < r[:, None]) if strict else (r[None, :] <= r[:, None])
    return jnp.asarray(m.astype(np.float32), dtype=BF16)


def _bias_placement():
    pq = np.zeros((3 * LANES, LANES), np.float32)
    pk = np.zeros((3 * LANES, LANES), np.float32)
    oq = np.zeros((1, LANES), np.float32)
    ok = np.zeros((1, LANES), np.float32)
    for p in range(3):
        for h in range(FOX_H):
            pq[p * LANES + h, 8 * p + h] = 1.0
            pk[p * LANES + h, 24 + 8 * p + h] = -1.0
            oq[0, 24 + 8 * p + h] = 1.0
            ok[0, 8 * p + h] = 1.0
    return (jnp.asarray(pq, dtype=BF16), jnp.asarray(pk, dtype=BF16), jnp.asarray(oq), jnp.asarray(ok))


def _head_lane_masks():
    m = np.zeros((FOX_H, LANES), np.float32)
    for p in range(3):
        for h in range(FOX_H):
            m[h, 8 * p + h] = 1.0
            m[h, 24 + 8 * p + h] = 1.0
    return m


def _pool_lane_windows(pool_w):
    gw = pool_w // len(POOL_WINDOWS)
    return jnp.asarray(np.repeat(np.asarray(POOL_WINDOWS, np.float32), gw)[None, :])


SEC_U, SEC_Q, SEC_K, SEC_V, SEC_CQ, SEC_G, SEC_F = range(7)


def _pool_mix(ext, u, row0, winl, wpool, spool):
    n = u.shape[0]
    s1 = ext + pltpu.roll(ext, 1, 0)
    s2 = s1 + pltpu.roll(s1, 2, 0)
    s3 = s2 + pltpu.roll(s2, 4, 0)
    s4 = s3 + pltpu.roll(s3, 8, 0)
    win = jnp.where(winl == 2.0, s1, jnp.where(winl == 4.0, s2, jnp.where(winl == 8.0, s3, s4)))
    win = win[HIST_ROWS:, :]
    pos = (row0 + lax.broadcasted_iota(jnp.int32, (n, 1), 0)).astype(F32)
    cnt = jnp.minimum(pos + 1.0, winl)
    pooled = win / cnt - u
    return (_dot(pooled.astype(BF16), wpool) * spool).astype(BF16)


def _inproj_kernel(x_ref, g_ref, w_ref, bf_ref, bg_ref, *rest, offs, prompt, tiles_per_batch, tm, n_gate_chunks):
    if prompt:
        (tri_ref, pq_ref, pk_ref, oq_ref, ok_ref, wpool_ref, spool_ref, winl_ref,
         u_ref, po_ref, q_ref, qx_ref, kb_ref, kx_ref, vb_ref, k_ref, v_ref, lf_ref, qc_ref, gt_ref,
         carry_ref, hist_ref) = rest
    else:
        (u_ref, q_ref, k_ref, v_ref, lf_ref, qc_ref, gt_ref) = rest

    h = _rms(x_ref[...], g_ref[...]).astype(BF16)

    def sec(s):
        return _dot(h, w_ref[:, offs[s]:offs[s + 1]])

    u = sec(SEC_U)
    u_ref[...] = u
    q = sec(SEC_Q)
    k = sec(SEC_K)
    v = sec(SEC_V)
    k_ref[...] = k
    v_ref[...] = v
    qc_ref[...] = sec(SEC_CQ).astype(BF16)
    gw = (offs[SEC_G + 1] - offs[SEC_G]) // n_gate_chunks
    for c in range(n_gate_chunks):
        a = offs[SEC_G] + c * gw
        z = _dot(h, w_ref[:, a:a + gw]) + bg_ref[:, c * gw:(c + 1) * gw]
        gt_ref[:, c * gw:(c + 1) * gw] = jax.nn.sigmoid(z).astype(BF16)
    zf = sec(SEC_F) + bf_ref[...]
    lane = lax.broadcasted_iota(jnp.int32, zf.shape, 1)
    logf = jnp.where(lane < FOX_H, _log_sigmoid(zf), 0.0)
    lf_ref[...] = logf[:, :FOX_H]

    if not prompt:
        q_ref[...] = q.astype(BF16)
        return

    tib = pl.program_id(0) % tiles_per_batch

    @pl.when(tib == 0)
    def _():
        carry_ref[...] = jnp.zeros_like(carry_ref)
        hist_ref[...] = jnp.zeros_like(hist_ref)

    n_pair = q.shape[1] // LANES
    for hp in range(n_pair):
        sl = slice(hp * LANES, (hp + 1) * LANES)
        q_ref[hp] = q[:, sl].astype(BF16)
        kb_ref[hp] = k[:, sl].astype(BF16)
        vb_ref[hp] = v[:, sl].astype(BF16)

    tri = tri_ref[...]
    hi, mid, lo = _split3(logf)
    cum = _dot(tri, hi) + _dot(tri, mid) + _dot(tri, lo) + carry_ref[...]
    carry_ref[...] = cum[tm - 1:tm, :]
    cp = jnp.concatenate(_split3(cum), axis=1)
    qx_ref[...] = (_dot(cp, pq_ref[...]) + oq_ref[...]).astype(BF16)
    kx_ref[...] = (_dot(cp, pk_ref[...]) + ok_ref[...]).astype(BF16)

    ext = jnp.concatenate([hist_ref[...], u], axis=0)
    hist_ref[...] = u[tm - HIST_ROWS:, :]
    po_ref[...] = _pool_mix(ext, u, tib * tm, winl_ref[...], wpool_ref[...], spool_ref[...])


def _inproj(x, g_mix, w_cat, b_f, b_g, offs, *, prompt, seq_len=None, consts=None, tm=512):
    t, d = x.shape
    tm = min(tm, t)
    n_tiles = t // tm
    pool_w = offs[SEC_U + 1] - offs[SEC_U]
    fox_w = offs[SEC_Q + 1] - offs[SEC_Q]
    cross_w = offs[SEC_CQ + 1] - offs[SEC_CQ]
    gate_w = offs[SEC_G + 1] - offs[SEC_G]
    n_pair = fox_w // LANES
    row = lambda w: pl.BlockSpec((tm, w), lambda i: (i, 0))
    pair = pl.BlockSpec((n_pair, tm, LANES), lambda i: (0, i, 0))
    in_specs = [row(d), _const_spec((1, d)),
                pl.BlockSpec(w_cat.shape, lambda i: (0, 0), pipeline_mode=pl.Buffered(1)),
                _const_spec((1, LANES)), _const_spec((1, gate_w))]
    args = [x, g_mix, w_cat, b_f, b_g]
    sd = jax.ShapeDtypeStruct
    if prompt:
        tri, pq, pk, oq, ok, wpool, spool, winl = consts
        in_specs += [_const_spec(a.shape) for a in consts]
        args += list(consts)
        out_shape = [sd((t, pool_w), F32), sd((t, pool_w), BF16),
                     sd((n_pair, t, LANES), BF16), sd((t, LANES), BF16),
                     sd((n_pair, t, LANES), BF16), sd((t, LANES), BF16),
                     sd((n_pair, t, LANES), BF16),
                     sd((t, fox_w), F32), sd((t, fox_w), F32), sd((t, FOX_H), F32),
                     sd((t, cross_w), BF16), sd((t, gate_w), BF16)]
        out_specs = [row(pool_w), row(pool_w), pair, row(LANES), pair, row(LANES), pair,
                     row(fox_w), row(fox_w), row(FOX_H), row(cross_w), row(gate_w)]
        scratch = [pltpu.VMEM((1, LANES), F32), pltpu.VMEM((HIST_ROWS, pool_w), F32)]
        tiles_per_batch = seq_len // tm
    else:
        out_shape = [sd((t, pool_w), F32), sd((t, fox_w), BF16), sd((t, fox_w), F32), sd((t, fox_w), F32),
                     sd((t, FOX_H), F32), sd((t, cross_w), BF16), sd((t, gate_w), BF16)]
        out_specs = [row(pool_w), row(fox_w), row(fox_w), row(fox_w), row(FOX_H), row(cross_w), row(gate_w)]
        scratch = []
        tiles_per_batch = 1
    kern = functools.partial(_inproj_kernel, offs=offs, prompt=prompt, tiles_per_batch=tiles_per_batch,
                             tm=tm, n_gate_chunks=N_BRANCH)
    return pl.pallas_call(
        kern, grid=(n_tiles,), in_specs=in_specs, out_specs=out_specs, out_shape=out_shape,
        scratch_shapes=scratch, compiler_params=_cparams(1),
        name="inproj_prompt" if prompt else "inproj_sample")(*args)


def _memkv_kernel(m_ref, g_ref, w_ref, k_ref, v_ref, *, cw):
    h = _rms(m_ref[...], g_ref[...]).astype(BF16)
    kv = _dot(h, w_ref[...])
    k_ref[...] = kv[:, :cw]
    v_ref[...] = kv[:, cw:]


def _memkv(mem, g_mem, w_kv, tm=256):
    t, d = mem.shape
    cw = w_kv.shape[1] // 2
    tm = min(tm, t)
    row = lambda w: pl.BlockSpec((tm, w), lambda i: (i, 0))
    return pl.pallas_call(
        functools.partial(_memkv_kernel, cw=cw), grid=(t // tm,),
        in_specs=[row(d), _const_spec((1, d)), _const_spec(w_kv.shape)],
        out_specs=[row(cw), row(cw)],
        out_shape=[jax.ShapeDtypeStruct((t, cw), F32)] * 2,
        compiler_params=_cparams(1), name="memkv")(mem, g_mem, w_kv)


def _cross_kernel(q_ref, k_ref, v_ref, o_ref):
    q = q_ref[...]
    kk = k_ref[0].astype(BF16)
    vv = v_ref[0].astype(BF16)
    lane = lax.broadcasted_iota(jnp.int32, (1, q.shape[1]), 1) // HEAD_DIM
    out = jnp.zeros(q.shape, F32)
    for h in range(q.shape[1] // HEAD_DIM):
        hm = lane == h
        s = _dot_nt(jnp.where(hm, q, jnp.zeros_like(q)), kk)
        m = jnp.max(s, axis=1, keepdims=True)
        p = jnp.exp(s - m)
        l = jnp.sum(p, axis=1, keepdims=True)
        o = _dot(p.astype(BF16), vv) / l
        out = jnp.where(hm, o, out)
    o_ref[...] = out.astype(BF16)


def _cross(qc, mk, mv, n_batch, tl=512):
    t, cw = qc.shape
    l = t // n_batch
    tl = min(tl, l)
    nl = l // tl
    m = mk.shape[1]
    return pl.pallas_call(
        _cross_kernel, grid=(n_batch, nl),
        in_specs=[pl.BlockSpec((tl, cw), lambda b, i: (b * nl + i, 0)),
                  pl.BlockSpec((1, m, cw), lambda b, i: (b, 0, 0)),
                  pl.BlockSpec((1, m, cw), lambda b, i: (b, 0, 0))],
        out_specs=pl.BlockSpec((tl, cw), lambda b, i: (b * nl + i, 0)),
        out_shape=jax.ShapeDtypeStruct((t, cw), BF16),
        compiler_params=_cparams(2), name="cross_attn")(qc, mk, mv)


def _fox_kernel(hm_ref, q_ref, qx_ref, k_ref, kx_ref, v_ref, o_ref, m_ref, l_ref, acc_ref, *, tq):
    hp = pl.program_id(1)
    qi = pl.program_id(2)
    q2 = q_ref[0]
    qx = qx_ref[...]
    lane = lax.broadcasted_iota(jnp.int32, (1, LANES), 1)
    zero = jnp.zeros_like(q2)
    qs = []
    for a in range(2):
        half = (lane < HEAD_DIM) if a == 0 else (lane >= HEAD_DIM)
        hmask = hm_ref[pl.ds(2 * hp + a, 1), :] > 0.5
        qs.append(jnp.concatenate([jnp.where(half, q2, zero), jnp.where(hmask, qx, zero)], axis=1))
    m_ref[...] = jnp.full_like(m_ref, NEG)
    l_ref[...] = jnp.zeros_like(l_ref)
    acc_ref[...] = jnp.zeros_like(acc_ref)

    def step(ki, masked):
        ks = pl.multiple_of(ki * tq, tq)
        kcat = jnp.concatenate([k_ref[0, pl.ds(ks, tq), :], kx_ref[pl.ds(ks, tq), :]], axis=1)
        v2 = v_ref[0, pl.ds(ks, tq), :]
        for a in range(2):
            s = _dot_nt(qs[a], kcat)
            if masked:
                r = lax.broadcasted_iota(jnp.int32, s.shape, 0)
                c = lax.broadcasted_iota(jnp.int32, s.shape, 1)
                s = jnp.where(c <= r, s, NEG)
            m_old = m_ref[a]
            m_new = jnp.maximum(m_old, jnp.max(s, axis=1, keepdims=True))
            alpha = jnp.exp(m_old - m_new)
            p = jnp.exp(s - m_new)
            l_ref[a] = alpha * l_ref[a] + jnp.sum(p, axis=1, keepdims=True)
            acc_ref[a] = alpha * acc_ref[a] + _dot(p.astype(BF16), v2)
            m_ref[a] = m_new

    def body(ki, carry):
        step(ki, False)
        return carry

    lax.fori_loop(0, qi, body, 0)
    step(qi, True)
    o = jnp.where(lane < HEAD_DIM, acc_ref[0] / l_ref[0], acc_ref[1] / l_ref[1])
    o_ref[...] = o.astype(BF16)


def _fox_prompt(hmask, q, qx, kb, kx, vb, n_batch, tq=512):
    n_pair, t, _ = q.shape
    l = t // n_batch
    tq = min(tq, l)
    nq = l // tq
    once = pl.Buffered(1)
    return pl.pallas_call(
        functools.partial(_fox_kernel, tq=tq), grid=(n_batch, n_pair, nq),
        in_specs=[_const_spec(hmask.shape),
                  pl.BlockSpec((1, tq, LANES), lambda b, h, i: (h, b * nq + i, 0)),
                  pl.BlockSpec((tq, LANES), lambda b, h, i: (b * nq + i, 0)),
                  pl.BlockSpec((1, l, LANES), lambda b, h, i: (h, b, 0), pipeline_mode=once),
                  pl.BlockSpec((l, LANES), lambda b, h, i: (b, 0), pipeline_mode=once),
                  pl.BlockSpec((1, l, LANES), lambda b, h, i: (h, b, 0), pipeline_mode=once)],
        out_specs=pl.BlockSpec((tq, LANES), lambda b, h, i: (b * nq + i, h)),
        out_shape=jax.ShapeDtypeStruct((t, n_pair * LANES), BF16),
        scratch_shapes=[pltpu.VMEM((2, tq, 1), F32), pltpu.VMEM((2, tq, 1), F32), pltpu.VMEM((2, tq, LANES), F32)],
        compiler_params=_cparams(3), name="fox_prompt")(hmask, q, qx, kb, kx, vb)


def _sample_kernel(q_ref, kn_ref, vn_ref, lfn_ref, u_ref, ck_ref, cv_ref, clf_ref, st_ref,
                   tri_ref, pq_ref, pk_ref, oq_ref, ok_ref, qmask_ref, xmask_ref, wpool_ref, spool_ref, winl_ref,
                   fo_ref, po_ref,
                   ext_ref, lf_ref, cum_ref, kcat_ref, vall_ref, *, past, ls, chunk):
    fw = q_ref.shape[1]
    lk_pad = kcat_ref.shape[0]
    n_heads = fw // HEAD_DIM

    u = u_ref[...]
    ext_ref[...] = jnp.zeros_like(ext_ref)
    ext_ref[pl.ds(HIST_ROWS - POOL_HIST, POOL_HIST), :] = st_ref[0]
    ext_ref[pl.ds(HIST_ROWS, ls), :] = u
    po_ref[...] = _pool_mix(ext_ref[...], u, past, winl_ref[...], wpool_ref[...], spool_ref[...])

    lf_ref[...] = jnp.zeros_like(lf_ref)
    lf_ref[pl.ds(0, past), pl.ds(0, FOX_H)] = clf_ref[0]
    lf_ref[pl.ds(past, ls), pl.ds(0, FOX_H)] = lfn_ref[...]
    tri = tri_ref[...]
    carry = jnp.zeros((1, LANES), F32)
    for c in range(lk_pad // chunk):
        rows = pl.ds(c * chunk, chunk)
        hi, mid, lo = _split3(lf_ref[rows, :])
        cum = _dot(tri, hi) + _dot(tri, mid) + _dot(tri, lo) + carry
        carry = cum[chunk - 1:chunk, :]
        cum_ref[rows, :] = cum
        cp = jnp.concatenate(_split3(cum), axis=1)
        kcat_ref[rows, pl.ds(fw, LANES)] = (_dot(cp, pk_ref[...]) + ok_ref[...]).astype(BF16)

    kcat_ref[pl.ds(0, past), pl.ds(0, fw)] = ck_ref[0].astype(BF16)
    kcat_ref[pl.ds(past, ls), pl.ds(0, fw)] = kn_ref[...].astype(BF16)
    vall_ref[pl.ds(0, past), :] = cv_ref[0].astype(BF16)
    vall_ref[pl.ds(past, ls), :] = vn_ref[...].astype(BF16)
    npad = lk_pad - past - ls
    kcat_ref[pl.ds(past + ls, npad), pl.ds(0, fw)] = jnp.zeros((npad, fw), BF16)
    vall_ref[pl.ds(past + ls, npad), :] = jnp.zeros((npad, fw), BF16)

    cq = cum_ref[pl.ds(past, ls), :]
    cpq = jnp.concatenate(_split3(cq), axis=1)
    qx = (_dot(cpq, pq_ref[...]) + oq_ref[...]).astype(BF16)
    q = q_ref[...]
    qbd = jnp.concatenate([jnp.concatenate([q] * n_heads, axis=0) * qmask_ref[...],
                           jnp.concatenate([qx] * n_heads, axis=0) * xmask_ref[...]], axis=1)
    s = _dot_nt(qbd, kcat_ref[...])
    r = lax.broadcasted_iota(jnp.int32, s.shape, 0) % ls
    c = lax.broadcasted_iota(jnp.int32, s.shape, 1)
    s = jnp.where(c <= r + past, s, NEG)
    m = jnp.max(s, axis=1, keepdims=True)
    p = jnp.exp(s - m)
    l = jnp.sum(p, axis=1, keepdims=True)
    o = _dot(p.astype(BF16), vall_ref[...]) / l
    lane_h = lax.broadcasted_iota(jnp.int32, (1, fw), 1) // HEAD_DIM
    out = jnp.zeros((ls, fw), F32)
    for h in range(n_heads):
        out = jnp.where(lane_h == h, o[h * ls:(h + 1) * ls, :], out)
    fo_ref[...] = out.astype(BF16)


def _sample_mix(q, kn, vn, lfn, u, ck, cv, clf, st, consts, chunk=256):
    t, fw = q.shape
    nb, past, _ = ck.shape
    ls = t // nb
    pw = u.shape[1]
    lk_pad = -(-(past + ls) // chunk) * chunk
    if lk_pad == past + ls:
        lk_pad += chunk
    n_heads = fw // HEAD_DIM
    hm = _head_lane_masks()
    qmask = np.zeros((n_heads * ls, fw), np.float32)
    xmask = np.zeros((n_heads * ls, LANES), np.float32)
    for h in range(n_heads):
        qmask[h * ls:(h + 1) * ls, h * HEAD_DIM:(h + 1) * HEAD_DIM] = 1.0
        xmask[h * ls:(h + 1) * ls, :] = hm[h]
    tri, pq, pk, oq, ok, wpool, spool, winl = consts
    tri_c = tri[:chunk, :chunk]
    cargs = [tri_c, pq, pk, oq, ok, jnp.asarray(qmask, dtype=BF16), jnp.asarray(xmask, dtype=BF16), wpool, spool, winl]
    row = lambda w: pl.BlockSpec((ls, w), lambda b: (b, 0))
    bat = lambda a: pl.BlockSpec((1,) + a.shape[1:], lambda b: (b, 0, 0))
    return pl.pallas_call(
        functools.partial(_sample_kernel, past=past, ls=ls, chunk=chunk), grid=(nb,),
        in_specs=[row(fw), row(fw), row(fw), row(FOX_H), row(pw), bat(ck), bat(cv), bat(clf), bat(st)]
                 + [_const_spec(a.shape) for a in cargs],
        out_specs=[row(fw), row(pw)],
        out_shape=[jax.ShapeDtypeStruct((t, fw), BF16), jax.ShapeDtypeStruct((t, pw), BF16)],
        scratch_shapes=[pltpu.VMEM((HIST_ROWS + ls, pw), F32), pltpu.VMEM((lk_pad, LANES), F32),
                        pltpu.VMEM((lk_pad, LANES), F32), pltpu.VMEM((lk_pad, fw + LANES), BF16),
                        pltpu.VMEM((lk_pad, fw), BF16)],
        compiler_params=_cparams(1), name="sample_mix")(q, kn, vn, lfn, u, ck, cv, clf, st, *cargs)


def _merge_kernel(x_ref, po_ref, fo_ref, co_ref, gt_ref, wbp_ref, wbf_ref, wbc_ref, wo_ref, gf_ref,
                  wrh_ref, wrl_ref, br_ref, tri_ref, cin_ref,
                  x1_ref, xn_ref, ri_ref, rw_ref, cnt_ref, carry_ref, *, d, n_exp, tm):
    @pl.when(pl.program_id(0) == 0)
    def _():
        carry_ref[...] = cin_ref[...]

    def gate(j):
        return gt_ref[:, j * d:(j + 1) * d].astype(F32)

    merged = (gate(0) * _dot(po_ref[...], wbp_ref[...])
              + gate(1) * _dot(fo_ref[...], wbf_ref[...])
              + gate(2) * _dot(co_ref[...], wbc_ref[...]))
    x1 = x_ref[...] + _dot(merged.astype(BF16), wo_ref[...])
    x1_ref[...] = x1
    xn = _rms(x1, gf_ref[...])
    xn_ref[...] = xn
    xh = xn.astype(BF16)
    xl = (xn - xh.astype(F32)).astype(BF16)
    logits = _dot(xh, wrh_ref[...]) + _dot(xl, wrh_ref[...]) + _dot(xh, wrl_ref[...]) + br_ref[...]

    lane = lax.broadcasted_iota(jnp.int32, logits.shape, 1)
    lane_f = lane.astype(F32)
    work = jnp.where(lane < n_exp, logits, NEG)
    vals, idxs, sels = [], [], []
    for _ in range(TOP_K):
        mx = jnp.max(work, axis=1, keepdims=True)
        idx = jnp.min(jnp.where(work == mx, lane_f, float(LANES)), axis=1, keepdims=True)
        sel = lane_f == idx
        vals.append(mx)
        idxs.append(idx)
        sels.append(sel)
        work = jnp.where(sel, NEG, work)
    es = [jnp.exp(v - vals[0]) for v in vals]
    den = es[0] + es[1] + es[2] + es[3]

    onehot = jnp.zeros(logits.shape, F32)
    for sel in sels:
        onehot = jnp.where(sel, 1.0, onehot)
    cum = _dot(tri_ref[...], onehot.astype(BF16)) + carry_ref[...]
    carry_ref[...] = cum[tm - 1:tm, :] + onehot[tm - 1:tm, :]
    cnt_ref[...] = carry_ref[...]

    ri = jnp.zeros(logits.shape, jnp.int32)
    rw = jnp.zeros(logits.shape, F32)
    for j in range(TOP_K):
        rank = jnp.sum(jnp.where(sels[j], cum, 0.0), axis=1, keepdims=True)
        ri = jnp.where(lane == j, idxs[j].astype(jnp.int32), ri)
        ri = jnp.where(lane == TOP_K + j, rank.astype(jnp.int32), ri)
        rw = jnp.where(lane == j, es[j] / den, rw)
    ri_ref[...] = ri
    rw_ref[...] = rw


def _merge(x, po, fo, co, gt, wbp, wbf, wbc, wo, g_ffn, wrh, wrl, br, tri_s, cnt_in, n_exp, tm=512):
    t, d = x.shape
    tm = min(tm, t)
    row = lambda a: pl.BlockSpec((tm, a.shape[1]), lambda i: (i, 0))
    consts = [wbp, wbf, wbc, wo, g_ffn, wrh, wrl, br, tri_s, cnt_in]
    sd = jax.ShapeDtypeStruct
    rspec = lambda w: pl.BlockSpec((tm, w), lambda i: (i, 0))
    return pl.pallas_call(
        functools.partial(_merge_kernel, d=d, n_exp=n_exp, tm=tm), grid=(t // tm,),
        in_specs=[row(x), row(po), row(fo), row(co), row(gt)] + [_const_spec(a.shape) for a in consts],
        out_specs=[rspec(d), rspec(d), rspec(LANES), rspec(LANES), _const_spec((1, LANES))],
        out_shape=[sd((t, d), F32), sd((t, d), F32), sd((t, LANES), jnp.int32), sd((t, LANES), F32),
                   sd((1, LANES), F32)],
        scratch_shapes=[pltpu.VMEM((1, LANES), F32)],
        compiler_params=_cparams(1), name="merge_router")(x, po, fo, co, gt, *consts)


def _wprep_kernel(w_ref, g_ref, u_ref, t_ref):
    fc = g_ref.shape[1]
    for c in range(t_ref.shape[0]):
        cols = slice(c * LANES, (c + 1) * LANES)
        t_ref[c] = w_ref[0, cols, :].T
        g_ref[0, :, cols] = t_ref[c, pl.ds(0, fc, stride=2), :].astype(BF16)
        u_ref[0, :, cols] = t_ref[c, pl.ds(1, fc, stride=2), :].astype(BF16)


def _wprep(w_gate_up, fc=256):
    e, d, f2 = w_gate_up.shape
    f = f2 // 2
    fc = min(fc, f)
    out = pl.BlockSpec((1, fc, d), lambda i, c: (i, c, 0))
    return pl.pallas_call(
        _wprep_kernel, grid=(e, f // fc),
        in_specs=[pl.BlockSpec((1, d, 2 * fc), lambda i, c: (i, 0, c))],
        out_specs=[out, out],
        out_shape=[jax.ShapeDtypeStruct((e, f, d), BF16)] * 2,
        scratch_shapes=[pltpu.VMEM((d // LANES, 2 * fc, LANES), F32)],
        compiler_params=_cparams(2), name="expert_weight_layout")(w_gate_up)


def _dispatch_kernel(pos_ref, x_ref, xs_in_ref, xs_ref, sem, *, tm):
    del xs_in_ref

    def copy(t, j):
        return pltpu.make_async_copy(x_ref.at[pl.ds(t, 1), :],
                                     xs_ref.at[pl.ds(pos_ref[t * TOP_K + j], 1), :], sem)

    def issue(t, c):
        for j in range(TOP_K):
            copy(t, j).start()
        return c

    def drain(t, c):
        for j in range(TOP_K):
            copy(t, j).wait()
        return c

    lax.fori_loop(0, tm, issue, 0)
    lax.fori_loop(0, tm, drain, 0)


def _dispatch(pos_flat, xn, xs, tm=256):
    t, d = xn.shape
    tm = min(tm, t)
    return pl.pallas_call(
        functools.partial(_dispatch_kernel, tm=tm), grid=(t // tm,),
        in_specs=[pl.BlockSpec((tm * TOP_K,), lambda i: (i,), memory_space=pltpu.SMEM),
                  pl.BlockSpec((tm, d), lambda i: (i, 0)),
                  pl.BlockSpec(memory_space=pl.ANY)],
        out_specs=pl.BlockSpec(memory_space=pl.ANY),
        out_shape=jax.ShapeDtypeStruct(xs.shape, xs.dtype),
        scratch_shapes=[pltpu.SemaphoreType.DMA],
        input_output_aliases={2: 0},
        compiler_params=_cparams(1), name="dispatch")(pos_flat, xn, xs)


def _ffn_kernel(te_ref, nu_ref, x_ref, wg_ref, wu_ref, wd_ref, bg_ref, bu_ref, bd_ref, y_ref):
    @pl.when(pl.program_id(0) < nu_ref[0])
    def _():
        x = x_ref[...].astype(BF16)
        g = _dot_nt(x, wg_ref[0]) + bg_ref[0]
        u = _dot_nt(x, wu_ref[0]) + bu_ref[0]
        gate = jnp.minimum(g, SWIGLU_LIMIT)
        up = jnp.clip(u, -SWIGLU_LIMIT, SWIGLU_LIMIT)
        act = (up + 1.0) * gate * jax.nn.sigmoid(SWIGLU_ALPHA * gate)
        y_ref[...] = _dot(act.astype(BF16), wd_ref[0]) + bd_ref[0]


def _ffn(tile_exp, n_used, xs, wg_t, wu_t, wd, bg, bu, bd, tm):
    p, d = xs.shape
    e, f, _ = wg_t.shape
    nt = p // tm
    rowi = lambda i, te, nu: (jnp.minimum(i, nu[0] - 1), 0)
    wi = lambda i, te, nu: (te[i], 0, 0)
    grid_spec = pltpu.PrefetchScalarGridSpec(
        num_scalar_prefetch=2, grid=(nt,),
        in_specs=[pl.BlockSpec((tm, d), rowi),
                  pl.BlockSpec((1, f, d), wi), pl.BlockSpec((1, f, d), wi), pl.BlockSpec((1, f, d), wi),
                  pl.BlockSpec((1, 1, f), wi), pl.BlockSpec((1, 1, f), wi), pl.BlockSpec((1, 1, d), wi)],
        out_specs=pl.BlockSpec((tm, d), rowi))
    return pl.pallas_call(
        _ffn_kernel, grid_spec=grid_spec, out_shape=jax.ShapeDtypeStruct((p, d), F32),
        compiler_params=_cparams(1), name="expert_ffn")(tile_exp, n_used, xs, wg_t, wu_t, wd, bg, bu, bd)


def _combine_kernel(pos_ref, w_ref, x1_ref, gf_ref, y_ref, o_ref, buf_ref, sem, *, tm):
    def copy(t, j):
        return pltpu.make_async_copy(y_ref.at[pl.ds(pos_ref[t * TOP_K + j], 1), :],
                                     buf_ref.at[j, pl.ds(t, 1), :], sem)

    def issue(t, c):
        for j in range(TOP_K):
            copy(t, j).start()
        return c

    def drain(t, c):
        for j in range(TOP_K):
            copy(t, j).wait()
        return c

    lax.fori_loop(0, tm, issue, 0)
    lax.fori_loop(0, tm, drain, 0)
    w = w_ref[...]
    acc = x1_ref[...]
    for j in range(TOP_K):
        acc = acc + w[:, j:j + 1] * buf_ref[j]
    o_ref[...] = _rms(acc, gf_ref[...])


def _combine(pos_flat, rw, x1, g_final, y, tm=256):
    t, d = x1.shape
    tm = min(tm, t)
    return pl.pallas_call(
        functools.partial(_combine_kernel, tm=tm), grid=(t // tm,),
        in_specs=[pl.BlockSpec((tm * TOP_K,), lambda i: (i,), memory_space=pltpu.SMEM),
                  pl.BlockSpec((tm, LANES), lambda i: (i, 0)),
                  pl.BlockSpec((tm, d), lambda i: (i, 0)),
                  _const_spec((1, d)),
                  pl.BlockSpec(memory_space=pl.ANY)],
        out_specs=pl.BlockSpec((tm, d), lambda i: (i, 0)),
        out_shape=jax.ShapeDtypeStruct((t, d), F32),
        scratch_shapes=[pltpu.VMEM((TOP_K, tm, d), F32), pltpu.SemaphoreType.DMA],
        compiler_params=_cparams(1), name="combine")(pos_flat, rw, x1, g_final, y)


def kernel(x_prompt, x_sample, mem_prompt, cache_fox_k, cache_fox_v, cache_fox_logf, state_pool, cache_mem_k, cache_mem_v, g_mix, w_in, b_f, w_pool, s_pool, w_br_pool, w_br_fox, w_br_cross, b_gates, w_out, g_mem, w_mem_kv, g_ffn, w_router, b_router, w_gate_up, b_gate_up, w_down, b_down, g_final):
    depth = w_in.shape[0]
    assert depth == 1, "single-layer model"
    bp, lp, d = x_prompt.shape
    bs, ls, _ = x_sample.shape
    past = cache_fox_k.shape[2]
    n_mem = mem_prompt.shape[1]
    n_exp = w_router.shape[2]
    d_ff = w_down.shape[2]
    pool_w = state_pool.shape[3]
    fox_w = FOX_H * HEAD_DIM
    cross_w = CROSS_H * HEAD_DIM
    scale = HEAD_DIM ** -0.5

    w = w_in[0]
    o_q = pool_w
    o_k, o_v, o_f = o_q + fox_w, o_q + 2 * fox_w, o_q + 3 * fox_w
    o_cq = o_f + FOX_H
    o_g = o_cq + cross_w
    w_cat = jnp.concatenate(
        [w[:, :o_q], w[:, o_q:o_k] * scale, w[:, o_k:o_v], w[:, o_v:o_f], w[:, o_cq:o_g] * scale, w[:, o_g:],
         jnp.pad(w[:, o_f:o_cq], ((0, 0), (0, LANES - FOX_H)))], axis=1).astype(BF16)
    widths = [pool_w, fox_w, fox_w, fox_w, cross_w, N_BRANCH * d, LANES]
    offs = tuple(int(v) for v in np.concatenate([[0], np.cumsum(widths)]))
    b_f_p = jnp.pad(b_f[0], (0, LANES - FOX_H)).reshape(1, LANES)
    b_g = b_gates[0].reshape(1, -1)
    g_mix2 = g_mix[0].reshape(1, d)
    gw = pool_w // len(POOL_WINDOWS)
    wpool_bd = jnp.zeros((pool_w, pool_w), F32)
    for g in range(len(POOL_WINDOWS)):
        wpool_bd = wpool_bd.at[g * gw:(g + 1) * gw, g * gw:(g + 1) * gw].set(w_pool[0, g])
    tm = 512
    pq, pk, oq, ok = _bias_placement()
    consts = (_tri(tm, strict=False), pq, pk, oq, ok, wpool_bd.astype(BF16), s_pool[0].reshape(1, pool_w),
              _pool_lane_windows(pool_w))
    hmask = jnp.asarray(_head_lane_masks())

    xp = x_prompt.reshape(bp * lp, d)
    (u_p, po_p, q_p, qx_p, kb_p, kx_p, vb_p, k_p, v_p, lf_p, qc_p, gt_p) = _inproj(
        xp, g_mix2, w_cat, b_f_p, b_g, offs, prompt=True, seq_len=lp, consts=consts, tm=tm)
    mk, mv = _memkv(mem_prompt.reshape(bp * n_mem, d), g_mem[0].reshape(1, d), w_mem_kv[0].astype(BF16))
    mk3, mv3 = mk.reshape(bp, n_mem, cross_w), mv.reshape(bp, n_mem, cross_w)
    fo_p = _fox_prompt(hmask, q_p, qx_p, kb_p, kx_p, vb_p, bp)
    co_p = _cross(qc_p, mk3, mv3, bp)

    xs_ = x_sample.reshape(bs * ls, d)
    (u_s, q_s, k_s, v_s, lf_s, qc_s, gt_s) = _inproj(xs_, g_mix2, w_cat, b_f_p, b_g, offs, prompt=False, tm=tm)
    fo_s, po_s = _sample_mix(q_s, k_s, v_s, lf_s, u_s,
                             cache_fox_k[0].reshape(bs, past, fox_w), cache_fox_v[0].reshape(bs, past, fox_w),
                             cache_fox_logf[0], state_pool[0], consts)
    co_s = _cross(qc_s, cache_mem_k[0].reshape(bs, n_mem, cross_w), cache_mem_v[0].reshape(bs, n_mem, cross_w), bs)

    wbp, wbf, wbc = w_br_pool[0].astype(BF16), w_br_fox[0].astype(BF16), w_br_cross[0].astype(BF16)
    wo = w_out[0].astype(BF16)
    g_ffn2 = g_ffn[0].reshape(1, d)
    wr = jnp.pad(w_router[0], ((0, 0), (0, LANES - n_exp)))
    wrh = wr.astype(BF16)
    wrl = (wr - wrh.astype(F32)).astype(BF16)
    br = jnp.pad(b_router[0], (0, LANES - n_exp)).reshape(1, LANES)
    tri_s = _tri(tm, strict=True)
    margs = (wbp, wbf, wbc, wo, g_ffn2, wrh, wrl, br, tri_s)
    x1_p, xn_p, ri_p, rw_p, cnt_p = _merge(xp, po_p, fo_p, co_p, gt_p, *margs, jnp.zeros((1, LANES), F32), n_exp, tm)
    x1_s, xn_s, ri_s, rw_s, cnt = _merge(xs_, po_s, fo_s, co_s, gt_s, *margs, cnt_p, n_exp, tm)

    tmf = 512
    t_all = bp * lp + bs * ls
    counts = cnt[0, :n_exp].astype(jnp.int32)
    tiles_e = (counts + tmf - 1) // tmf
    tile_end = jnp.cumsum(tiles_e)
    row_off = (tile_end - tiles_e) * tmf
    nt_max = (t_all * TOP_K + n_exp * (tmf - 1)) // tmf + 1
    n_used = tile_end[-1:]
    tile_ids = jnp.minimum(jnp.arange(nt_max, dtype=jnp.int32), n_used[0] - 1)
    tile_exp = jnp.minimum(jnp.sum(tile_ids[:, None] >= tile_end[None, :], axis=1), n_exp - 1).astype(jnp.int32)

    def positions(ri):
        e = ri[:, :TOP_K]
        r = ri[:, TOP_K:2 * TOP_K]
        off = jnp.sum(jnp.where(e[:, :, None] == jnp.arange(n_exp)[None, None, :], row_off[None, None, :], 0), axis=2)
        return (off + r).reshape(-1).astype(jnp.int32)

    pos_p, pos_s = positions(ri_p), positions(ri_s)

    wg_t, wu_t = _wprep(w_gate_up[0])
    wd = w_down[0].astype(BF16)
    bgu = b_gate_up[0]
    bg_e, bu_e = bgu[:, 0::2].reshape(n_exp, 1, d_ff), bgu[:, 1::2].reshape(n_exp, 1, d_ff)
    bd_e = b_down[0].reshape(n_exp, 1, d)
    xs_sorted = jnp.zeros((nt_max * tmf, d), F32)
    xs_sorted = _dispatch(pos_p, xn_p, xs_sorted)
    xs_sorted = _dispatch(pos_s, xn_s, xs_sorted)
    y = _ffn(tile_exp, n_used.astype(jnp.int32), xs_sorted, wg_t, wu_t, wd, bg_e, bu_e, bd_e, tmf)
    g_fin = g_final.reshape(1, d)
    y_p = _combine(pos_p, rw_p, x1_p, g_fin, y)
    y_s = _combine(pos_s, rw_s, x1_s, g_fin, y)

    kv5 = lambda a, b, l: a.reshape(1, b, l, FOX_H, HEAD_DIM)
    return (y_p.reshape(bp, lp, d), y_s.reshape(bs, ls, d),
            kv5(k_p, bp, lp), kv5(v_p, bp, lp), lf_p.reshape(1, bp, lp, FOX_H),
            u_p.reshape(bp, lp, pool_w)[:, lp - POOL_HIST:, :][None],
            mk.reshape(1, bp, n_mem, CROSS_H, HEAD_DIM), mv.reshape(1, bp, n_mem, CROSS_H, HEAD_DIM),
            kv5(k_s, bs, ls), kv5(v_s, bs, ls), lf_s.reshape(1, bs, ls, FOX_H),
            u_s.reshape(bs, ls, pool_w)[:, ls - POOL_HIST:, :][None])
```

```python
import functools

import jax
import jax.numpy as jnp
import numpy as np
from jax import lax
from jax.experimental import pallas as pl
from jax.experimental.pallas import tpu as pltpu

F32 = jnp.float32
BF16 = jnp.bfloat16

HEAD_DIM = 64
FOX_H = 8
CROSS_H = 4
POOL_WINDOWS = (2, 4, 8, 16)
POOL_HIST = 15
N_BRANCH = 3
TOP_K = 4
SWIGLU_LIMIT = 7.0
SWIGLU_ALPHA = 1.702
RMS_EPS = 1e-5
NEG = -1e30
LOG2E = 1.4426950408889634
QSLAB = 256
V_ROWS = HEAD_DIM + 16

LANES = 128
HIST_ROWS = 16
VMEM_LIMIT = 56 * 1024 * 1024


def _cparams(n_axes=1, vmem=VMEM_LIMIT):
    return pltpu.CompilerParams(dimension_semantics=("arbitrary",) * n_axes, vmem_limit_bytes=vmem)


def _const_spec(shape):
    nd = len(shape)
    return pl.BlockSpec(shape, lambda *_: (0,) * nd)


def _split3(x):
    hi = x.astype(BF16)
    r = x - hi.astype(F32)
    mid = r.astype(BF16)
    lo = (r - mid.astype(F32)).astype(BF16)
    return hi, mid, lo


def _rms(x, g):
    ms = jnp.mean(x * x, axis=-1, keepdims=True)
    return x * lax.rsqrt(ms + RMS_EPS) * g


def _log_sigmoid(z):
    return jnp.minimum(z, 0.0) - jnp.log1p(jnp.exp(-jnp.abs(z)))


def _dot(a, b):
    return jnp.dot(a, b, preferred_element_type=F32)


def _dot_nt(a, b):
    return lax.dot_general(a, b, (((1,), (1,)), ((), ())), preferred_element_type=F32)


def _tri(n, strict):
    r = np.arange(n)
    m = (r[None, :] < r[:, None]) if strict else (r[None, :] <= r[:, None])
    return jnp.asarray(m.astype(np.float32), dtype=BF16)


def _bias_placement():
    pq = np.zeros((3 * LANES, LANES), np.float32)
    pk = np.zeros((3 * LANES, LANES), np.float32)
    oq = np.zeros((1, LANES), np.float32)
    ok = np.zeros((1, LANES), np.float32)
    for p in range(3):
        for h in range(FOX_H):
            pq[p * LANES + h, 8 * p + h] = 1.0
            pk[p * LANES + h, 24 + 8 * p + h] = -1.0
            oq[0, 24 + 8 * p + h] = 1.0
            ok[0, 8 * p + h] = 1.0
    return (jnp.asarray(pq, dtype=BF16), jnp.asarray(pk, dtype=BF16), jnp.asarray(oq), jnp.asarray(ok))


def _head_lane_masks():
    m = np.zeros((FOX_H, LANES), np.float32)
    for p in range(3):
        for h in range(FOX_H):
            m[h, 8 * p + h] = 1.0
            m[h, 24 + 8 * p + h] = 1.0
    return m


def _pool_lane_windows(pool_w):
    gw = pool_w // len(POOL_WINDOWS)
    return jnp.asarray(np.repeat(np.asarray(POOL_WINDOWS, np.float32), gw)[None, :])


SEC_U, SEC_Q, SEC_K, SEC_V, SEC_CQ, SEC_G, SEC_F = range(7)


def _pool_mix(ext, u, row0, winl, wpool, spool):
    n = u.shape[0]
    s1 = ext + pltpu.roll(ext, 1, 0)
    s2 = s1 + pltpu.roll(s1, 2, 0)
    s3 = s2 + pltpu.roll(s2, 4, 0)
    s4 = s3 + pltpu.roll(s3, 8, 0)
    win = jnp.where(winl == 2.0, s1, jnp.where(winl == 4.0, s2, jnp.where(winl == 8.0, s3, s4)))
    win = win[HIST_ROWS:, :]
    pos = (row0 + lax.broadcasted_iota(jnp.int32, (n, 1), 0)).astype(F32)
    cnt = jnp.minimum(pos + 1.0, winl)
    pooled = win / cnt - u
    return (_dot(pooled.astype(BF16), wpool) * spool).astype(BF16)


def _inproj_kernel(x_ref, g_ref, w_ref, bf_ref, bg_ref, *rest, offs, prompt, tiles_per_batch, tm, n_gate_chunks):
    if prompt:
        (tri_ref, pq_ref, pk_ref, oq_ref, ok_ref, wpool_ref, spool_ref, winl_ref,
         u_ref, po_ref, q_ref, qx_ref, kb_ref, kx_ref, vb_ref, k_ref, v_ref, lf_ref, qc_ref, gt_ref,
         carry_ref, hist_ref) = rest
    else:
        (u_ref, q_ref, k_ref, v_ref, lf_ref, qc_ref, gt_ref) = rest

    h = _rms(x_ref[...], g_ref[...]).astype(BF16)

    def sec(s):
        return _dot(h, w_ref[:, offs[s]:offs[s + 1]])

    u = sec(SEC_U)
    u_ref[...] = u
    q = sec(SEC_Q)
    k = sec(SEC_K)
    v = sec(SEC_V)
    k_ref[...] = k
    v_ref[...] = v
    qc_ref[...] = sec(SEC_CQ).astype(BF16)
    gw = (offs[SEC_G + 1] - offs[SEC_G]) // n_gate_chunks
    for c in range(n_gate_chunks):
        a = offs[SEC_G] + c * gw
        z = _dot(h, w_ref[:, a:a + gw]) + bg_ref[:, c * gw:(c + 1) * gw]
        gt_ref[:, c * gw:(c + 1) * gw] = jax.nn.sigmoid(z).astype(BF16)
    zf = sec(SEC_F) + bf_ref[...]
    lane = lax.broadcasted_iota(jnp.int32, zf.shape, 1)
    logf = jnp.where(lane < FOX_H, _log_sigmoid(zf), 0.0)
    lf_ref[...] = logf[:, :FOX_H]

    if not prompt:
        q_ref[...] = q.astype(BF16)
        return

    tib = pl.program_id(0) % tiles_per_batch

    @pl.when(tib == 0)
    def _():
        carry_ref[...] = jnp.zeros_like(carry_ref)
        hist_ref[...] = jnp.zeros_like(hist_ref)

    n_pair = q.shape[1] // LANES
    for hp in range(n_pair):
        sl = slice(hp * LANES, (hp + 1) * LANES)
        q_ref[hp] = q[:, sl].T.astype(BF16)
        kb_ref[hp] = k[:, sl].astype(BF16)
        vt = v[:, sl].T.astype(BF16)
        for a in range(2):
            vb_ref[2 * hp + a, 0, 0:HEAD_DIM, :] = vt[a * HEAD_DIM:(a + 1) * HEAD_DIM, :]
            vb_ref[2 * hp + a, 0, HEAD_DIM:V_ROWS, :] = jnp.ones((V_ROWS - HEAD_DIM, tm), BF16)

    tri = tri_ref[...]
    hi, mid, lo = _split3(logf)
    cum = _dot(tri, hi) + _dot(tri, mid) + _dot(tri, lo) + carry_ref[...]
    carry_ref[...] = cum[tm - 1:tm, :]
    cp = jnp.concatenate(_split3(cum * LOG2E), axis=1)
    qx_ref[...] = (_dot(cp, pq_ref[...]) + oq_ref[...]).T.astype(BF16)
    kx_ref[...] = (_dot(cp, pk_ref[...]) + ok_ref[...]).astype(BF16)

    ext = jnp.concatenate([hist_ref[...], u], axis=0)
    hist_ref[...] = u[tm - HIST_ROWS:, :]
    po_ref[...] = _pool_mix(ext, u, tib * tm, winl_ref[...], wpool_ref[...], spool_ref[...])


def _inproj(x, g_mix, w_cat, b_f, b_g, offs, *, prompt, seq_len=None, consts=None, tm=512):
    t, d = x.shape
    tm = min(tm, t)
    n_tiles = t // tm
    pool_w = offs[SEC_U + 1] - offs[SEC_U]
    fox_w = offs[SEC_Q + 1] - offs[SEC_Q]
    cross_w = offs[SEC_CQ + 1] - offs[SEC_CQ]
    gate_w = offs[SEC_G + 1] - offs[SEC_G]
    n_pair = fox_w // LANES
    row = lambda w: pl.BlockSpec((tm, w), lambda i: (i, 0))
    pair = pl.BlockSpec((n_pair, tm, LANES), lambda i: (0, i, 0))
    in_specs = [row(d), _const_spec((1, d)),
                pl.BlockSpec(w_cat.shape, lambda i: (0, 0), pipeline_mode=pl.Buffered(1)),
                _const_spec((1, LANES)), _const_spec((1, gate_w))]
    args = [x, g_mix, w_cat, b_f, b_g]
    sd = jax.ShapeDtypeStruct
    if prompt:
        tri, pq, pk, oq, ok, wpool, spool, winl = consts
        in_specs += [_const_spec(a.shape) for a in consts]
        args += list(consts)
        pair_t = pl.BlockSpec((n_pair, LANES, tm), lambda i: (0, 0, i))
        head_t = pl.BlockSpec((FOX_H, 1, V_ROWS, tm), lambda i: (0, i, 0, 0))
        out_shape = [sd((t, pool_w), F32), sd((t, pool_w), BF16),
                     sd((n_pair, LANES, t), BF16), sd((LANES, t), BF16),
                     sd((n_pair, t, LANES), BF16), sd((t, LANES), BF16),
                     sd((FOX_H, n_tiles, V_ROWS, tm), BF16),
                     sd((t, fox_w), F32), sd((t, fox_w), F32), sd((t, FOX_H), F32),
                     sd((t, cross_w), BF16), sd((t, gate_w), BF16)]
        out_specs = [row(pool_w), row(pool_w), pair_t, pl.BlockSpec((LANES, tm), lambda i: (0, i)),
                     pair, row(LANES), head_t,
                     row(fox_w), row(fox_w), row(FOX_H), row(cross_w), row(gate_w)]
        scratch = [pltpu.VMEM((1, LANES), F32), pltpu.VMEM((HIST_ROWS, pool_w), F32)]
        tiles_per_batch = seq_len // tm
    else:
        out_shape = [sd((t, pool_w), F32), sd((t, fox_w), BF16), sd((t, fox_w), F32), sd((t, fox_w), F32),
                     sd((t, FOX_H), F32), sd((t, cross_w), BF16), sd((t, gate_w), BF16)]
        out_specs = [row(pool_w), row(fox_w), row(fox_w), row(fox_w), row(FOX_H), row(cross_w), row(gate_w)]
        scratch = []
        tiles_per_batch = 1
    kern = functools.partial(_inproj_kernel, offs=offs, prompt=prompt, tiles_per_batch=tiles_per_batch,
                             tm=tm, n_gate_chunks=N_BRANCH)
    return pl.pallas_call(
        kern, grid=(n_tiles,), in_specs=in_specs, out_specs=out_specs, out_shape=out_shape,
        scratch_shapes=scratch, compiler_params=_cparams(1),
        name="inproj_prompt" if prompt else "inproj_sample")(*args)


def _memkv_kernel(m_ref, g_ref, w_ref, k_ref, v_ref, *, cw):
    h = _rms(m_ref[...], g_ref[...]).astype(BF16)
    kv = _dot(h, w_ref[...])
    k_ref[...] = kv[:, :cw]
    v_ref[...] = kv[:, cw:]


def _memkv(mem, g_mem, w_kv, tm=256):
    t, d = mem.shape
    cw = w_kv.shape[1] // 2
    tm = min(tm, t)
    row = lambda w: pl.BlockSpec((tm, w), lambda i: (i, 0))
    return pl.pallas_call(
        functools.partial(_memkv_kernel, cw=cw), grid=(t // tm,),
        in_specs=[row(d), _const_spec((1, d)), _const_spec(w_kv.shape)],
        out_specs=[row(cw), row(cw)],
        out_shape=[jax.ShapeDtypeStruct((t, cw), F32)] * 2,
        compiler_params=_cparams(1), name="memkv")(mem, g_mem, w_kv)


def _cross_kernel(q_ref, k_ref, v_ref, o_ref):
    q = q_ref[...]
    kk = k_ref[0].astype(BF16)
    vv = v_ref[0].astype(BF16)
    lane = lax.broadcasted_iota(jnp.int32, (1, q.shape[1]), 1) // HEAD_DIM
    out = jnp.zeros(q.shape, F32)
    for h in range(q.shape[1] // HEAD_DIM):
        hm = lane == h
        s = _dot_nt(jnp.where(hm, q, jnp.zeros_like(q)), kk)
        m = jnp.max(s, axis=1, keepdims=True)
        p = jnp.exp(s - m)
        l = jnp.sum(p, axis=1, keepdims=True)
        o = _dot(p.astype(BF16), vv) / l
        out = jnp.where(hm, o, out)
    o_ref[...] = out.astype(BF16)


def _cross(qc, mk, mv, n_batch, tl=512):
    t, cw = qc.shape
    l = t // n_batch
    tl = min(tl, l)
    nl = l // tl
    m = mk.shape[1]
    return pl.pallas_call(
        _cross_kernel, grid=(n_batch, nl),
        in_specs=[pl.BlockSpec((tl, cw), lambda b, i: (b * nl + i, 0)),
                  pl.BlockSpec((1, m, cw), lambda b, i: (b, 0, 0)),
                  pl.BlockSpec((1, m, cw), lambda b, i: (b, 0, 0))],
        out_specs=pl.BlockSpec((tl, cw), lambda b, i: (b * nl + i, 0)),
        out_shape=jax.ShapeDtypeStruct((t, cw), BF16),
        compiler_params=_cparams(2), name="cross_attn")(qc, mk, mv)


def _fox_kernel(hm_ref, q_ref, qx_ref, k_ref, kx_ref, v_ref, o_ref, m0_ref, m1_ref, acc0_ref, acc1_ref, s0_ref, *, tq):
    qi = pl.program_id(2)
    qt = q_ref[0]
    qxt = qx_ref[...]
    row = lax.broadcasted_iota(jnp.int32, (LANES, 1), 0)
    zero = jnp.zeros_like(qt)
    qs = []
    for a in range(2):
        half = (row < HEAD_DIM) if a == 0 else (row >= HEAD_DIM)
        qs.append(jnp.concatenate([jnp.where(half, qt, zero), jnp.where(hm_ref[a] > 0.5, qxt, zero)], axis=0))
    ms = (m0_ref, m1_ref)
    accs = (acc0_ref, acc1_ref)
    for a in range(2):
        ms[a][...] = jnp.full_like(ms[a], NEG)
        accs[a][...] = jnp.zeros_like(accs[a])

    def scores(a, ki):
        ks = pl.multiple_of(ki * tq, tq)
        kcat = jnp.concatenate([k_ref[0, pl.ds(ks, tq), :], kx_ref[pl.ds(ks, tq), :]], axis=1)
        return _dot(kcat, qs[a])

    def softmax_pv(a, s, ki, masked):
        if masked:
            r = lax.broadcasted_iota(jnp.int32, s.shape, 0)
            c = lax.broadcasted_iota(jnp.int32, s.shape, 1)
            s = jnp.where(r <= c, s, NEG)
        m_old = ms[a][...]
        m_new = jnp.maximum(m_old, jnp.max(s, axis=0, keepdims=True))
        alpha = jnp.exp2(m_old - m_new)
        p = jnp.exp2(s - m_new).astype(BF16)
        accs[a][...] = alpha * accs[a][...] + _dot(v_ref[a, ki], p)
        ms[a][...] = m_new

    s0_ref[...] = scores(0, 0)

    def body(ki, carry):
        s1 = scores(1, ki)
        softmax_pv(0, s0_ref[...], ki, False)
        s0_ref[...] = scores(0, ki + 1)
        softmax_pv(1, s1, ki, False)
        return carry

    lax.fori_loop(0, qi, body, 0)
    s1 = scores(1, qi)
    softmax_pv(0, s0_ref[...], qi, True)
    softmax_pv(1, s1, qi, True)
    outs = [accs[a][0:HEAD_DIM, :] / accs[a][HEAD_DIM:HEAD_DIM + 1, :] for a in range(2)]
    o_ref[...] = jnp.concatenate(outs, axis=0).T.astype(BF16)


def _fox_prompt(hmask, q, qx, kb, kx, vb, n_batch):
    n_pair, _, t = q.shape
    _, n_tiles, _, tq = vb.shape
    l = t // n_batch
    nq = l // tq
    once = pl.Buffered(1)
    return pl.pallas_call(
        functools.partial(_fox_kernel, tq=tq), grid=(n_batch, n_pair, nq),
        in_specs=[pl.BlockSpec((2, LANES, 1), lambda b, h, i: (h, 0, 0)),
                  pl.BlockSpec((1, LANES, tq), lambda b, h, i: (h, 0, b * nq + i)),
                  pl.BlockSpec((LANES, tq), lambda b, h, i: (0, b * nq + i)),
                  pl.BlockSpec((1, l, LANES), lambda b, h, i: (h, b, 0), pipeline_mode=once),
                  pl.BlockSpec((l, LANES), lambda b, h, i: (b, 0), pipeline_mode=once),
                  pl.BlockSpec((2, nq, V_ROWS, tq), lambda b, h, i: (h, b, 0, 0), pipeline_mode=once)],
        out_specs=pl.BlockSpec((tq, LANES), lambda b, h, i: (b * nq + i, h)),
        out_shape=jax.ShapeDtypeStruct((t, n_pair * LANES), BF16),
        scratch_shapes=[pltpu.VMEM((1, tq), F32), pltpu.VMEM((1, tq), F32),
                        pltpu.VMEM((V_ROWS, tq), F32), pltpu.VMEM((V_ROWS, tq), F32), pltpu.VMEM((tq, tq), F32)],
        compiler_params=_cparams(3), name="fox_prompt")(hmask, q, qx, kb, kx, vb)


def _sample_kernel(q_ref, kn_ref, vn_ref, lfn_ref, u_ref, ck_ref, cv_ref, clf_ref, st_ref,
                   tri_ref, pq_ref, pk_ref, oq_ref, ok_ref, qmask_ref, xmask_ref, wpool_ref, spool_ref, winl_ref,
                   fo_ref, po_ref,
                   ext_ref, lf_ref, cum_ref, kcat_ref, vall_ref, *, past, ls, chunk):
    fw = q_ref.shape[1]
    lk_pad = kcat_ref.shape[0]
    n_heads = fw // HEAD_DIM

    u = u_ref[...]
    ext_ref[...] = jnp.zeros_like(ext_ref)
    ext_ref[pl.ds(HIST_ROWS - POOL_HIST, POOL_HIST), :] = st_ref[0]
    ext_ref[pl.ds(HIST_ROWS, ls), :] = u
    po_ref[...] = _pool_mix(ext_ref[...], u, past, winl_ref[...], wpool_ref[...], spool_ref[...])

    lf_ref[...] = jnp.zeros_like(lf_ref)
    lf_ref[pl.ds(0, past), pl.ds(0, FOX_H)] = clf_ref[0]
    lf_ref[pl.ds(past, ls), pl.ds(0, FOX_H)] = lfn_ref[...]
    tri = tri_ref[...]
    carry = jnp.zeros((1, LANES), F32)
    for c in range(lk_pad // chunk):
        rows = pl.ds(c * chunk, chunk)
        hi, mid, lo = _split3(lf_ref[rows, :])
        cum = _dot(tri, hi) + _dot(tri, mid) + _dot(tri, lo) + carry
        carry = cum[chunk - 1:chunk, :]
        cum_ref[rows, :] = cum
        cp = jnp.concatenate(_split3(cum * LOG2E), axis=1)
        kcat_ref[rows, pl.ds(fw, LANES)] = (_dot(cp, pk_ref[...]) + ok_ref[...]).astype(BF16)

    kcat_ref[pl.ds(0, past), pl.ds(0, fw)] = ck_ref[0].astype(BF16)
    kcat_ref[pl.ds(past, ls), pl.ds(0, fw)] = kn_ref[...].astype(BF16)
    vall_ref[pl.ds(0, past), :] = cv_ref[0].astype(BF16)
    vall_ref[pl.ds(past, ls), :] = vn_ref[...].astype(BF16)
    npad = lk_pad - past - ls
    kcat_ref[pl.ds(past + ls, npad), pl.ds(0, fw)] = jnp.zeros((npad, fw), BF16)
    vall_ref[pl.ds(past + ls, npad), :] = jnp.zeros((npad, fw), BF16)

    cq = cum_ref[pl.ds(past, ls), :]
    cpq = jnp.concatenate(_split3(cq * LOG2E), axis=1)
    qx = (_dot(cpq, pq_ref[...]) + oq_ref[...]).astype(BF16)
    q = q_ref[...]
    qbd = jnp.concatenate([jnp.concatenate([q] * n_heads, axis=0) * qmask_ref[...],
                           jnp.concatenate([qx] * n_heads, axis=0) * xmask_ref[...]], axis=1)
    s = _dot_nt(qbd, kcat_ref[...])
    r = lax.broadcasted_iota(jnp.int32, s.shape, 0) % ls
    c = lax.broadcasted_iota(jnp.int32, s.shape, 1)
    s = jnp.where(c <= r + past, s, NEG)
    m = jnp.max(s, axis=1, keepdims=True)
    p = jnp.exp2(s - m)
    l = jnp.sum(p, axis=1, keepdims=True)
    o = _dot(p.astype(BF16), vall_ref[...]) / l
    lane_h = lax.broadcasted_iota(jnp.int32, (1, fw), 1) // HEAD_DIM
    out = jnp.zeros((ls, fw), F32)
    for h in range(n_heads):
        out = jnp.where(lane_h == h, o[h * ls:(h + 1) * ls, :], out)
    fo_ref[...] = out.astype(BF16)


def _sample_mix(q, kn, vn, lfn, u, ck, cv, clf, st, consts, chunk=256):
    t, fw = q.shape
    nb, past, _ = ck.shape
    ls = t // nb
    pw = u.shape[1]
    lk_pad = -(-(past + ls) // chunk) * chunk
    if lk_pad == past + ls:
        lk_pad += chunk
    n_heads = fw // HEAD_DIM
    hm = _head_lane_masks()
    qmask = np.zeros((n_heads * ls, fw), np.float32)
    xmask = np.zeros((n_heads * ls, LANES), np.float32)
    for h in range(n_heads):
        qmask[h * ls:(h + 1) * ls, h * HEAD_DIM:(h + 1) * HEAD_DIM] = 1.0
        xmask[h * ls:(h + 1) * ls, :] = hm[h]
    tri, pq, pk, oq, ok, wpool, spool, winl = consts
    tri_c = tri[:chunk, :chunk]
    cargs = [tri_c, pq, pk, oq, ok, jnp.asarray(qmask, dtype=BF16), jnp.asarray(xmask, dtype=BF16), wpool, spool, winl]
    row = lambda w: pl.BlockSpec((ls, w), lambda b: (b, 0))
    bat = lambda a: pl.BlockSpec((1,) + a.shape[1:], lambda b: (b, 0, 0))
    return pl.pallas_call(
        functools.partial(_sample_kernel, past=past, ls=ls, chunk=chunk), grid=(nb,),
        in_specs=[row(fw), row(fw), row(fw), row(FOX_H), row(pw), bat(ck), bat(cv), bat(clf), bat(st)]
                 + [_const_spec(a.shape) for a in cargs],
        out_specs=[row(fw), row(pw)],
        out_shape=[jax.ShapeDtypeStruct((t, fw), BF16), jax.ShapeDtypeStruct((t, pw), BF16)],
        scratch_shapes=[pltpu.VMEM((HIST_ROWS + ls, pw), F32), pltpu.VMEM((lk_pad, LANES), F32),
                        pltpu.VMEM((lk_pad, LANES), F32), pltpu.VMEM((lk_pad, fw + LANES), BF16),
                        pltpu.VMEM((lk_pad, fw), BF16)],
        compiler_params=_cparams(1), name="sample_mix")(q, kn, vn, lfn, u, ck, cv, clf, st, *cargs)


def _merge_kernel(x_ref, po_ref, fo_ref, co_ref, gt_ref, wbp_ref, wbf_ref, wbc_ref, wo_ref, gf_ref,
                  wrh_ref, wrl_ref, br_ref, tri_ref, cin_ref,
                  x1_ref, xn_ref, ri_ref, rw_ref, cnt_ref, carry_ref, *, d, n_exp, tm):
    @pl.when(pl.program_id(0) == 0)
    def _():
        carry_ref[...] = cin_ref[...]

    def gate(j):
        return gt_ref[:, j * d:(j + 1) * d].astype(F32)

    merged = (gate(0) * _dot(po_ref[...], wbp_ref[...])
              + gate(1) * _dot(fo_ref[...], wbf_ref[...])
              + gate(2) * _dot(co_ref[...], wbc_ref[...]))
    x1 = x_ref[...] + _dot(merged.astype(BF16), wo_ref[...])
    x1_ref[...] = x1
    xn = _rms(x1, gf_ref[...])
    xn_ref[...] = xn
    xh = xn.astype(BF16)
    xl = (xn - xh.astype(F32)).astype(BF16)
    logits = _dot(xh, wrh_ref[...]) + _dot(xl, wrh_ref[...]) + _dot(xh, wrl_ref[...]) + br_ref[...]

    lane = lax.broadcasted_iota(jnp.int32, logits.shape, 1)
    lane_f = lane.astype(F32)
    work = jnp.where(lane < n_exp, logits, NEG)
    vals, idxs, sels = [], [], []
    for _ in range(TOP_K):
        mx = jnp.max(work, axis=1, keepdims=True)
        idx = jnp.min(jnp.where(work == mx, lane_f, float(LANES)), axis=1, keepdims=True)
        sel = lane_f == idx
        vals.append(mx)
        idxs.append(idx)
        sels.append(sel)
        work = jnp.where(sel, NEG, work)
    es = [jnp.exp(v - vals[0]) for v in vals]
    den = es[0] + es[1] + es[2] + es[3]

    onehot = jnp.zeros(logits.shape, F32)
    for sel in sels:
        onehot = jnp.where(sel, 1.0, onehot)
    cum = _dot(tri_ref[...], onehot.astype(BF16)) + carry_ref[...]
    carry_ref[...] = cum[tm - 1:tm, :] + onehot[tm - 1:tm, :]
    cnt_ref[...] = carry_ref[...]

    ri = jnp.zeros(logits.shape, jnp.int32)
    rw = jnp.zeros(logits.shape, F32)
    for j in range(TOP_K):
        rank = jnp.sum(jnp.where(sels[j], cum, 0.0), axis=1, keepdims=True)
        ri = jnp.where(lane == j, idxs[j].astype(jnp.int32), ri)
        ri = jnp.where(lane == TOP_K + j, rank.astype(jnp.int32), ri)
        rw = jnp.where(lane == j, es[j] / den, rw)
    ri_ref[...] = ri
    rw_ref[...] = rw


def _merge(x, po, fo, co, gt, wbp, wbf, wbc, wo, g_ffn, wrh, wrl, br, tri_s, cnt_in, n_exp, tm=512):
    t, d = x.shape
    tm = min(tm, t)
    row = lambda a: pl.BlockSpec((tm, a.shape[1]), lambda i: (i, 0))
    consts = [wbp, wbf, wbc, wo, g_ffn, wrh, wrl, br, tri_s, cnt_in]
    sd = jax.ShapeDtypeStruct
    rspec = lambda w: pl.BlockSpec((tm, w), lambda i: (i, 0))
    return pl.pallas_call(
        functools.partial(_merge_kernel, d=d, n_exp=n_exp, tm=tm), grid=(t // tm,),
        in_specs=[row(x), row(po), row(fo), row(co), row(gt)] + [_const_spec(a.shape) for a in consts],
        out_specs=[rspec(d), rspec(d), rspec(LANES), rspec(LANES), _const_spec((1, LANES))],
        out_shape=[sd((t, d), F32), sd((t, d), F32), sd((t, LANES), jnp.int32), sd((t, LANES), F32),
                   sd((1, LANES), F32)],
        scratch_shapes=[pltpu.VMEM((1, LANES), F32)],
        compiler_params=_cparams(1), name="merge_router")(x, po, fo, co, gt, *consts)


def _wprep_kernel(w_ref, g_ref, u_ref, t_ref):
    fc = g_ref.shape[1]
    for c in range(t_ref.shape[0]):
        cols = slice(c * LANES, (c + 1) * LANES)
        t_ref[c] = w_ref[0, cols, :].T
        g_ref[0, :, cols] = t_ref[c, pl.ds(0, fc, stride=2), :].astype(BF16)
        u_ref[0, :, cols] = t_ref[c, pl.ds(1, fc, stride=2), :].astype(BF16)


def _wprep(w_gate_up, fc=256):
    e, d, f2 = w_gate_up.shape
    f = f2 // 2
    fc = min(fc, f)
    out = pl.BlockSpec((1, fc, d), lambda i, c: (i, c, 0))
    return pl.pallas_call(
        _wprep_kernel, grid=(e, f // fc),
        in_specs=[pl.BlockSpec((1, d, 2 * fc), lambda i, c: (i, 0, c))],
        out_specs=[out, out],
        out_shape=[jax.ShapeDtypeStruct((e, f, d), BF16)] * 2,
        scratch_shapes=[pltpu.VMEM((d // LANES, 2 * fc, LANES), F32)],
        compiler_params=_cparams(2), name="expert_weight_layout")(w_gate_up)


def _zero_pad_rows(ps_ref, pl_ref, xs_ref, z_ref, sem, n_exp, n_bits):
    z_ref[...] = jnp.zeros_like(z_ref)
    sub = 8

    def each(fn):
        for e in range(n_exp):
            start = ps_ref[e]
            n = pl_ref[e]
            head = jnp.minimum((-start) & (sub - 1), n)
            for r in range(sub - 1):
                @pl.when(r < head)
                def _():
                    fn(pltpu.make_async_copy(z_ref.at[pl.ds(0, 1), :], xs_ref.at[pl.ds(start + r, 1), :], sem))
            start8 = start + head
            n8 = (n - head) // sub
            for b in range(n_bits - 3):
                rows = sub << b
                off = pl.multiple_of(start8 + sub * (n8 & ((1 << b) - 1)), sub)

                @pl.when(((n8 >> b) & 1) == 1)
                def _():
                    fn(pltpu.make_async_copy(z_ref.at[pl.ds(0, rows), :], xs_ref.at[pl.ds(off, rows), :], sem))

    each(lambda c: c.start())
    each(lambda c: c.wait())


def _dispatch_kernel(pos_ref, x_ref, *rest, tm, first, n_exp, n_bits):
    if first:
        ps_ref, pl_ref, xs_ref, sem, z_ref = rest

        @pl.when(pl.program_id(0) == 0)
        def _():
            _zero_pad_rows(ps_ref, pl_ref, xs_ref, z_ref, sem, n_exp, n_bits)
    else:
        _, xs_ref, sem = rest

    def copy(t, j):
        return pltpu.make_async_copy(x_ref.at[pl.ds(t, 1), :],
                                     xs_ref.at[pl.ds(pos_ref[t * TOP_K + j], 1), :], sem)

    def issue(t, c):
        for j in range(TOP_K):
            copy(t, j).start()
        return c

    def drain(t, c):
        for j in range(TOP_K):
            copy(t, j).wait()
        return c

    lax.fori_loop(0, tm, issue, 0)
    lax.fori_loop(0, tm, drain, 0)


def _dispatch(pos_flat, xn, n_rows, pad=None, xs=None, tm=256, tile_rows=512):
    t, d = xn.shape
    tm = min(tm, t)
    first = xs is None
    n_bits = (tile_rows - 1).bit_length()
    smem = lambda n: pl.BlockSpec((n,), lambda i: (0,), memory_space=pltpu.SMEM)
    in_specs = [pl.BlockSpec((tm * TOP_K,), lambda i: (i,), memory_space=pltpu.SMEM),
                pl.BlockSpec((tm, d), lambda i: (i, 0))]
    scratch = [pltpu.SemaphoreType.DMA]
    if first:
        n_exp = pad[0].shape[0]
        in_specs += [smem(n_exp), smem(n_exp)]
        args = (pos_flat, xn, pad[0], pad[1])
        scratch.append(pltpu.VMEM((1 << (n_bits - 1), d), F32))
        aliases = {}
    else:
        n_exp = 0
        in_specs.append(pl.BlockSpec(memory_space=pl.ANY))
        args = (pos_flat, xn, xs)
        aliases = {2: 0}
    return pl.pallas_call(
        functools.partial(_dispatch_kernel, tm=tm, first=first, n_exp=n_exp, n_bits=n_bits), grid=(t // tm,),
        in_specs=in_specs,
        out_specs=pl.BlockSpec(memory_space=pl.ANY),
        out_shape=jax.ShapeDtypeStruct((n_rows, d), F32),
        scratch_shapes=scratch,
        input_output_aliases=aliases,
        compiler_params=_cparams(1), name="dispatch")(*args)


def _ffn_kernel(te_ref, nu_ref, x_ref, wg_ref, wu_ref, wd_ref, bg_ref, bu_ref, bd_ref, y_ref):
    @pl.when(pl.program_id(0) < nu_ref[0])
    def _():
        x = x_ref[...].astype(BF16)
        g = _dot_nt(x, wg_ref[0]) + bg_ref[0]
        u = _dot_nt(x, wu_ref[0]) + bu_ref[0]
        gate = jnp.minimum(g, SWIGLU_LIMIT)
        up = jnp.clip(u, -SWIGLU_LIMIT, SWIGLU_LIMIT)
        act = (up + 1.0) * gate * jax.nn.sigmoid(SWIGLU_ALPHA * gate)
        y_ref[...] = _dot(act.astype(BF16), wd_ref[0]) + bd_ref[0]


def _ffn(tile_exp, n_used, xs, wg_t, wu_t, wd, bg, bu, bd, tm):
    p, d = xs.shape
    e, f, _ = wg_t.shape
    nt = p // tm
    rowi = lambda i, te, nu: (jnp.minimum(i, nu[0] - 1), 0)
    wi = lambda i, te, nu: (te[i], 0, 0)
    grid_spec = pltpu.PrefetchScalarGridSpec(
        num_scalar_prefetch=2, grid=(nt,),
        in_specs=[pl.BlockSpec((tm, d), rowi),
                  pl.BlockSpec((1, f, d), wi), pl.BlockSpec((1, f, d), wi), pl.BlockSpec((1, f, d), wi),
                  pl.BlockSpec((1, 1, f), wi), pl.BlockSpec((1, 1, f), wi), pl.BlockSpec((1, 1, d), wi)],
        out_specs=pl.BlockSpec((tm, d), rowi))
    return pl.pallas_call(
        _ffn_kernel, grid_spec=grid_spec, out_shape=jax.ShapeDtypeStruct((p, d), F32),
        compiler_params=_cparams(1), name="expert_ffn")(tile_exp, n_used, xs, wg_t, wu_t, wd, bg, bu, bd)


def _combine_kernel(pos_ref, w_ref, x1_ref, gf_ref, y_ref, o_ref, buf_ref, sem, *, tm):
    def copy(t, j):
        return pltpu.make_async_copy(y_ref.at[pl.ds(pos_ref[t * TOP_K + j], 1), :],
                                     buf_ref.at[j, pl.ds(t, 1), :], sem)

    def issue(t, c):
        for j in range(TOP_K):
            copy(t, j).start()
        return c

    def drain(t, c):
        for j in range(TOP_K):
            copy(t, j).wait()
        return c

    lax.fori_loop(0, tm, issue, 0)
    lax.fori_loop(0, tm, drain, 0)
    w = w_ref[...]
    acc = x1_ref[...]
    for j in range(TOP_K):
        acc = acc + w[:, j:j + 1] * buf_ref[j]
    o_ref[...] = _rms(acc, gf_ref[...])


def _combine(pos_flat, rw, x1, g_final, y, tm=256):
    t, d = x1.shape
    tm = min(tm, t)
    return pl.pallas_call(
        functools.partial(_combine_kernel, tm=tm), grid=(t // tm,),
        in_specs=[pl.BlockSpec((tm * TOP_K,), lambda i: (i,), memory_space=pltpu.SMEM),
                  pl.BlockSpec((tm, LANES), lambda i: (i, 0)),
                  pl.BlockSpec((tm, d), lambda i: (i, 0)),
                  _const_spec((1, d)),
                  pl.BlockSpec(memory_space=pl.ANY)],
        out_specs=pl.BlockSpec((tm, d), lambda i: (i, 0)),
        out_shape=jax.ShapeDtypeStruct((t, d), F32),
        scratch_shapes=[pltpu.VMEM((TOP_K, tm, d), F32), pltpu.SemaphoreType.DMA],
        compiler_params=_cparams(1), name="combine")(pos_flat, rw, x1, g_final, y)


def kernel(x_prompt, x_sample, mem_prompt, cache_fox_k, cache_fox_v, cache_fox_logf, state_pool, cache_mem_k, cache_mem_v, g_mix, w_in, b_f, w_pool, s_pool, w_br_pool, w_br_fox, w_br_cross, b_gates, w_out, g_mem, w_mem_kv, g_ffn, w_router, b_router, w_gate_up, b_gate_up, w_down, b_down, g_final):
    depth = w_in.shape[0]
    assert depth == 1, "single-layer model"
    bp, lp, d = x_prompt.shape
    bs, ls, _ = x_sample.shape
    past = cache_fox_k.shape[2]
    n_mem = mem_prompt.shape[1]
    n_exp = w_router.shape[2]
    d_ff = w_down.shape[2]
    pool_w = state_pool.shape[3]
    fox_w = FOX_H * HEAD_DIM
    cross_w = CROSS_H * HEAD_DIM
    scale = HEAD_DIM ** -0.5

    w = w_in[0]
    o_q = pool_w
    o_k, o_v, o_f = o_q + fox_w, o_q + 2 * fox_w, o_q + 3 * fox_w
    o_cq = o_f + FOX_H
    o_g = o_cq + cross_w
    w_cat = jnp.concatenate(
        [w[:, :o_q], w[:, o_q:o_k] * (scale * LOG2E), w[:, o_k:o_v], w[:, o_v:o_f], w[:, o_cq:o_g] * scale, w[:, o_g:],
         jnp.pad(w[:, o_f:o_cq], ((0, 0), (0, LANES - FOX_H)))], axis=1).astype(BF16)
    widths = [pool_w, fox_w, fox_w, fox_w, cross_w, N_BRANCH * d, LANES]
    offs = tuple(int(v) for v in np.concatenate([[0], np.cumsum(widths)]))
    b_f_p = jnp.pad(b_f[0], (0, LANES - FOX_H)).reshape(1, LANES)
    b_g = b_gates[0].reshape(1, -1)
    g_mix2 = g_mix[0].reshape(1, d)
    gw = pool_w // len(POOL_WINDOWS)
    wpool_bd = jnp.zeros((pool_w, pool_w), F32)
    for g in range(len(POOL_WINDOWS)):
        wpool_bd = wpool_bd.at[g * gw:(g + 1) * gw, g * gw:(g + 1) * gw].set(w_pool[0, g])
    tm = 512
    pq, pk, oq, ok = _bias_placement()
    consts = (_tri(tm, strict=False), pq, pk, oq, ok, wpool_bd.astype(BF16), s_pool[0].reshape(1, pool_w),
              _pool_lane_windows(pool_w))
    hmask = jnp.asarray(_head_lane_masks()[:, :, None])

    xp = x_prompt.reshape(bp * lp, d)
    (u_p, po_p, q_p, qx_p, kb_p, kx_p, vb_p, k_p, v_p, lf_p, qc_p, gt_p) = _inproj(
        xp, g_mix2, w_cat, b_f_p, b_g, offs, prompt=True, seq_len=lp, consts=consts, tm=tm)
    mk, mv = _memkv(mem_prompt.reshape(bp * n_mem, d), g_mem[0].reshape(1, d), w_mem_kv[0].astype(BF16))
    mk3, mv3 = mk.reshape(bp, n_mem, cross_w), mv.reshape(bp, n_mem, cross_w)
    fo_p = _fox_prompt(hmask, q_p, qx_p, kb_p, kx_p, vb_p, bp)
    co_p = _cross(qc_p, mk3, mv3, bp)

    xs_ = x_sample.reshape(bs * ls, d)
    (u_s, q_s, k_s, v_s, lf_s, qc_s, gt_s) = _inproj(xs_, g_mix2, w_cat, b_f_p, b_g, offs, prompt=False, tm=tm)
    fo_s, po_s = _sample_mix(q_s, k_s, v_s, lf_s, u_s,
                             cache_fox_k[0].reshape(bs, past, fox_w), cache_fox_v[0].reshape(bs, past, fox_w),
                             cache_fox_logf[0], state_pool[0], consts)
    co_s = _cross(qc_s, cache_mem_k[0].reshape(bs, n_mem, cross_w), cache_mem_v[0].reshape(bs, n_mem, cross_w), bs)

    wbp, wbf, wbc = w_br_pool[0].astype(BF16), w_br_fox[0].astype(BF16), w_br_cross[0].astype(BF16)
    wo = w_out[0].astype(BF16)
    g_ffn2 = g_ffn[0].reshape(1, d)
    wr = jnp.pad(w_router[0], ((0, 0), (0, LANES - n_exp)))
    wrh = wr.astype(BF16)
    wrl = (wr - wrh.astype(F32)).astype(BF16)
    br = jnp.pad(b_router[0], (0, LANES - n_exp)).reshape(1, LANES)
    tri_s = _tri(tm, strict=True)
    margs = (wbp, wbf, wbc, wo, g_ffn2, wrh, wrl, br, tri_s)
    x1_p, xn_p, ri_p, rw_p, cnt_p = _merge(xp, po_p, fo_p, co_p, gt_p, *margs, jnp.zeros((1, LANES), F32), n_exp, tm)
    x1_s, xn_s, ri_s, rw_s, cnt = _merge(xs_, po_s, fo_s, co_s, gt_s, *margs, cnt_p, n_exp, tm)

    tmf = 512
    t_all = bp * lp + bs * ls
    counts = cnt[0, :n_exp].astype(jnp.int32)
    tiles_e = (counts + tmf - 1) // tmf
    tile_end = jnp.cumsum(tiles_e)
    row_off = (tile_end - tiles_e) * tmf
    nt_max = (t_all * TOP_K + n_exp * (tmf - 1)) // tmf + 1
    n_used = tile_end[-1:]
    tile_ids = jnp.minimum(jnp.arange(nt_max, dtype=jnp.int32), n_used[0] - 1)
    tile_exp = jnp.minimum(jnp.sum(tile_ids[:, None] >= tile_end[None, :], axis=1), n_exp - 1).astype(jnp.int32)

    def positions(ri):
        e = ri[:, :TOP_K]
        r = ri[:, TOP_K:2 * TOP_K]
        off = jnp.sum(jnp.where(e[:, :, None] == jnp.arange(n_exp)[None, None, :], row_off[None, None, :], 0), axis=2)
        return (off + r).reshape(-1).astype(jnp.int32)

    pos_p, pos_s = positions(ri_p), positions(ri_s)

    wg_t, wu_t = _wprep(w_gate_up[0])
    wd = w_down[0].astype(BF16)
    bgu = b_gate_up[0]
    bg_e, bu_e = bgu[:, 0::2].reshape(n_exp, 1, d_ff), bgu[:, 1::2].reshape(n_exp, 1, d_ff)
    bd_e = b_down[0].reshape(n_exp, 1, d)
    pad = ((row_off + counts).astype(jnp.int32), (tiles_e * tmf - counts).astype(jnp.int32))
    xs_sorted = _dispatch(pos_p, xn_p, nt_max * tmf, pad=pad, tile_rows=tmf)
    xs_sorted = _dispatch(pos_s, xn_s, nt_max * tmf, xs=xs_sorted, tile_rows=tmf)
    y = _ffn(tile_exp, n_used.astype(jnp.int32), xs_sorted, wg_t, wu_t, wd, bg_e, bu_e, bd_e, tmf)
    g_fin = g_final.reshape(1, d)
    y_p = _combine(pos_p, rw_p, x1_p, g_fin, y)
    y_s = _combine(pos_s, rw_s, x1_s, g_fin, y)

    kv5 = lambda a, b, l: a.reshape(1, b, l, FOX_H, HEAD_DIM)
    return (y_p.reshape(bp, lp, d), y_s.reshape(bs, ls, d),
            kv5(k_p, bp, lp), kv5(v_p, bp, lp), lf_p.reshape(1, bp, lp, FOX_H),
            u_p.reshape(bp, lp, pool_w)[:, lp - POOL_HIST:, :][None],
            mk.reshape(1, bp, n_mem, CROSS_H, HEAD_DIM), mv.reshape(1, bp, n_mem, CROSS_H, HEAD_DIM),
            kv5(k_s, bs, ls), kv5(v_s, bs, ls), lf_s.reshape(1, bs, ls, FOX_H),
            u_s.reshape(bs, ls, pool_w)[:, ls - POOL_HIST:, :][None])
```

```python
import functools

import jax
import jax.numpy as jnp
import numpy as np
from jax import lax
from jax.experimental import pallas as pl
from jax.experimental.pallas import tpu as pltpu

F32 = jnp.float32
BF16 = jnp.bfloat16

HEAD_DIM = 64
FOX_H = 8
CROSS_H = 4
POOL_WINDOWS = (2, 4, 8, 16)
POOL_HIST = 15
N_BRANCH = 3
TOP_K = 4
SWIGLU_LIMIT = 7.0
SWIGLU_ALPHA = 1.702
RMS_EPS = 1e-5
NEG = -1e30
LOG2E = 1.4426950408889634
QSLAB = 256
V_ROWS = HEAD_DIM + 16

LANES = 128
DMA_UNROLL = 4
HIST_ROWS = 16
VMEM_LIMIT = 56 * 1024 * 1024


def _cparams(n_axes=1, vmem=VMEM_LIMIT):
    return pltpu.CompilerParams(dimension_semantics=("arbitrary",) * n_axes, vmem_limit_bytes=vmem)


def _const_spec(shape):
    nd = len(shape)
    return pl.BlockSpec(shape, lambda *_: (0,) * nd)


def _split3(x):
    hi = x.astype(BF16)
    r = x - hi.astype(F32)
    mid = r.astype(BF16)
    lo = (r - mid.astype(F32)).astype(BF16)
    return hi, mid, lo


def _rms(x, g):
    ms = jnp.mean(x * x, axis=-1, keepdims=True)
    return x * lax.rsqrt(ms + RMS_EPS) * g


def _log_sigmoid(z):
    return jnp.minimum(z, 0.0) - jnp.log1p(jnp.exp(-jnp.abs(z)))


def _dot(a, b):
    return jnp.dot(a, b, preferred_element_type=F32)


def _dot_nt(a, b):
    return lax.dot_general(a, b, (((1,), (1,)), ((), ())), preferred_element_type=F32)


def _tri(n, strict):
    r = np.arange(n)
    m = (r[None, :] < r[:, None]) if strict else (r[None, :] <= r[:, None])
    return jnp.asarray(m.astype(np.float32), dtype=BF16)


def _bias_placement():
    pq = np.zeros((3 * LANES, LANES), np.float32)
    pk = np.zeros((3 * LANES, LANES), np.float32)
    oq = np.zeros((1, LANES), np.float32)
    ok = np.zeros((1, LANES), np.float32)
    for p in range(3):
        for h in range(FOX_H):
            pq[p * LANES + h, 8 * p + h] = 1.0
            pk[p * LANES + h, 24 + 8 * p + h] = -1.0
            oq[0, 24 + 8 * p + h] = 1.0
            ok[0, 8 * p + h] = 1.0
    return (jnp.asarray(pq, dtype=BF16), jnp.asarray(pk, dtype=BF16), jnp.asarray(oq), jnp.asarray(ok))


def _head_lane_masks():
    m = np.zeros((FOX_H, LANES), np.float32)
    for p in range(3):
        for h in range(FOX_H):
            m[h, 8 * p + h] = 1.0
            m[h, 24 + 8 * p + h] = 1.0
    return m


def _pool_lane_windows(pool_w):
    gw = pool_w // len(POOL_WINDOWS)
    return jnp.asarray(np.repeat(np.asarray(POOL_WINDOWS, np.float32), gw)[None, :])


SEC_U, SEC_Q, SEC_K, SEC_V, SEC_CQ, SEC_G, SEC_F = range(7)


def _pool_mix(ext, u, row0, winl, wpool, spool):
    n = u.shape[0]
    s1 = ext + pltpu.roll(ext, 1, 0)
    s2 = s1 + pltpu.roll(s1, 2, 0)
    s3 = s2 + pltpu.roll(s2, 4, 0)
    s4 = s3 + pltpu.roll(s3, 8, 0)
    win = jnp.where(winl == 2.0, s1, jnp.where(winl == 4.0, s2, jnp.where(winl == 8.0, s3, s4)))
    win = win[HIST_ROWS:, :]
    pos = (row0 + lax.broadcasted_iota(jnp.int32, (n, 1), 0)).astype(F32)
    cnt = jnp.minimum(pos + 1.0, winl)
    pooled = win / cnt - u
    return (_dot(pooled.astype(BF16), wpool) * spool).astype(BF16)


def _inproj_kernel(x_ref, g_ref, w_ref, bf_ref, bg_ref, *rest, offs, prompt, tiles_per_batch, tm, n_gate_chunks):
    if prompt:
        (tri_ref, pq_ref, pk_ref, oq_ref, ok_ref, wpool_ref, spool_ref, winl_ref,
         u_ref, po_ref, q_ref, qx_ref, kb_ref, kx_ref, vb_ref, k_ref, v_ref, lf_ref, qc_ref, gt_ref,
         carry_ref, hist_ref) = rest
    else:
        (u_ref, q_ref, k_ref, v_ref, lf_ref, qc_ref, gt_ref) = rest

    h = _rms(x_ref[...], g_ref[...]).astype(BF16)

    def sec(s):
        return _dot(h, w_ref[:, offs[s]:offs[s + 1]])

    u = sec(SEC_U)
    u_ref[...] = u
    q = sec(SEC_Q)
    k = sec(SEC_K)
    v = sec(SEC_V)
    k_ref[...] = k
    v_ref[...] = v
    qc_ref[...] = sec(SEC_CQ).astype(BF16)
    gw = (offs[SEC_G + 1] - offs[SEC_G]) // n_gate_chunks
    for c in range(n_gate_chunks):
        a = offs[SEC_G] + c * gw
        z = _dot(h, w_ref[:, a:a + gw]) + bg_ref[:, c * gw:(c + 1) * gw]
        gt_ref[:, c * gw:(c + 1) * gw] = jax.nn.sigmoid(z).astype(BF16)
    zf = sec(SEC_F) + bf_ref[...]
    lane = lax.broadcasted_iota(jnp.int32, zf.shape, 1)
    logf = jnp.where(lane < FOX_H, _log_sigmoid(zf), 0.0)
    lf_ref[...] = logf[:, :FOX_H]

    if not prompt:
        q_ref[...] = q.astype(BF16)
        return

    tib = pl.program_id(0) % tiles_per_batch

    @pl.when(tib == 0)
    def _():
        carry_ref[...] = jnp.zeros_like(carry_ref)
        hist_ref[...] = jnp.zeros_like(hist_ref)

    n_pair = q.shape[1] // LANES
    for hp in range(n_pair):
        sl = slice(hp * LANES, (hp + 1) * LANES)
        q_ref[hp] = q[:, sl].T.astype(BF16)
        kb_ref[hp] = k[:, sl].astype(BF16)
        vt = v[:, sl].T.astype(BF16)
        for a in range(2):
            vb_ref[2 * hp + a, 0, 0:HEAD_DIM, :] = vt[a * HEAD_DIM:(a + 1) * HEAD_DIM, :]
            vb_ref[2 * hp + a, 0, HEAD_DIM:V_ROWS, :] = jnp.ones((V_ROWS - HEAD_DIM, tm), BF16)

    tri = tri_ref[...]
    hi, mid, lo = _split3(logf)
    cum = _dot(tri, hi) + _dot(tri, mid) + _dot(tri, lo) + carry_ref[...]
    carry_ref[...] = cum[tm - 1:tm, :]
    cp = jnp.concatenate(_split3(cum * LOG2E), axis=1)
    qx_ref[...] = (_dot(cp, pq_ref[...]) + oq_ref[...]).T.astype(BF16)
    kx_ref[...] = (_dot(cp, pk_ref[...]) + ok_ref[...]).astype(BF16)

    ext = jnp.concatenate([hist_ref[...], u], axis=0)
    hist_ref[...] = u[tm - HIST_ROWS:, :]
    po_ref[...] = _pool_mix(ext, u, tib * tm, winl_ref[...], wpool_ref[...], spool_ref[...])


def _inproj(x, g_mix, w_cat, b_f, b_g, offs, *, prompt, seq_len=None, consts=None, tm=512):
    t, d = x.shape
    tm = min(tm, t)
    n_tiles = t // tm
    pool_w = offs[SEC_U + 1] - offs[SEC_U]
    fox_w = offs[SEC_Q + 1] - offs[SEC_Q]
    cross_w = offs[SEC_CQ + 1] - offs[SEC_CQ]
    gate_w = offs[SEC_G + 1] - offs[SEC_G]
    n_pair = fox_w // LANES
    row = lambda w: pl.BlockSpec((tm, w), lambda i: (i, 0))
    pair = pl.BlockSpec((n_pair, tm, LANES), lambda i: (0, i, 0))
    in_specs = [row(d), _const_spec((1, d)),
                pl.BlockSpec(w_cat.shape, lambda i: (0, 0), pipeline_mode=pl.Buffered(1)),
                _const_spec((1, LANES)), _const_spec((1, gate_w))]
    args = [x, g_mix, w_cat, b_f, b_g]
    sd = jax.ShapeDtypeStruct
    if prompt:
        tri, pq, pk, oq, ok, wpool, spool, winl = consts
        in_specs += [_const_spec(a.shape) for a in consts]
        args += list(consts)
        pair_t = pl.BlockSpec((n_pair, LANES, tm), lambda i: (0, 0, i))
        head_t = pl.BlockSpec((FOX_H, 1, V_ROWS, tm), lambda i: (0, i, 0, 0))
        out_shape = [sd((t, pool_w), F32), sd((t, pool_w), BF16),
                     sd((n_pair, LANES, t), BF16), sd((LANES, t), BF16),
                     sd((n_pair, t, LANES), BF16), sd((t, LANES), BF16),
                     sd((FOX_H, n_tiles, V_ROWS, tm), BF16),
                     sd((t, fox_w), F32), sd((t, fox_w), F32), sd((t, FOX_H), F32),
                     sd((t, cross_w), BF16), sd((t, gate_w), BF16)]
        out_specs = [row(pool_w), row(pool_w), pair_t, pl.BlockSpec((LANES, tm), lambda i: (0, i)),
                     pair, row(LANES), head_t,
                     row(fox_w), row(fox_w), row(FOX_H), row(cross_w), row(gate_w)]
        scratch = [pltpu.VMEM((1, LANES), F32), pltpu.VMEM((HIST_ROWS, pool_w), F32)]
        tiles_per_batch = seq_len // tm
    else:
        out_shape = [sd((t, pool_w), F32), sd((t, fox_w), BF16), sd((t, fox_w), F32), sd((t, fox_w), F32),
                     sd((t, FOX_H), F32), sd((t, cross_w), BF16), sd((t, gate_w), BF16)]
        out_specs = [row(pool_w), row(fox_w), row(fox_w), row(fox_w), row(FOX_H), row(cross_w), row(gate_w)]
        scratch = []
        tiles_per_batch = 1
    kern = functools.partial(_inproj_kernel, offs=offs, prompt=prompt, tiles_per_batch=tiles_per_batch,
                             tm=tm, n_gate_chunks=N_BRANCH)
    return pl.pallas_call(
        kern, grid=(n_tiles,), in_specs=in_specs, out_specs=out_specs, out_shape=out_shape,
        scratch_shapes=scratch, compiler_params=_cparams(1),
        name="inproj_prompt" if prompt else "inproj_sample")(*args)


def _memkv_kernel(m_ref, g_ref, w_ref, k_ref, v_ref, *, cw):
    h = _rms(m_ref[...], g_ref[...]).astype(BF16)
    kv = _dot(h, w_ref[...])
    k_ref[...] = kv[:, :cw]
    v_ref[...] = kv[:, cw:]


def _memkv(mem, g_mem, w_kv, tm=256):
    t, d = mem.shape
    cw = w_kv.shape[1] // 2
    tm = min(tm, t)
    row = lambda w: pl.BlockSpec((tm, w), lambda i: (i, 0))
    return pl.pallas_call(
        functools.partial(_memkv_kernel, cw=cw), grid=(t // tm,),
        in_specs=[row(d), _const_spec((1, d)), _const_spec(w_kv.shape)],
        out_specs=[row(cw), row(cw)],
        out_shape=[jax.ShapeDtypeStruct((t, cw), F32)] * 2,
        compiler_params=_cparams(1), name="memkv")(mem, g_mem, w_kv)


def _cross_kernel(q_ref, k_ref, v_ref, o_ref):
    q = q_ref[...]
    kk = k_ref[0].astype(BF16)
    vv = v_ref[0].astype(BF16)
    lane = lax.broadcasted_iota(jnp.int32, (1, q.shape[1]), 1) // HEAD_DIM
    out = jnp.zeros(q.shape, F32)
    for h in range(q.shape[1] // HEAD_DIM):
        hm = lane == h
        s = _dot_nt(jnp.where(hm, q, jnp.zeros_like(q)), kk)
        m = jnp.max(s, axis=1, keepdims=True)
        p = jnp.exp(s - m)
        l = jnp.sum(p, axis=1, keepdims=True)
        o = _dot(p.astype(BF16), vv) / l
        out = jnp.where(hm, o, out)
    o_ref[...] = out.astype(BF16)


def _cross(qc, mk, mv, n_batch, tl=512):
    t, cw = qc.shape
    l = t // n_batch
    tl = min(tl, l)
    nl = l // tl
    m = mk.shape[1]
    return pl.pallas_call(
        _cross_kernel, grid=(n_batch, nl),
        in_specs=[pl.BlockSpec((tl, cw), lambda b, i: (b * nl + i, 0)),
                  pl.BlockSpec((1, m, cw), lambda b, i: (b, 0, 0)),
                  pl.BlockSpec((1, m, cw), lambda b, i: (b, 0, 0))],
        out_specs=pl.BlockSpec((tl, cw), lambda b, i: (b * nl + i, 0)),
        out_shape=jax.ShapeDtypeStruct((t, cw), BF16),
        compiler_params=_cparams(2), name="cross_attn")(qc, mk, mv)


def _fox_kernel(hm_ref, q_ref, qx_ref, k_ref, kx_ref, v_ref, o_ref, m0_ref, m1_ref, acc0_ref, acc1_ref, s0_ref, *, tq):
    qi = pl.program_id(2)
    qt = q_ref[0]
    qxt = qx_ref[...]
    row = lax.broadcasted_iota(jnp.int32, (LANES, 1), 0)
    zero = jnp.zeros_like(qt)
    qs = []
    for a in range(2):
        half = (row < HEAD_DIM) if a == 0 else (row >= HEAD_DIM)
        qs.append(jnp.concatenate([jnp.where(half, qt, zero), jnp.where(hm_ref[a] > 0.5, qxt, zero)], axis=0))
    ms = (m0_ref, m1_ref)
    accs = (acc0_ref, acc1_ref)
    for a in range(2):
        ms[a][...] = jnp.full_like(ms[a], NEG)
        accs[a][...] = jnp.zeros_like(accs[a])

    def scores(a, ki):
        ks = pl.multiple_of(ki * tq, tq)
        kcat = jnp.concatenate([k_ref[0, pl.ds(ks, tq), :], kx_ref[pl.ds(ks, tq), :]], axis=1)
        return _dot(kcat, qs[a])

    def softmax_pv(a, s, ki, masked):
        if masked:
            r = lax.broadcasted_iota(jnp.int32, s.shape, 0)
            c = lax.broadcasted_iota(jnp.int32, s.shape, 1)
            s = jnp.where(r <= c, s, NEG)
        m_old = ms[a][...]
        m_new = jnp.maximum(m_old, jnp.max(s, axis=0, keepdims=True))
        alpha = jnp.exp2(m_old - m_new)
        p = jnp.exp2(s - m_new).astype(BF16)
        accs[a][...] = alpha * accs[a][...] + _dot(v_ref[a, ki], p)
        ms[a][...] = m_new

    s0_ref[...] = scores(0, 0)

    def body(ki, carry):
        s1 = scores(1, ki)
        softmax_pv(0, s0_ref[...], ki, False)
        s0_ref[...] = scores(0, ki + 1)
        softmax_pv(1, s1, ki, False)
        return carry

    lax.fori_loop(0, qi, body, 0)
    s1 = scores(1, qi)
    softmax_pv(0, s0_ref[...], qi, True)
    softmax_pv(1, s1, qi, True)
    outs = [accs[a][0:HEAD_DIM, :] / accs[a][HEAD_DIM:HEAD_DIM + 1, :] for a in range(2)]
    o_ref[...] = jnp.concatenate(outs, axis=0).T.astype(BF16)


def _fox_prompt(hmask, q, qx, kb, kx, vb, n_batch):
    n_pair, _, t = q.shape
    _, n_tiles, _, tq = vb.shape
    l = t // n_batch
    nq = l // tq
    once = pl.Buffered(1)
    return pl.pallas_call(
        functools.partial(_fox_kernel, tq=tq), grid=(n_batch, n_pair, nq),
        in_specs=[pl.BlockSpec((2, LANES, 1), lambda b, h, i: (h, 0, 0)),
                  pl.BlockSpec((1, LANES, tq), lambda b, h, i: (h, 0, b * nq + i)),
                  pl.BlockSpec((LANES, tq), lambda b, h, i: (0, b * nq + i)),
                  pl.BlockSpec((1, l, LANES), lambda b, h, i: (h, b, 0), pipeline_mode=once),
                  pl.BlockSpec((l, LANES), lambda b, h, i: (b, 0), pipeline_mode=once),
                  pl.BlockSpec((2, nq, V_ROWS, tq), lambda b, h, i: (h, b, 0, 0), pipeline_mode=once)],
        out_specs=pl.BlockSpec((tq, LANES), lambda b, h, i: (b * nq + i, h)),
        out_shape=jax.ShapeDtypeStruct((t, n_pair * LANES), BF16),
        scratch_shapes=[pltpu.VMEM((1, tq), F32), pltpu.VMEM((1, tq), F32),
                        pltpu.VMEM((V_ROWS, tq), F32), pltpu.VMEM((V_ROWS, tq), F32), pltpu.VMEM((tq, tq), F32)],
        compiler_params=_cparams(3), name="fox_prompt")(hmask, q, qx, kb, kx, vb)


def _sample_kernel(q_ref, kn_ref, vn_ref, lfn_ref, u_ref, ck_ref, cv_ref, clf_ref, st_ref,
                   tri_ref, pq_ref, pk_ref, oq_ref, ok_ref, qmask_ref, xmask_ref, wpool_ref, spool_ref, winl_ref,
                   fo_ref, po_ref,
                   ext_ref, lf_ref, cum_ref, kcat_ref, vall_ref, *, past, ls, chunk):
    fw = q_ref.shape[1]
    lk_pad = kcat_ref.shape[0]
    n_heads = fw // HEAD_DIM

    u = u_ref[...]
    ext_ref[...] = jnp.zeros_like(ext_ref)
    ext_ref[pl.ds(HIST_ROWS - POOL_HIST, POOL_HIST), :] = st_ref[0]
    ext_ref[pl.ds(HIST_ROWS, ls), :] = u
    po_ref[...] = _pool_mix(ext_ref[...], u, past, winl_ref[...], wpool_ref[...], spool_ref[...])

    lf_ref[...] = jnp.zeros_like(lf_ref)
    lf_ref[pl.ds(0, past), pl.ds(0, FOX_H)] = clf_ref[0]
    lf_ref[pl.ds(past, ls), pl.ds(0, FOX_H)] = lfn_ref[...]
    tri = tri_ref[...]
    carry = jnp.zeros((1, LANES), F32)
    for c in range(lk_pad // chunk):
        rows = pl.ds(c * chunk, chunk)
        hi, mid, lo = _split3(lf_ref[rows, :])
        cum = _dot(tri, hi) + _dot(tri, mid) + _dot(tri, lo) + carry
        carry = cum[chunk - 1:chunk, :]
        cum_ref[rows, :] = cum
        cp = jnp.concatenate(_split3(cum * LOG2E), axis=1)
        kcat_ref[rows, pl.ds(fw, LANES)] = (_dot(cp, pk_ref[...]) + ok_ref[...]).astype(BF16)

    kcat_ref[pl.ds(0, past), pl.ds(0, fw)] = ck_ref[0].astype(BF16)
    kcat_ref[pl.ds(past, ls), pl.ds(0, fw)] = kn_ref[...].astype(BF16)
    vall_ref[pl.ds(0, past), :] = cv_ref[0].astype(BF16)
    vall_ref[pl.ds(past, ls), :] = vn_ref[...].astype(BF16)
    npad = lk_pad - past - ls
    kcat_ref[pl.ds(past + ls, npad), pl.ds(0, fw)] = jnp.zeros((npad, fw), BF16)
    vall_ref[pl.ds(past + ls, npad), :] = jnp.zeros((npad, fw), BF16)

    cq = cum_ref[pl.ds(past, ls), :]
    cpq = jnp.concatenate(_split3(cq * LOG2E), axis=1)
    qx = (_dot(cpq, pq_ref[...]) + oq_ref[...]).astype(BF16)
    q = q_ref[...]
    qbd = jnp.concatenate([jnp.concatenate([q] * n_heads, axis=0) * qmask_ref[...],
                           jnp.concatenate([qx] * n_heads, axis=0) * xmask_ref[...]], axis=1)
    s = _dot_nt(qbd, kcat_ref[...])
    r = lax.broadcasted_iota(jnp.int32, s.shape, 0) % ls
    c = lax.broadcasted_iota(jnp.int32, s.shape, 1)
    s = jnp.where(c <= r + past, s, NEG)
    m = jnp.max(s, axis=1, keepdims=True)
    p = jnp.exp2(s - m)
    l = jnp.sum(p, axis=1, keepdims=True)
    o = _dot(p.astype(BF16), vall_ref[...]) / l
    lane_h = lax.broadcasted_iota(jnp.int32, (1, fw), 1) // HEAD_DIM
    out = jnp.zeros((ls, fw), F32)
    for h in range(n_heads):
        out = jnp.where(lane_h == h, o[h * ls:(h + 1) * ls, :], out)
    fo_ref[...] = out.astype(BF16)


def _sample_mix(q, kn, vn, lfn, u, ck, cv, clf, st, consts, chunk=256):
    t, fw = q.shape
    nb, past, _ = ck.shape
    ls = t // nb
    pw = u.shape[1]
    lk_pad = -(-(past + ls) // chunk) * chunk
    if lk_pad == past + ls:
        lk_pad += chunk
    n_heads = fw // HEAD_DIM
    hm = _head_lane_masks()
    qmask = np.zeros((n_heads * ls, fw), np.float32)
    xmask = np.zeros((n_heads * ls, LANES), np.float32)
    for h in range(n_heads):
        qmask[h * ls:(h + 1) * ls, h * HEAD_DIM:(h + 1) * HEAD_DIM] = 1.0
        xmask[h * ls:(h + 1) * ls, :] = hm[h]
    tri, pq, pk, oq, ok, wpool, spool, winl = consts
    tri_c = tri[:chunk, :chunk]
    cargs = [tri_c, pq, pk, oq, ok, jnp.asarray(qmask, dtype=BF16), jnp.asarray(xmask, dtype=BF16), wpool, spool, winl]
    row = lambda w: pl.BlockSpec((ls, w), lambda b: (b, 0))
    bat = lambda a: pl.BlockSpec((1,) + a.shape[1:], lambda b: (b, 0, 0))
    return pl.pallas_call(
        functools.partial(_sample_kernel, past=past, ls=ls, chunk=chunk), grid=(nb,),
        in_specs=[row(fw), row(fw), row(fw), row(FOX_H), row(pw), bat(ck), bat(cv), bat(clf), bat(st)]
                 + [_const_spec(a.shape) for a in cargs],
        out_specs=[row(fw), row(pw)],
        out_shape=[jax.ShapeDtypeStruct((t, fw), BF16), jax.ShapeDtypeStruct((t, pw), BF16)],
        scratch_shapes=[pltpu.VMEM((HIST_ROWS + ls, pw), F32), pltpu.VMEM((lk_pad, LANES), F32),
                        pltpu.VMEM((lk_pad, LANES), F32), pltpu.VMEM((lk_pad, fw + LANES), BF16),
                        pltpu.VMEM((lk_pad, fw), BF16)],
        compiler_params=_cparams(1), name="sample_mix")(q, kn, vn, lfn, u, ck, cv, clf, st, *cargs)


def _merge_kernel(x_ref, po_ref, fo_ref, co_ref, gt_ref, wbp_ref, wbf_ref, wbc_ref, wo_ref, gf_ref,
                  wrh_ref, wrl_ref, br_ref, tri_ref, cin_ref,
                  x1_ref, xn_ref, ri_ref, rw_ref, cnt_ref, carry_ref, *, d, n_exp, tm):
    @pl.when(pl.program_id(0) == 0)
    def _():
        carry_ref[...] = cin_ref[...]

    def gate(j):
        return gt_ref[:, j * d:(j + 1) * d].astype(F32)

    merged = (gate(0) * _dot(po_ref[...], wbp_ref[...])
              + gate(1) * _dot(fo_ref[...], wbf_ref[...])
              + gate(2) * _dot(co_ref[...], wbc_ref[...]))
    x1 = x_ref[...] + _dot(merged.astype(BF16), wo_ref[...])
    x1_ref[...] = x1
    xn = _rms(x1, gf_ref[...])
    xn_ref[...] = xn
    xh = xn.astype(BF16)
    xl = (xn - xh.astype(F32)).astype(BF16)
    logits = _dot(xh, wrh_ref[...]) + _dot(xl, wrh_ref[...]) + _dot(xh, wrl_ref[...]) + br_ref[...]

    lane = lax.broadcasted_iota(jnp.int32, logits.shape, 1)
    lane_f = lane.astype(F32)
    work = jnp.where(lane < n_exp, logits, NEG)
    vals, idxs, sels = [], [], []
    for _ in range(TOP_K):
        mx = jnp.max(work, axis=1, keepdims=True)
        idx = jnp.min(jnp.where(work == mx, lane_f, float(LANES)), axis=1, keepdims=True)
        sel = lane_f == idx
        vals.append(mx)
        idxs.append(idx)
        sels.append(sel)
        work = jnp.where(sel, NEG, work)
    es = [jnp.exp(v - vals[0]) for v in vals]
    den = es[0] + es[1] + es[2] + es[3]

    onehot = jnp.zeros(logits.shape, F32)
    for sel in sels:
        onehot = jnp.where(sel, 1.0, onehot)
    cum = _dot(tri_ref[...], onehot.astype(BF16)) + carry_ref[...]
    carry_ref[...] = cum[tm - 1:tm, :] + onehot[tm - 1:tm, :]
    cnt_ref[...] = carry_ref[...]

    ri = jnp.zeros(logits.shape, jnp.int32)
    rw = jnp.zeros(logits.shape, F32)
    for j in range(TOP_K):
        rank = jnp.sum(jnp.where(sels[j], cum, 0.0), axis=1, keepdims=True)
        ri = jnp.where(lane == j, idxs[j].astype(jnp.int32), ri)
        ri = jnp.where(lane == TOP_K + j, rank.astype(jnp.int32), ri)
        rw = jnp.where(lane == j, es[j] / den, rw)
    ri_ref[...] = ri
    rw_ref[...] = rw


def _merge(x, po, fo, co, gt, wbp, wbf, wbc, wo, g_ffn, wrh, wrl, br, tri_s, cnt_in, n_exp, tm=512):
    t, d = x.shape
    tm = min(tm, t)
    row = lambda a: pl.BlockSpec((tm, a.shape[1]), lambda i: (i, 0))
    consts = [wbp, wbf, wbc, wo, g_ffn, wrh, wrl, br, tri_s, cnt_in]
    sd = jax.ShapeDtypeStruct
    rspec = lambda w: pl.BlockSpec((tm, w), lambda i: (i, 0))
    return pl.pallas_call(
        functools.partial(_merge_kernel, d=d, n_exp=n_exp, tm=tm), grid=(t // tm,),
        in_specs=[row(x), row(po), row(fo), row(co), row(gt)] + [_const_spec(a.shape) for a in consts],
        out_specs=[rspec(d), rspec(d), rspec(LANES), rspec(LANES), _const_spec((1, LANES))],
        out_shape=[sd((t, d), F32), sd((t, d), F32), sd((t, LANES), jnp.int32), sd((t, LANES), F32),
                   sd((1, LANES), F32)],
        scratch_shapes=[pltpu.VMEM((1, LANES), F32)],
        compiler_params=_cparams(1), name="merge_router")(x, po, fo, co, gt, *consts)


def _wprep_kernel(w_ref, g_ref, u_ref, t_ref):
    fc = g_ref.shape[1]
    for c in range(t_ref.shape[0]):
        cols = slice(c * LANES, (c + 1) * LANES)
        t_ref[c] = w_ref[0, cols, :].T
        g_ref[0, :, cols] = t_ref[c, pl.ds(0, fc, stride=2), :].astype(BF16)
        u_ref[0, :, cols] = t_ref[c, pl.ds(1, fc, stride=2), :].astype(BF16)


def _wprep(w_gate_up, fc=256):
    e, d, f2 = w_gate_up.shape
    f = f2 // 2
    fc = min(fc, f)
    out = pl.BlockSpec((1, fc, d), lambda i, c: (i, c, 0))
    return pl.pallas_call(
        _wprep_kernel, grid=(e, f // fc),
        in_specs=[pl.BlockSpec((1, d, 2 * fc), lambda i, c: (i, 0, c))],
        out_specs=[out, out],
        out_shape=[jax.ShapeDtypeStruct((e, f, d), BF16)] * 2,
        scratch_shapes=[pltpu.VMEM((d // LANES, 2 * fc, LANES), F32)],
        compiler_params=_cparams(2), name="expert_weight_layout")(w_gate_up)


def _zero_pad_rows(ps_ref, pl_ref, xs_ref, z_ref, sem, n_exp, n_bits):
    z_ref[...] = jnp.zeros_like(z_ref)
    sub = 8

    def each(fn):
        for e in range(n_exp):
            start = ps_ref[e]
            n = pl_ref[e]
            head = jnp.minimum((-start) & (sub - 1), n)
            for r in range(sub - 1):
                @pl.when(r < head)
                def _():
                    fn(pltpu.make_async_copy(z_ref.at[pl.ds(0, 1), :], xs_ref.at[pl.ds(start + r, 1), :], sem))
            start8 = start + head
            n8 = (n - head) // sub
            for b in range(n_bits - 3):
                rows = sub << b
                off = pl.multiple_of(start8 + sub * (n8 & ((1 << b) - 1)), sub)

                @pl.when(((n8 >> b) & 1) == 1)
                def _():
                    fn(pltpu.make_async_copy(z_ref.at[pl.ds(0, rows), :], xs_ref.at[pl.ds(off, rows), :], sem))

    each(lambda c: c.start())
    each(lambda c: c.wait())


def _dispatch_kernel(pos_ref, x_ref, *rest, tm, first, n_exp, n_bits):
    if first:
        ps_ref, pl_ref, xs_ref, sem, z_ref = rest

        @pl.when(pl.program_id(0) == 0)
        def _():
            _zero_pad_rows(ps_ref, pl_ref, xs_ref, z_ref, sem, n_exp, n_bits)
    else:
        _, xs_ref, sem = rest

    def issue(t, c):
        for j in range(TOP_K):
            pltpu.make_async_copy(x_ref.at[pl.ds(t, 1), :],
                                  xs_ref.at[pl.ds(pos_ref[t * TOP_K + j], 1), :], sem).start()
        return c

    lax.fori_loop(0, tm, issue, 0, unroll=DMA_UNROLL)
    for j in range(TOP_K):
        pltpu.make_async_copy(x_ref, xs_ref.at[pl.ds(0, tm), :], sem).wait()


def _dispatch(pos_flat, xn, n_rows, pad=None, xs=None, tm=512, tile_rows=512):
    t, d = xn.shape
    tm = min(tm, t)
    first = xs is None
    n_bits = (tile_rows - 1).bit_length()
    smem = lambda n: pl.BlockSpec((n,), lambda i: (0,), memory_space=pltpu.SMEM)
    in_specs = [pl.BlockSpec((tm * TOP_K,), lambda i: (i,), memory_space=pltpu.SMEM),
                pl.BlockSpec((tm, d), lambda i: (i, 0))]
    scratch = [pltpu.SemaphoreType.DMA]
    if first:
        n_exp = pad[0].shape[0]
        in_specs += [smem(n_exp), smem(n_exp)]
        args = (pos_flat, xn, pad[0], pad[1])
        scratch.append(pltpu.VMEM((1 << (n_bits - 1), d), F32))
        aliases = {}
    else:
        n_exp = 0
        in_specs.append(pl.BlockSpec(memory_space=pl.ANY))
        args = (pos_flat, xn, xs)
        aliases = {2: 0}
    return pl.pallas_call(
        functools.partial(_dispatch_kernel, tm=tm, first=first, n_exp=n_exp, n_bits=n_bits), grid=(t // tm,),
        in_specs=in_specs,
        out_specs=pl.BlockSpec(memory_space=pl.ANY),
        out_shape=jax.ShapeDtypeStruct((n_rows, d), F32),
        scratch_shapes=scratch,
        input_output_aliases=aliases,
        compiler_params=_cparams(1), name="dispatch")(*args)


def _ffn_kernel(te_ref, nu_ref, x_ref, wg_ref, wu_ref, wd_ref, bg_ref, bu_ref, bd_ref, y_ref):
    @pl.when(pl.program_id(0) < nu_ref[0])
    def _():
        x = x_ref[...].astype(BF16)
        g = _dot_nt(x, wg_ref[0]) + bg_ref[0]
        u = _dot_nt(x, wu_ref[0]) + bu_ref[0]
        gate = jnp.minimum(g, SWIGLU_LIMIT)
        up = jnp.clip(u, -SWIGLU_LIMIT, SWIGLU_LIMIT)
        act = (up + 1.0) * gate * jax.nn.sigmoid(SWIGLU_ALPHA * gate)
        y_ref[...] = _dot(act.astype(BF16), wd_ref[0]) + bd_ref[0]


def _ffn(tile_exp, n_used, xs, wg_t, wu_t, wd, bg, bu, bd, tm):
    p, d = xs.shape
    e, f, _ = wg_t.shape
    nt = p // tm
    rowi = lambda i, te, nu: (jnp.minimum(i, nu[0] - 1), 0)
    wi = lambda i, te, nu: (te[i], 0, 0)
    grid_spec = pltpu.PrefetchScalarGridSpec(
        num_scalar_prefetch=2, grid=(nt,),
        in_specs=[pl.BlockSpec((tm, d), rowi),
                  pl.BlockSpec((1, f, d), wi), pl.BlockSpec((1, f, d), wi), pl.BlockSpec((1, f, d), wi),
                  pl.BlockSpec((1, 1, f), wi), pl.BlockSpec((1, 1, f), wi), pl.BlockSpec((1, 1, d), wi)],
        out_specs=pl.BlockSpec((tm, d), rowi))
    return pl.pallas_call(
        _ffn_kernel, grid_spec=grid_spec, out_shape=jax.ShapeDtypeStruct((p, d), F32),
        compiler_params=_cparams(1), name="expert_ffn")(tile_exp, n_used, xs, wg_t, wu_t, wd, bg, bu, bd)


def _combine_kernel(pos_ref, posn_ref, w_ref, x1_ref, gf_ref, y_ref, o_ref, buf_ref, sems, *, tm):
    i = pl.program_id(0)
    n = pl.num_programs(0)
    slot = i % 2

    def gather(idx_ref, s):
        def issue(t, c):
            for j in range(TOP_K):
                pltpu.make_async_copy(y_ref.at[pl.ds(idx_ref[t * TOP_K + j], 1), :],
                                      buf_ref.at[s, j, pl.ds(t, 1), :], sems.at[s]).start()
            return c

        lax.fori_loop(0, tm, issue, 0, unroll=DMA_UNROLL)

    @pl.when(i == 0)
    def _():
        gather(pos_ref, slot)

    @pl.when(i + 1 < n)
    def _():
        gather(posn_ref, 1 - slot)

    for j in range(TOP_K):
        pltpu.make_async_copy(y_ref.at[pl.ds(0, tm), :], buf_ref.at[slot, j], sems.at[slot]).wait()
    w = w_ref[...]
    acc = x1_ref[...]
    for j in range(TOP_K):
        acc = acc + w[:, j:j + 1] * buf_ref[slot, j]
    o_ref[...] = _rms(acc, gf_ref[...])


def _combine(pos_flat, rw, x1, g_final, y, tm=256):
    t, d = x1.shape
    tm = min(tm, t)
    n = t // tm
    return pl.pallas_call(
        functools.partial(_combine_kernel, tm=tm), grid=(n,),
        in_specs=[pl.BlockSpec((tm * TOP_K,), lambda i: (i,), memory_space=pltpu.SMEM),
                  pl.BlockSpec((tm * TOP_K,), lambda i: (jnp.minimum(i + 1, n - 1),), memory_space=pltpu.SMEM),
                  pl.BlockSpec((tm, LANES), lambda i: (i, 0)),
                  pl.BlockSpec((tm, d), lambda i: (i, 0)),
                  _const_spec((1, d)),
                  pl.BlockSpec(memory_space=pl.ANY)],
        out_specs=pl.BlockSpec((tm, d), lambda i: (i, 0)),
        out_shape=jax.ShapeDtypeStruct((t, d), F32),
        scratch_shapes=[pltpu.VMEM((2, TOP_K, tm, d), F32), pltpu.SemaphoreType.DMA((2,))],
        compiler_params=_cparams(1), name="combine")(pos_flat, pos_flat, rw, x1, g_final, y)


def kernel(x_prompt, x_sample, mem_prompt, cache_fox_k, cache_fox_v, cache_fox_logf, state_pool, cache_mem_k, cache_mem_v, g_mix, w_in, b_f, w_pool, s_pool, w_br_pool, w_br_fox, w_br_cross, b_gates, w_out, g_mem, w_mem_kv, g_ffn, w_router, b_router, w_gate_up, b_gate_up, w_down, b_down, g_final):
    depth = w_in.shape[0]
    assert depth == 1, "single-layer model"
    bp, lp, d = x_prompt.shape
    bs, ls, _ = x_sample.shape
    past = cache_fox_k.shape[2]
    n_mem = mem_prompt.shape[1]
    n_exp = w_router.shape[2]
    d_ff = w_down.shape[2]
    pool_w = state_pool.shape[3]
    fox_w = FOX_H * HEAD_DIM
    cross_w = CROSS_H * HEAD_DIM
    scale = HEAD_DIM ** -0.5

    w = w_in[0]
    o_q = pool_w
    o_k, o_v, o_f = o_q + fox_w, o_q + 2 * fox_w, o_q + 3 * fox_w
    o_cq = o_f + FOX_H
    o_g = o_cq + cross_w
    w_cat = jnp.concatenate(
        [w[:, :o_q], w[:, o_q:o_k] * (scale * LOG2E), w[:, o_k:o_v], w[:, o_v:o_f], w[:, o_cq:o_g] * scale, w[:, o_g:],
         jnp.pad(w[:, o_f:o_cq], ((0, 0), (0, LANES - FOX_H)))], axis=1).astype(BF16)
    widths = [pool_w, fox_w, fox_w, fox_w, cross_w, N_BRANCH * d, LANES]
    offs = tuple(int(v) for v in np.concatenate([[0], np.cumsum(widths)]))
    b_f_p = jnp.pad(b_f[0], (0, LANES - FOX_H)).reshape(1, LANES)
    b_g = b_gates[0].reshape(1, -1)
    g_mix2 = g_mix[0].reshape(1, d)
    gw = pool_w // len(POOL_WINDOWS)
    wpool_bd = jnp.zeros((pool_w, pool_w), F32)
    for g in range(len(POOL_WINDOWS)):
        wpool_bd = wpool_bd.at[g * gw:(g + 1) * gw, g * gw:(g + 1) * gw].set(w_pool[0, g])
    tm = 512
    pq, pk, oq, ok = _bias_placement()
    consts = (_tri(tm, strict=False), pq, pk, oq, ok, wpool_bd.astype(BF16), s_pool[0].reshape(1, pool_w),
              _pool_lane_windows(pool_w))
    hmask = jnp.asarray(_head_lane_masks()[:, :, None])

    xp = x_prompt.reshape(bp * lp, d)
    (u_p, po_p, q_p, qx_p, kb_p, kx_p, vb_p, k_p, v_p, lf_p, qc_p, gt_p) = _inproj(
        xp, g_mix2, w_cat, b_f_p, b_g, offs, prompt=True, seq_len=lp, consts=consts, tm=tm)
    mk, mv = _memkv(mem_prompt.reshape(bp * n_mem, d), g_mem[0].reshape(1, d), w_mem_kv[0].astype(BF16))
    mk3, mv3 = mk.reshape(bp, n_mem, cross_w), mv.reshape(bp, n_mem, cross_w)
    fo_p = _fox_prompt(hmask, q_p, qx_p, kb_p, kx_p, vb_p, bp)
    co_p = _cross(qc_p, mk3, mv3, bp)

    xs_ = x_sample.reshape(bs * ls, d)
    (u_s, q_s, k_s, v_s, lf_s, qc_s, gt_s) = _inproj(xs_, g_mix2, w_cat, b_f_p, b_g, offs, prompt=False, tm=tm)
    fo_s, po_s = _sample_mix(q_s, k_s, v_s, lf_s, u_s,
                             cache_fox_k[0].reshape(bs, past, fox_w), cache_fox_v[0].reshape(bs, past, fox_w),
                             cache_fox_logf[0], state_pool[0], consts)
    co_s = _cross(qc_s, cache_mem_k[0].reshape(bs, n_mem, cross_w), cache_mem_v[0].reshape(bs, n_mem, cross_w), bs)

    wbp, wbf, wbc = w_br_pool[0].astype(BF16), w_br_fox[0].astype(BF16), w_br_cross[0].astype(BF16)
    wo = w_out[0].astype(BF16)
    g_ffn2 = g_ffn[0].reshape(1, d)
    wr = jnp.pad(w_router[0], ((0, 0), (0, LANES - n_exp)))
    wrh = wr.astype(BF16)
    wrl = (wr - wrh.astype(F32)).astype(BF16)
    br = jnp.pad(b_router[0], (0, LANES - n_exp)).reshape(1, LANES)
    tri_s = _tri(tm, strict=True)
    margs = (wbp, wbf, wbc, wo, g_ffn2, wrh, wrl, br, tri_s)
    x1_p, xn_p, ri_p, rw_p, cnt_p = _merge(xp, po_p, fo_p, co_p, gt_p, *margs, jnp.zeros((1, LANES), F32), n_exp, tm)
    x1_s, xn_s, ri_s, rw_s, cnt = _merge(xs_, po_s, fo_s, co_s, gt_s, *margs, cnt_p, n_exp, tm)

    tmf = 512
    t_all = bp * lp + bs * ls
    counts = cnt[0, :n_exp].astype(jnp.int32)
    tiles_e = (counts + tmf - 1) // tmf
    tile_end = jnp.cumsum(tiles_e)
    row_off = (tile_end - tiles_e) * tmf
    nt_max = (t_all * TOP_K + n_exp * (tmf - 1)) // tmf + 1
    n_used = tile_end[-1:]
    tile_ids = jnp.minimum(jnp.arange(nt_max, dtype=jnp.int32), n_used[0] - 1)
    tile_exp = jnp.minimum(jnp.sum(tile_ids[:, None] >= tile_end[None, :], axis=1), n_exp - 1).astype(jnp.int32)

    def positions(ri):
        e = ri[:, :TOP_K]
        r = ri[:, TOP_K:2 * TOP_K]
        off = jnp.sum(jnp.where(e[:, :, None] == jnp.arange(n_exp)[None, None, :], row_off[None, None, :], 0), axis=2)
        return (off + r).reshape(-1).astype(jnp.int32)

    pos_p, pos_s = positions(ri_p), positions(ri_s)

    wg_t, wu_t = _wprep(w_gate_up[0])
    wd = w_down[0].astype(BF16)
    bgu = b_gate_up[0]
    bg_e, bu_e = bgu[:, 0::2].reshape(n_exp, 1, d_ff), bgu[:, 1::2].reshape(n_exp, 1, d_ff)
    bd_e = b_down[0].reshape(n_exp, 1, d)
    pad = ((row_off + counts).astype(jnp.int32), (tiles_e * tmf - counts).astype(jnp.int32))
    xs_sorted = _dispatch(pos_p, xn_p, nt_max * tmf, pad=pad, tile_rows=tmf)
    xs_sorted = _dispatch(pos_s, xn_s, nt_max * tmf, xs=xs_sorted, tile_rows=tmf)
    y = _ffn(tile_exp, n_used.astype(jnp.int32), xs_sorted, wg_t, wu_t, wd, bg_e, bu_e, bd_e, tmf)
    g_fin = g_final.reshape(1, d)
    y_p = _combine(pos_p, rw_p, x1_p, g_fin, y)
    y_s = _combine(pos_s, rw_s, x1_s, g_fin, y)

    kv5 = lambda a, b, l: a.reshape(1, b, l, FOX_H, HEAD_DIM)
    return (y_p.reshape(bp, lp, d), y_s.reshape(bs, ls, d),
            kv5(k_p, bp, lp), kv5(v_p, bp, lp), lf_p.reshape(1, bp, lp, FOX_H),
            u_p.reshape(bp, lp, pool_w)[:, lp - POOL_HIST:, :][None],
            mk.reshape(1, bp, n_mem, CROSS_H, HEAD_DIM), mv.reshape(1, bp, n_mem, CROSS_H, HEAD_DIM),
            kv5(k_s, bs, ls), kv5(v_s, bs, ls), lf_s.reshape(1, bs, ls, FOX_H),
            u_s.reshape(bs, ls, pool_w)[:, ls - POOL_HIST:, :][None])
```

```python
import functools

import jax
import jax.numpy as jnp
import numpy as np
from jax import lax
from jax.experimental import pallas as pl
from jax.experimental.pallas import tpu as pltpu

F32 = jnp.float32
BF16 = jnp.bfloat16

HEAD_DIM = 64
FOX_H = 8
CROSS_H = 4
POOL_WINDOWS = (2, 4, 8, 16)
POOL_HIST = 15
N_BRANCH = 3
TOP_K = 4
SWIGLU_LIMIT = 7.0
SWIGLU_ALPHA = 1.702
RMS_EPS = 1e-5
NEG = -1e30
LOG2E = 1.4426950408889634
ZERO_EXP2 = 150.0
V_ROWS = HEAD_DIM + 16

LANES = 128
DMA_UNROLL = 4
HIST_ROWS = 16
VMEM_LIMIT = 56 * 1024 * 1024


def _cparams(n_axes=1, vmem=VMEM_LIMIT):
    return pltpu.CompilerParams(dimension_semantics=("arbitrary",) * n_axes, vmem_limit_bytes=vmem)


def _const_spec(shape):
    nd = len(shape)
    return pl.BlockSpec(shape, lambda *_: (0,) * nd)


def _split3(x):
    hi = x.astype(BF16)
    r = x - hi.astype(F32)
    mid = r.astype(BF16)
    lo = (r - mid.astype(F32)).astype(BF16)
    return hi, mid, lo


def _rms(x, g):
    ms = jnp.mean(x * x, axis=-1, keepdims=True)
    return x * lax.rsqrt(ms + RMS_EPS) * g


def _log_sigmoid(z):
    return jnp.minimum(z, 0.0) - jnp.log1p(jnp.exp(-jnp.abs(z)))


def _dot(a, b):
    return jnp.dot(a, b, preferred_element_type=F32)


def _dot_nt(a, b):
    return lax.dot_general(a, b, (((1,), (1,)), ((), ())), preferred_element_type=F32)


def _tri(n, strict):
    r = np.arange(n)
    m = (r[None, :] < r[:, None]) if strict else (r[None, :] <= r[:, None])
    return jnp.asarray(m.astype(np.float32), dtype=BF16)


def _bias_placement():
    pq = np.zeros((3 * LANES, LANES), np.float32)
    pk = np.zeros((3 * LANES, LANES), np.float32)
    oq = np.zeros((1, LANES), np.float32)
    ok = np.zeros((1, LANES), np.float32)
    for p in range(3):
        for h in range(FOX_H):
            pq[p * LANES + h, 8 * p + h] = 1.0
            pk[p * LANES + h, 24 + 8 * p + h] = -1.0
            oq[0, 24 + 8 * p + h] = 1.0
            ok[0, 8 * p + h] = 1.0
    return (jnp.asarray(pq, dtype=BF16), jnp.asarray(pk, dtype=BF16), jnp.asarray(oq), jnp.asarray(ok))


def _head_lane_masks():
    m = np.zeros((FOX_H, LANES), np.float32)
    for p in range(3):
        for h in range(FOX_H):
            m[h, 8 * p + h] = 1.0
            m[h, 24 + 8 * p + h] = 1.0
    return m


def _head_selector(fox_w):
    m = np.zeros((fox_w, LANES), np.float32)
    for h in range(fox_w // HEAD_DIM):
        m[h * HEAD_DIM:(h + 1) * HEAD_DIM, h] = 1.0
    return jnp.asarray(m, dtype=BF16)


def _pool_lane_windows(pool_w):
    gw = pool_w // len(POOL_WINDOWS)
    return jnp.asarray(np.repeat(np.asarray(POOL_WINDOWS, np.float32), gw)[None, :])


SEC_U, SEC_Q, SEC_K, SEC_V, SEC_CQ, SEC_G, SEC_F = range(7)


def _pool_mix(ext, u, row0, winl, wpool, spool):
    n = u.shape[0]
    s1 = ext + pltpu.roll(ext, 1, 0)
    s2 = s1 + pltpu.roll(s1, 2, 0)
    s3 = s2 + pltpu.roll(s2, 4, 0)
    s4 = s3 + pltpu.roll(s3, 8, 0)
    win = jnp.where(winl == 2.0, s1, jnp.where(winl == 4.0, s2, jnp.where(winl == 8.0, s3, s4)))
    win = win[HIST_ROWS:, :]
    pos = (row0 + lax.broadcasted_iota(jnp.int32, (n, 1), 0)).astype(F32)
    cnt = jnp.minimum(pos + 1.0, winl)
    pooled = win / cnt - u
    return (_dot(pooled.astype(BF16), wpool) * spool).astype(BF16)


def _inproj_kernel(x_ref, g_ref, w_ref, bf_ref, bg_ref, *rest, offs, prompt, tiles_per_batch, tm, n_gate_chunks):
    if prompt:
        (tri_ref, pq_ref, pk_ref, oq_ref, ok_ref, wpool_ref, spool_ref, winl_ref, hsel_ref,
         u_ref, po_ref, q_ref, qx_ref, kb_ref, kx_ref, vb_ref, k_ref, v_ref, lf_ref, qc_ref, gt_ref, st_ref,
         carry_ref, hist_ref) = rest
    else:
        (u_ref, q_ref, k_ref, v_ref, lf_ref, qc_ref, gt_ref) = rest

    h = _rms(x_ref[...], g_ref[...]).astype(BF16)

    def sec(s):
        return _dot(h, w_ref[:, offs[s]:offs[s + 1]])

    u = sec(SEC_U)
    u_ref[...] = u
    q = sec(SEC_Q)
    k = sec(SEC_K)
    v = sec(SEC_V)
    k_ref[...] = k
    v_ref[...] = v
    qc_ref[...] = sec(SEC_CQ).astype(BF16)
    gw = (offs[SEC_G + 1] - offs[SEC_G]) // n_gate_chunks
    for c in range(n_gate_chunks):
        a = offs[SEC_G] + c * gw
        z = _dot(h, w_ref[:, a:a + gw]) + bg_ref[:, c * gw:(c + 1) * gw]
        gt_ref[:, c * gw:(c + 1) * gw] = jax.nn.sigmoid(z).astype(BF16)
    zf = sec(SEC_F) + bf_ref[...]
    lane = lax.broadcasted_iota(jnp.int32, zf.shape, 1)
    logf = jnp.where(lane < FOX_H, _log_sigmoid(zf), 0.0)
    lf_ref[...] = logf[:, :FOX_H]

    if not prompt:
        q_ref[...] = q.astype(BF16)
        return

    tib = pl.program_id(0) % tiles_per_batch

    @pl.when(tib == 0)
    def _():
        carry_ref[...] = jnp.zeros_like(carry_ref)
        hist_ref[...] = jnp.zeros_like(hist_ref)

    n_pair = q.shape[1] // LANES
    for hp in range(n_pair):
        sl = slice(hp * LANES, (hp + 1) * LANES)
        q_ref[hp] = q[:, sl].T.astype(BF16)
        kb_ref[hp] = k[:, sl].astype(BF16)
        vt = v[:, sl].T.astype(BF16)
        for a in range(2):
            vb_ref[2 * hp + a, 0, 0:HEAD_DIM, :] = vt[a * HEAD_DIM:(a + 1) * HEAD_DIM, :]
            vb_ref[2 * hp + a, 0, HEAD_DIM:V_ROWS, :] = jnp.ones((V_ROWS - HEAD_DIM, tm), BF16)

    tri = tri_ref[...]
    hi, mid, lo = _split3(logf)
    cum = _dot(tri, hi) + _dot(tri, mid) + _dot(tri, lo) + carry_ref[...]
    carry_ref[...] = cum[tm - 1:tm, :]
    cum2 = cum * LOG2E
    cp = jnp.concatenate(_split3(cum2), axis=1)
    qx_ref[...] = (_dot(cp, pq_ref[...]) + oq_ref[...]).T.astype(BF16)
    kx_ref[...] = (_dot(cp, pk_ref[...]) + ok_ref[...]).astype(BF16)

    def max_sq_norm(a):
        ab = a.astype(BF16).astype(F32)
        sq = ab * ab
        sh = sq.astype(BF16)
        sl = (sq - sh.astype(F32)).astype(BF16)
        return jnp.max(_dot(sh, hsel_ref[...]) + _dot(sl, hsel_ref[...]), axis=0, keepdims=True)

    srow = lax.broadcasted_iota(jnp.int32, (8, LANES), 0)
    st_ref[0] = jnp.where(srow == 0, max_sq_norm(q),
                          jnp.where(srow == 1, max_sq_norm(k),
                                    jnp.where(srow == 2, cum2[0:1, :],
                                              jnp.where(srow == 3, cum2[tm - 1:tm, :], 0.0))))

    ext = jnp.concatenate([hist_ref[...], u], axis=0)
    hist_ref[...] = u[tm - HIST_ROWS:, :]
    po_ref[...] = _pool_mix(ext, u, tib * tm, winl_ref[...], wpool_ref[...], spool_ref[...])


def _inproj(x, g_mix, w_cat, b_f, b_g, offs, *, prompt, seq_len=None, consts=None, tm=512):
    t, d = x.shape
    tm = min(tm, t)
    n_tiles = t // tm
    pool_w = offs[SEC_U + 1] - offs[SEC_U]
    fox_w = offs[SEC_Q + 1] - offs[SEC_Q]
    cross_w = offs[SEC_CQ + 1] - offs[SEC_CQ]
    gate_w = offs[SEC_G + 1] - offs[SEC_G]
    n_pair = fox_w // LANES
    row = lambda w: pl.BlockSpec((tm, w), lambda i: (i, 0))
    pair = pl.BlockSpec((n_pair, tm, LANES), lambda i: (0, i, 0))
    in_specs = [row(d), _const_spec((1, d)),
                pl.BlockSpec(w_cat.shape, lambda i: (0, 0), pipeline_mode=pl.Buffered(1)),
                _const_spec((1, LANES)), _const_spec((1, gate_w))]
    args = [x, g_mix, w_cat, b_f, b_g]
    sd = jax.ShapeDtypeStruct
    if prompt:
        in_specs += [_const_spec(a.shape) for a in consts]
        args += list(consts)
        pair_t = pl.BlockSpec((n_pair, LANES, tm), lambda i: (0, 0, i))
        head_t = pl.BlockSpec((FOX_H, 1, V_ROWS, tm), lambda i: (0, i, 0, 0))
        out_shape = [sd((t, pool_w), F32), sd((t, pool_w), BF16),
                     sd((n_pair, LANES, t), BF16), sd((LANES, t), BF16),
                     sd((n_pair, t, LANES), BF16), sd((t, LANES), BF16),
                     sd((FOX_H, n_tiles, V_ROWS, tm), BF16),
                     sd((t, fox_w), F32), sd((t, fox_w), F32), sd((t, FOX_H), F32),
                     sd((t, cross_w), BF16), sd((t, gate_w), BF16), sd((n_tiles, 8, LANES), F32)]
        out_specs = [row(pool_w), row(pool_w), pair_t, pl.BlockSpec((LANES, tm), lambda i: (0, i)),
                     pair, row(LANES), head_t,
                     row(fox_w), row(fox_w), row(FOX_H), row(cross_w), row(gate_w),
                     pl.BlockSpec((1, 8, LANES), lambda i: (i, 0, 0))]
        scratch = [pltpu.VMEM((1, LANES), F32), pltpu.VMEM((HIST_ROWS, pool_w), F32)]
        tiles_per_batch = seq_len // tm
    else:
        out_shape = [sd((t, pool_w), F32), sd((t, fox_w), BF16), sd((t, fox_w), F32), sd((t, fox_w), F32),
                     sd((t, FOX_H), F32), sd((t, cross_w), BF16), sd((t, gate_w), BF16)]
        out_specs = [row(pool_w), row(fox_w), row(fox_w), row(fox_w), row(FOX_H), row(cross_w), row(gate_w)]
        scratch = []
        tiles_per_batch = 1
    kern = functools.partial(_inproj_kernel, offs=offs, prompt=prompt, tiles_per_batch=tiles_per_batch,
                             tm=tm, n_gate_chunks=N_BRANCH)
    return pl.pallas_call(
        kern, grid=(n_tiles,), in_specs=in_specs, out_specs=out_specs, out_shape=out_shape,
        scratch_shapes=scratch, compiler_params=_cparams(1),
        name="inproj_prompt" if prompt else "inproj_sample")(*args)


def _memkv_kernel(m_ref, g_ref, w_ref, k_ref, v_ref, *, cw):
    h = _rms(m_ref[...], g_ref[...]).astype(BF16)
    kv = _dot(h, w_ref[...])
    k_ref[...] = kv[:, :cw]
    v_ref[...] = kv[:, cw:]


def _memkv(mem, g_mem, w_kv, tm=256):
    t, d = mem.shape
    cw = w_kv.shape[1] // 2
    tm = min(tm, t)
    row = lambda w: pl.BlockSpec((tm, w), lambda i: (i, 0))
    return pl.pallas_call(
        functools.partial(_memkv_kernel, cw=cw), grid=(t // tm,),
        in_specs=[row(d), _const_spec((1, d)), _const_spec(w_kv.shape)],
        out_specs=[row(cw), row(cw)],
        out_shape=[jax.ShapeDtypeStruct((t, cw), F32)] * 2,
        compiler_params=_cparams(1), name="memkv")(mem, g_mem, w_kv)


def _cross_kernel(q_ref, k_ref, v_ref, o_ref):
    q = q_ref[...]
    kk = k_ref[0].astype(BF16)
    vv = v_ref[0].astype(BF16)
    lane = lax.broadcasted_iota(jnp.int32, (1, q.shape[1]), 1) // HEAD_DIM
    out = jnp.zeros(q.shape, F32)
    for h in range(q.shape[1] // HEAD_DIM):
        hm = lane == h
        s = _dot_nt(jnp.where(hm, q, jnp.zeros_like(q)), kk)
        m = jnp.max(s, axis=1, keepdims=True)
        p = jnp.exp(s - m)
        l = jnp.sum(p, axis=1, keepdims=True)
        o = _dot(p.astype(BF16), vv) / l
        out = jnp.where(hm, o, out)
    o_ref[...] = out.astype(BF16)


def _cross(qc, mk, mv, n_batch, tl=512):
    t, cw = qc.shape
    l = t // n_batch
    tl = min(tl, l)
    nl = l // tl
    m = mk.shape[1]
    return pl.pallas_call(
        _cross_kernel, grid=(n_batch, nl),
        in_specs=[pl.BlockSpec((tl, cw), lambda b, i: (b * nl + i, 0)),
                  pl.BlockSpec((1, m, cw), lambda b, i: (b, 0, 0)),
                  pl.BlockSpec((1, m, cw), lambda b, i: (b, 0, 0))],
        out_specs=pl.BlockSpec((tl, cw), lambda b, i: (b * nl + i, 0)),
        out_shape=jax.ShapeDtypeStruct((t, cw), BF16),
        compiler_params=_cparams(2), name="cross_attn")(qc, mk, mv)


def _fox_kernel(kf_ref, hm_ref, q_ref, qx_ref, k_ref, kx_ref, v_ref, o_ref,
                m0_ref, m1_ref, acc0_ref, acc1_ref, s0_ref, *, tq, n_pair, nq):
    qi = pl.program_id(2)
    qt = q_ref[0]
    qxt = qx_ref[...]
    row = lax.broadcasted_iota(jnp.int32, (LANES, 1), 0)
    zero = jnp.zeros_like(qt)
    qs = []
    for a in range(2):
        half = (row < HEAD_DIM) if a == 0 else (row >= HEAD_DIM)
        qs.append(jnp.concatenate([jnp.where(half, qt, zero), jnp.where(hm_ref[a] > 0.5, qxt, zero)], axis=0))
    ms = (m0_ref, m1_ref)
    accs = (acc0_ref, acc1_ref)
    for a in range(2):
        ms[a][...] = jnp.full_like(ms[a], NEG)
        accs[a][...] = jnp.zeros_like(accs[a])

    def scores(a, ki):
        ks = pl.multiple_of(ki * tq, tq)
        kcat = jnp.concatenate([k_ref[0, pl.ds(ks, tq), :], kx_ref[pl.ds(ks, tq), :]], axis=1)
        return _dot(kcat, qs[a])

    def softmax_pv(a, s, ki, masked):
        if masked:
            r = lax.broadcasted_iota(jnp.int32, s.shape, 0)
            c = lax.broadcasted_iota(jnp.int32, s.shape, 1)
            s = jnp.where(r <= c, s, NEG)
        m_old = ms[a][...]
        m_new = jnp.maximum(m_old, jnp.max(s, axis=0, keepdims=True))
        alpha = jnp.exp2(m_old - m_new)
        p = jnp.exp2(s - m_new).astype(BF16)
        accs[a][...] = alpha * accs[a][...] + _dot(v_ref[a, ki], p)
        ms[a][...] = m_new

    k_first = kf_ref[(pl.program_id(0) * n_pair + pl.program_id(1)) * nq + qi]
    s0_ref[...] = scores(0, qi)
    s1 = scores(1, qi)
    softmax_pv(0, s0_ref[...], qi, True)
    s0_ref[...] = scores(0, jnp.maximum(qi - 1, 0))
    softmax_pv(1, s1, qi, True)

    def body(j, carry):
        ki = qi - 1 - j

        @pl.when(ki >= k_first)
        def _():
            s1 = scores(1, ki)
            softmax_pv(0, s0_ref[...], ki, False)
            s0_ref[...] = scores(0, jnp.maximum(ki - 1, 0))
            softmax_pv(1, s1, ki, False)
        return carry

    lax.fori_loop(0, qi, body, 0)
    outs = [accs[a][0:HEAD_DIM, :] / accs[a][HEAD_DIM:HEAD_DIM + 1, :] for a in range(2)]
    o_ref[...] = jnp.concatenate(outs, axis=0).T.astype(BF16)


def _first_needed_block(stats, n_batch, n_pair):
    nt = stats.shape[0]
    nq = nt // n_batch
    st = stats[:, :4, :FOX_H].reshape(n_batch, nq, 4, FOX_H)
    slack = 1.0 + 2.0 ** -8
    qn, kn = jnp.sqrt(st[:, :, 0]) * slack, jnp.sqrt(st[:, :, 1]) * slack
    c_first, c_last = st[:, :, 2], st[:, :, 3]
    upper = qn[:, :, None, :] * kn[:, None, :, :] + (c_first[:, :, None, :] - c_last[:, None, :, :])
    lower = -(qn * kn)[:, :, None, :]
    qi = jnp.arange(nq)[None, :, None, None]
    ki = jnp.arange(nq)[None, None, :, None]
    needed = (ki == qi) | ((ki < qi) & ~(upper - lower + 2.0 < -ZERO_EXP2))
    first = jnp.min(jnp.where(needed, ki, nq), axis=2).astype(jnp.int32)
    first = jnp.min(first.reshape(n_batch, nq, n_pair, 2), axis=3)
    return jnp.transpose(first, (0, 2, 1)).reshape(-1)


def _fox_prompt(k_first, hmask, q, qx, kb, kx, vb, n_batch):
    n_pair, _, t = q.shape
    _, n_tiles, _, tq = vb.shape
    l = t // n_batch
    nq = l // tq
    once = pl.Buffered(1)
    grid_spec = pltpu.PrefetchScalarGridSpec(
        num_scalar_prefetch=1, grid=(n_batch, n_pair, nq),
        in_specs=[pl.BlockSpec((2, LANES, 1), lambda b, h, i, kf: (h, 0, 0)),
                  pl.BlockSpec((1, LANES, tq), lambda b, h, i, kf: (h, 0, b * nq + i)),
                  pl.BlockSpec((LANES, tq), lambda b, h, i, kf: (0, b * nq + i)),
                  pl.BlockSpec((1, l, LANES), lambda b, h, i, kf: (h, b, 0), pipeline_mode=once),
                  pl.BlockSpec((l, LANES), lambda b, h, i, kf: (b, 0), pipeline_mode=once),
                  pl.BlockSpec((2, nq, V_ROWS, tq), lambda b, h, i, kf: (h, b, 0, 0), pipeline_mode=once)],
        out_specs=pl.BlockSpec((tq, LANES), lambda b, h, i, kf: (b * nq + i, h)),
        scratch_shapes=[pltpu.VMEM((1, tq), F32), pltpu.VMEM((1, tq), F32),
                        pltpu.VMEM((V_ROWS, tq), F32), pltpu.VMEM((V_ROWS, tq), F32), pltpu.VMEM((tq, tq), F32)])
    return pl.pallas_call(
        functools.partial(_fox_kernel, tq=tq, n_pair=n_pair, nq=nq), grid_spec=grid_spec,
        out_shape=jax.ShapeDtypeStruct((t, n_pair * LANES), BF16),
        compiler_params=_cparams(3), name="fox_prompt")(k_first, hmask, q, qx, kb, kx, vb)


def _sample_kernel(q_ref, kn_ref, vn_ref, lfn_ref, u_ref, ck_ref, cv_ref, clf_ref, st_ref,
                   tri_ref, pq_ref, pk_ref, oq_ref, ok_ref, qmask_ref, xmask_ref, wpool_ref, spool_ref, winl_ref,
                   fo_ref, po_ref,
                   ext_ref, lf_ref, cum_ref, kcat_ref, vall_ref, *, past, ls, chunk):
    fw = q_ref.shape[1]
    lk_pad = kcat_ref.shape[0]
    n_heads = fw // HEAD_DIM

    u = u_ref[...]
    ext_ref[...] = jnp.zeros_like(ext_ref)
    ext_ref[pl.ds(HIST_ROWS - POOL_HIST, POOL_HIST), :] = st_ref[0]
    ext_ref[pl.ds(HIST_ROWS, ls), :] = u
    po_ref[...] = _pool_mix(ext_ref[...], u, past, winl_ref[...], wpool_ref[...], spool_ref[...])

    lf_ref[...] = jnp.zeros_like(lf_ref)
    lf_ref[pl.ds(0, past), pl.ds(0, FOX_H)] = clf_ref[0]
    lf_ref[pl.ds(past, ls), pl.ds(0, FOX_H)] = lfn_ref[...]
    tri = tri_ref[...]
    carry = jnp.zeros((1, LANES), F32)
    for c in range(lk_pad // chunk):
        rows = pl.ds(c * chunk, chunk)
        hi, mid, lo = _split3(lf_ref[rows, :])
        cum = _dot(tri, hi) + _dot(tri, mid) + _dot(tri, lo) + carry
        carry = cum[chunk - 1:chunk, :]
        cum_ref[rows, :] = cum
        cp = jnp.concatenate(_split3(cum * LOG2E), axis=1)
        kcat_ref[rows, pl.ds(fw, LANES)] = (_dot(cp, pk_ref[...]) + ok_ref[...]).astype(BF16)

    kcat_ref[pl.ds(0, past), pl.ds(0, fw)] = ck_ref[0].astype(BF16)
    kcat_ref[pl.ds(past, ls), pl.ds(0, fw)] = kn_ref[...].astype(BF16)
    vall_ref[pl.ds(0, past), :] = cv_ref[0].astype(BF16)
    vall_ref[pl.ds(past, ls), :] = vn_ref[...].astype(BF16)
    npad = lk_pad - past - ls
    kcat_ref[pl.ds(past + ls, npad), pl.ds(0, fw)] = jnp.zeros((npad, fw), BF16)
    vall_ref[pl.ds(past + ls, npad), :] = jnp.zeros((npad, fw), BF16)

    cq = cum_ref[pl.ds(past, ls), :]
    cpq = jnp.concatenate(_split3(cq * LOG2E), axis=1)
    qx = (_dot(cpq, pq_ref[...]) + oq_ref[...]).astype(BF16)
    q = q_ref[...]
    qbd = jnp.concatenate([jnp.concatenate([q] * n_heads, axis=0) * qmask_ref[...],
                           jnp.concatenate([qx] * n_heads, axis=0) * xmask_ref[...]], axis=1)
    s = _dot_nt(qbd, kcat_ref[...])
    r = lax.broadcasted_iota(jnp.int32, s.shape, 0) % ls
    c = lax.broadcasted_iota(jnp.int32, s.shape, 1)
    s = jnp.where(c <= r + past, s, NEG)
    m = jnp.max(s, axis=1, keepdims=True)
    p = jnp.exp2(s - m)
    l = jnp.sum(p, axis=1, keepdims=True)
    o = _dot(p.astype(BF16), vall_ref[...]) / l
    lane_h = lax.broadcasted_iota(jnp.int32, (1, fw), 1) // HEAD_DIM
    out = jnp.zeros((ls, fw), F32)
    for h in range(n_heads):
        out = jnp.where(lane_h == h, o[h * ls:(h + 1) * ls, :], out)
    fo_ref[...] = out.astype(BF16)


def _sample_mix(q, kn, vn, lfn, u, ck, cv, clf, st, consts, chunk=256):
    t, fw = q.shape
    nb, past, _ = ck.shape
    ls = t // nb
    pw = u.shape[1]
    lk_pad = -(-(past + ls) // chunk) * chunk
    if lk_pad == past + ls:
        lk_pad += chunk
    n_heads = fw // HEAD_DIM
    hm = _head_lane_masks()
    qmask = np.zeros((n_heads * ls, fw), np.float32)
    xmask = np.zeros((n_heads * ls, LANES), np.float32)
    for h in range(n_heads):
        qmask[h * ls:(h + 1) * ls, h * HEAD_DIM:(h + 1) * HEAD_DIM] = 1.0
        xmask[h * ls:(h + 1) * ls, :] = hm[h]
    tri, pq, pk, oq, ok, wpool, spool, winl = consts[:8]
    tri_c = tri[:chunk, :chunk]
    cargs = [tri_c, pq, pk, oq, ok, jnp.asarray(qmask, dtype=BF16), jnp.asarray(xmask, dtype=BF16), wpool, spool, winl]
    row = lambda w: pl.BlockSpec((ls, w), lambda b: (b, 0))
    bat = lambda a: pl.BlockSpec((1,) + a.shape[1:], lambda b: (b, 0, 0))
    return pl.pallas_call(
        functools.partial(_sample_kernel, past=past, ls=ls, chunk=chunk), grid=(nb,),
        in_specs=[row(fw), row(fw), row(fw), row(FOX_H), row(pw), bat(ck), bat(cv), bat(clf), bat(st)]
                 + [_const_spec(a.shape) for a in cargs],
        out_specs=[row(fw), row(pw)],
        out_shape=[jax.ShapeDtypeStruct((t, fw), BF16), jax.ShapeDtypeStruct((t, pw), BF16)],
        scratch_shapes=[pltpu.VMEM((HIST_ROWS + ls, pw), F32), pltpu.VMEM((lk_pad, LANES), F32),
                        pltpu.VMEM((lk_pad, LANES), F32), pltpu.VMEM((lk_pad, fw + LANES), BF16),
                        pltpu.VMEM((lk_pad, fw), BF16)],
        compiler_params=_cparams(1), name="sample_mix")(q, kn, vn, lfn, u, ck, cv, clf, st, *cargs)


def _merge_kernel(x_ref, po_ref, fo_ref, co_ref, gt_ref, wbp_ref, wbf_ref, wbc_ref, wo_ref, gf_ref,
                  wrh_ref, wrl_ref, br_ref, tri_ref, cin_ref,
                  x1_ref, xn_ref, ri_ref, rw_ref, cnt_ref, carry_ref, *, d, n_exp, tm):
    @pl.when(pl.program_id(0) == 0)
    def _():
        carry_ref[...] = cin_ref[...]

    def gate(j):
        return gt_ref[:, j * d:(j + 1) * d].astype(F32)

    merged = (gate(0) * _dot(po_ref[...], wbp_ref[...])
              + gate(1) * _dot(fo_ref[...], wbf_ref[...])
              + gate(2) * _dot(co_ref[...], wbc_ref[...]))
    x1 = x_ref[...] + _dot(merged.astype(BF16), wo_ref[...])
    x1_ref[...] = x1
    xn = _rms(x1, gf_ref[...])
    xn_ref[...] = xn
    xh = xn.astype(BF16)
    xl = (xn - xh.astype(F32)).astype(BF16)
    logits = _dot(xh, wrh_ref[...]) + _dot(xl, wrh_ref[...]) + _dot(xh, wrl_ref[...]) + br_ref[...]

    lane = lax.broadcasted_iota(jnp.int32, logits.shape, 1)
    lane_f = lane.astype(F32)
    work = jnp.where(lane < n_exp, logits, NEG)
    vals, idxs, sels = [], [], []
    for _ in range(TOP_K):
        mx = jnp.max(work, axis=1, keepdims=True)
        idx = jnp.min(jnp.where(work == mx, lane_f, float(LANES)), axis=1, keepdims=True)
        sel = lane_f == idx
        vals.append(mx)
        idxs.append(idx)
        sels.append(sel)
        work = jnp.where(sel, NEG, work)
    es = [jnp.exp(v - vals[0]) for v in vals]
    den = es[0] + es[1] + es[2] + es[3]

    onehot = jnp.zeros(logits.shape, F32)
    for sel in sels:
        onehot = jnp.where(sel, 1.0, onehot)
    cum = _dot(tri_ref[...], onehot.astype(BF16)) + carry_ref[...]
    carry_ref[...] = cum[tm - 1:tm, :] + onehot[tm - 1:tm, :]
    cnt_ref[...] = carry_ref[...]

    ri = jnp.zeros(logits.shape, jnp.int32)
    rw = jnp.zeros(logits.shape, F32)
    for j in range(TOP_K):
        rank = jnp.sum(jnp.where(sels[j], cum, 0.0), axis=1, keepdims=True)
        ri = jnp.where(lane == j, idxs[j].astype(jnp.int32), ri)
        ri = jnp.where(lane == TOP_K + j, rank.astype(jnp.int32), ri)
        rw = jnp.where(lane == j, es[j] / den, rw)
    ri_ref[...] = ri
    rw_ref[...] = rw


def _merge(x, po, fo, co, gt, wbp, wbf, wbc, wo, g_ffn, wrh, wrl, br, tri_s, cnt_in, n_exp, tm=512):
    t, d = x.shape
    tm = min(tm, t)
    row = lambda a: pl.BlockSpec((tm, a.shape[1]), lambda i: (i, 0))
    consts = [wbp, wbf, wbc, wo, g_ffn, wrh, wrl, br, tri_s, cnt_in]
    sd = jax.ShapeDtypeStruct
    rspec = lambda w: pl.BlockSpec((tm, w), lambda i: (i, 0))
    return pl.pallas_call(
        functools.partial(_merge_kernel, d=d, n_exp=n_exp, tm=tm), grid=(t // tm,),
        in_specs=[row(x), row(po), row(fo), row(co), row(gt)] + [_const_spec(a.shape) for a in consts],
        out_specs=[rspec(d), rspec(d), rspec(LANES), rspec(LANES), _const_spec((1, LANES))],
        out_shape=[sd((t, d), F32), sd((t, d), F32), sd((t, LANES), jnp.int32), sd((t, LANES), F32),
                   sd((1, LANES), F32)],
        scratch_shapes=[pltpu.VMEM((1, LANES), F32)],
        compiler_params=_cparams(1), name="merge_router")(x, po, fo, co, gt, *consts)


def _wprep_kernel(w_ref, g_ref, u_ref, t_ref):
    fc = g_ref.shape[1]
    for c in range(t_ref.shape[0]):
        cols = slice(c * LANES, (c + 1) * LANES)
        t_ref[c] = w_ref[0, cols, :].T
        g_ref[0, :, cols] = t_ref[c, pl.ds(0, fc, stride=2), :].astype(BF16)
        u_ref[0, :, cols] = t_ref[c, pl.ds(1, fc, stride=2), :].astype(BF16)


def _wprep(w_gate_up, fc=256):
    e, d, f2 = w_gate_up.shape
    f = f2 // 2
    fc = min(fc, f)
    out = pl.BlockSpec((1, fc, d), lambda i, c: (i, c, 0))
    return pl.pallas_call(
        _wprep_kernel, grid=(e, f // fc),
        in_specs=[pl.BlockSpec((1, d, 2 * fc), lambda i, c: (i, 0, c))],
        out_specs=[out, out],
        out_shape=[jax.ShapeDtypeStruct((e, f, d), BF16)] * 2,
        scratch_shapes=[pltpu.VMEM((d // LANES, 2 * fc, LANES), F32)],
        compiler_params=_cparams(2), name="expert_weight_layout")(w_gate_up)


def _zero_pad_rows(ps_ref, pl_ref, xs_ref, z_ref, sem, n_exp, n_bits):
    z_ref[...] = jnp.zeros_like(z_ref)
    sub = 8

    def each(fn):
        for e in range(n_exp):
            start = ps_ref[e]
            n = pl_ref[e]
            head = jnp.minimum((-start) & (sub - 1), n)
            for r in range(sub - 1):
                @pl.when(r < head)
                def _():
                    fn(pltpu.make_async_copy(z_ref.at[pl.ds(0, 1), :], xs_ref.at[pl.ds(start + r, 1), :], sem))
            start8 = start + head
            n8 = (n - head) // sub
            for b in range(n_bits - 3):
                rows = sub << b
                off = pl.multiple_of(start8 + sub * (n8 & ((1 << b) - 1)), sub)

                @pl.when(((n8 >> b) & 1) == 1)
                def _():
                    fn(pltpu.make_async_copy(z_ref.at[pl.ds(0, rows), :], xs_ref.at[pl.ds(off, rows), :], sem))

    each(lambda c: c.start())
    each(lambda c: c.wait())


def _dispatch_kernel(pos_ref, x_ref, *rest, tm, first, n_exp, n_bits, n_chunks):
    if first and n_chunks:
        ps_ref, pl_ref, w_ref, xs_ref, g_ref, u_ref, sem, z_ref, t_ref = rest
    elif first:
        ps_ref, pl_ref, xs_ref, sem, z_ref = rest
    else:
        _, xs_ref, sem = rest
    if first:
        @pl.when(pl.program_id(0) == 0)
        def _():
            _zero_pad_rows(ps_ref, pl_ref, xs_ref, z_ref, sem, n_exp, n_bits)

    def issue(t, c):
        for j in range(TOP_K):
            pltpu.make_async_copy(x_ref.at[pl.ds(t, 1), :],
                                  xs_ref.at[pl.ds(pos_ref[t * TOP_K + j], 1), :], sem).start()
        return c

    lax.fori_loop(0, tm, issue, 0, unroll=DMA_UNROLL)
    if first and n_chunks:
        @pl.when(pl.program_id(0) < n_chunks)
        def _():
            _wprep_kernel(w_ref, g_ref, u_ref, t_ref)
    for j in range(TOP_K):
        pltpu.make_async_copy(x_ref, xs_ref.at[pl.ds(0, tm), :], sem).wait()


def _dispatch(pos_flat, xn, n_rows, pad=None, xs=None, w_gate_up=None, tm=512, tile_rows=512):
    t, d = xn.shape
    tm = min(tm, t)
    steps = t // tm
    first = xs is None
    n_bits = (tile_rows - 1).bit_length()
    smem = lambda n: pl.BlockSpec((n,), lambda i: (0,), memory_space=pltpu.SMEM)
    in_specs = [pl.BlockSpec((tm * TOP_K,), lambda i: (i,), memory_space=pltpu.SMEM),
                pl.BlockSpec((tm, d), lambda i: (i, 0))]
    scratch = [pltpu.SemaphoreType.DMA]
    out_specs = [pl.BlockSpec(memory_space=pl.ANY)]
    out_shape = [jax.ShapeDtypeStruct((n_rows, d), F32)]
    n_chunks = 0
    if first:
        n_exp = pad[0].shape[0]
        in_specs += [smem(n_exp), smem(n_exp)]
        args = [pos_flat, xn, pad[0], pad[1]]
        scratch.append(pltpu.VMEM((1 << (n_bits - 1), d), F32))
        aliases = {}
        if w_gate_up is not None:
            e, _, f2 = w_gate_up.shape
            f = f2 // 2
            per_exp = steps // e
            assert per_exp >= 1 and f % per_exp == 0
            fc = f // per_exp
            n_chunks = e * per_exp
            chunk = lambda i: jnp.minimum(i, n_chunks - 1)
            in_specs.append(pl.BlockSpec((1, d, 2 * fc), lambda i: (chunk(i) // per_exp, 0, chunk(i) % per_exp)))
            wout = pl.BlockSpec((1, fc, d), lambda i: (chunk(i) // per_exp, chunk(i) % per_exp, 0))
            out_specs += [wout, wout]
            out_shape += [jax.ShapeDtypeStruct((e, f, d), BF16)] * 2
            scratch.append(pltpu.VMEM((d // LANES, 2 * fc, LANES), F32))
            args.append(w_gate_up)
    else:
        n_exp = 0
        in_specs.append(pl.BlockSpec(memory_space=pl.ANY))
        args = [pos_flat, xn, xs]
        aliases = {2: 0}
    out = pl.pallas_call(
        functools.partial(_dispatch_kernel, tm=tm, first=first, n_exp=n_exp, n_bits=n_bits, n_chunks=n_chunks),
        grid=(steps,), in_specs=in_specs, out_specs=out_specs, out_shape=out_shape,
        scratch_shapes=scratch, input_output_aliases=aliases,
        compiler_params=_cparams(1), name="dispatch")(*args)
    return out if n_chunks else out[0]


def _ffn_kernel(te_ref, nu_ref, x_ref, wg_ref, wu_ref, wd_ref, bg_ref, bu_ref, bd_ref, y_ref):
    @pl.when(pl.program_id(0) < nu_ref[0])
    def _():
        x = x_ref[...].astype(BF16)
        g = _dot_nt(x, wg_ref[0]) + bg_ref[0]
        u = _dot_nt(x, wu_ref[0]) + bu_ref[0]
        gate = jnp.minimum(g, SWIGLU_LIMIT)
        up = jnp.clip(u, -SWIGLU_LIMIT, SWIGLU_LIMIT)
        act = (up + 1.0) * gate * jax.nn.sigmoid(SWIGLU_ALPHA * gate)
        y_ref[...] = _dot(act.astype(BF16), wd_ref[0]) + bd_ref[0]


def _ffn(tile_exp, n_used, xs, wg_t, wu_t, wd, bg, bu, bd, tm):
    p, d = xs.shape
    e, f, _ = wg_t.shape
    nt = p // tm
    rowi = lambda i, te, nu: (jnp.minimum(i, nu[0] - 1), 0)
    wi = lambda i, te, nu: (te[i], 0, 0)
    grid_spec = pltpu.PrefetchScalarGridSpec(
        num_scalar_prefetch=2, grid=(nt,),
        in_specs=[pl.BlockSpec((tm, d), rowi),
                  pl.BlockSpec((1, f, d), wi), pl.BlockSpec((1, f, d), wi), pl.BlockSpec((1, f, d), wi),
                  pl.BlockSpec((1, 1, f), wi), pl.BlockSpec((1, 1, f), wi), pl.BlockSpec((1, 1, d), wi)],
        out_specs=pl.BlockSpec((tm, d), rowi))
    return pl.pallas_call(
        _ffn_kernel, grid_spec=grid_spec, out_shape=jax.ShapeDtypeStruct((p, d), F32),
        compiler_params=_cparams(1), name="expert_ffn")(tile_exp, n_used, xs, wg_t, wu_t, wd, bg, bu, bd)


def _combine_kernel(pos_ref, posn_ref, w_ref, x1_ref, gf_ref, y_ref, o_ref, buf_ref, sems, *, tm):
    i = pl.program_id(0)
    n = pl.num_programs(0)
    slot = i % 2

    def gather(idx_ref, s):
        def issue(t, c):
            for j in range(TOP_K):
                pltpu.make_async_copy(y_ref.at[pl.ds(idx_ref[t * TOP_K + j], 1), :],
                                      buf_ref.at[s, j, pl.ds(t, 1), :], sems.at[s]).start()
            return c

        lax.fori_loop(0, tm, issue, 0, unroll=DMA_UNROLL)

    @pl.when(i == 0)
    def _():
        gather(pos_ref, slot)

    @pl.when(i + 1 < n)
    def _():
        gather(posn_ref, 1 - slot)

    for j in range(TOP_K):
        pltpu.make_async_copy(y_ref.at[pl.ds(0, tm), :], buf_ref.at[slot, j], sems.at[slot]).wait()
    w = w_ref[...]
    acc = x1_ref[...]
    for j in range(TOP_K):
        acc = acc + w[:, j:j + 1] * buf_ref[slot, j]
    o_ref[...] = _rms(acc, gf_ref[...])


def _combine(pos_flat, rw, x1, g_final, y, tm=256):
    t, d = x1.shape
    tm = min(tm, t)
    n = t // tm
    return pl.pallas_call(
        functools.partial(_combine_kernel, tm=tm), grid=(n,),
        in_specs=[pl.BlockSpec((tm * TOP_K,), lambda i: (i,), memory_space=pltpu.SMEM),
                  pl.BlockSpec((tm * TOP_K,), lambda i: (jnp.minimum(i + 1, n - 1),), memory_space=pltpu.SMEM),
                  pl.BlockSpec((tm, LANES), lambda i: (i, 0)),
                  pl.BlockSpec((tm, d), lambda i: (i, 0)),
                  _const_spec((1, d)),
                  pl.BlockSpec(memory_space=pl.ANY)],
        out_specs=pl.BlockSpec((tm, d), lambda i: (i, 0)),
        out_shape=jax.ShapeDtypeStruct((t, d), F32),
        scratch_shapes=[pltpu.VMEM((2, TOP_K, tm, d), F32), pltpu.SemaphoreType.DMA((2,))],
        compiler_params=_cparams(1), name="combine")(pos_flat, pos_flat, rw, x1, g_final, y)


def kernel(x_prompt, x_sample, mem_prompt, cache_fox_k, cache_fox_v, cache_fox_logf, state_pool, cache_mem_k, cache_mem_v, g_mix, w_in, b_f, w_pool, s_pool, w_br_pool, w_br_fox, w_br_cross, b_gates, w_out, g_mem, w_mem_kv, g_ffn, w_router, b_router, w_gate_up, b_gate_up, w_down, b_down, g_final):
    depth = w_in.shape[0]
    assert depth == 1, "single-layer model"
    bp, lp, d = x_prompt.shape
    bs, ls, _ = x_sample.shape
    past = cache_fox_k.shape[2]
    n_mem = mem_prompt.shape[1]
    n_exp = w_router.shape[2]
    d_ff = w_down.shape[2]
    pool_w = state_pool.shape[3]
    fox_w = FOX_H * HEAD_DIM
    cross_w = CROSS_H * HEAD_DIM
    scale = HEAD_DIM ** -0.5

    w = w_in[0]
    o_q = pool_w
    o_k, o_v, o_f = o_q + fox_w, o_q + 2 * fox_w, o_q + 3 * fox_w
    o_cq = o_f + FOX_H
    o_g = o_cq + cross_w
    w_cat = jnp.concatenate(
        [w[:, :o_q], w[:, o_q:o_k] * (scale * LOG2E), w[:, o_k:o_v], w[:, o_v:o_f], w[:, o_cq:o_g] * scale, w[:, o_g:],
         jnp.pad(w[:, o_f:o_cq], ((0, 0), (0, LANES - FOX_H)))], axis=1).astype(BF16)
    widths = [pool_w, fox_w, fox_w, fox_w, cross_w, N_BRANCH * d, LANES]
    offs = tuple(int(v) for v in np.concatenate([[0], np.cumsum(widths)]))
    b_f_p = jnp.pad(b_f[0], (0, LANES - FOX_H)).reshape(1, LANES)
    b_g = b_gates[0].reshape(1, -1)
    g_mix2 = g_mix[0].reshape(1, d)
    gw = pool_w // len(POOL_WINDOWS)
    wpool_bd = jnp.zeros((pool_w, pool_w), F32)
    for g in range(len(POOL_WINDOWS)):
        wpool_bd = wpool_bd.at[g * gw:(g + 1) * gw, g * gw:(g + 1) * gw].set(w_pool[0, g])
    tm = 512
    pq, pk, oq, ok = _bias_placement()
    consts = (_tri(tm, strict=False), pq, pk, oq, ok, wpool_bd.astype(BF16), s_pool[0].reshape(1, pool_w),
              _pool_lane_windows(pool_w), _head_selector(fox_w))
    hmask = jnp.asarray(_head_lane_masks()[:, :, None])

    xp = x_prompt.reshape(bp * lp, d)
    (u_p, po_p, q_p, qx_p, kb_p, kx_p, vb_p, k_p, v_p, lf_p, qc_p, gt_p, stats_p) = _inproj(
        xp, g_mix2, w_cat, b_f_p, b_g, offs, prompt=True, seq_len=lp, consts=consts, tm=tm)
    mk, mv = _memkv(mem_prompt.reshape(bp * n_mem, d), g_mem[0].reshape(1, d), w_mem_kv[0].astype(BF16))
    mk3, mv3 = mk.reshape(bp, n_mem, cross_w), mv.reshape(bp, n_mem, cross_w)
    k_first = _first_needed_block(stats_p, bp, fox_w // LANES)
    fo_p = _fox_prompt(k_first, hmask, q_p, qx_p, kb_p, kx_p, vb_p, bp)
    co_p = _cross(qc_p, mk3, mv3, bp)

    xs_ = x_sample.reshape(bs * ls, d)
    (u_s, q_s, k_s, v_s, lf_s, qc_s, gt_s) = _inproj(xs_, g_mix2, w_cat, b_f_p, b_g, offs, prompt=False, tm=tm)
    fo_s, po_s = _sample_mix(q_s, k_s, v_s, lf_s, u_s,
                             cache_fox_k[0].reshape(bs, past, fox_w), cache_fox_v[0].reshape(bs, past, fox_w),
                             cache_fox_logf[0], state_pool[0], consts)
    co_s = _cross(qc_s, cache_mem_k[0].reshape(bs, n_mem, cross_w), cache_mem_v[0].reshape(bs, n_mem, cross_w), bs)

    wbp, wbf, wbc = w_br_pool[0].astype(BF16), w_br_fox[0].astype(BF16), w_br_cross[0].astype(BF16)
    wo = w_out[0].astype(BF16)
    g_ffn2 = g_ffn[0].reshape(1, d)
    wr = jnp.pad(w_router[0], ((0, 0), (0, LANES - n_exp)))
    wrh = wr.astype(BF16)
    wrl = (wr - wrh.astype(F32)).astype(BF16)
    br = jnp.pad(b_router[0], (0, LANES - n_exp)).reshape(1, LANES)
    tri_s = _tri(tm, strict=True)
    margs = (wbp, wbf, wbc, wo, g_ffn2, wrh, wrl, br, tri_s)
    x1_p, xn_p, ri_p, rw_p, cnt_p = _merge(xp, po_p, fo_p, co_p, gt_p, *margs, jnp.zeros((1, LANES), F32), n_exp, tm)
    x1_s, xn_s, ri_s, rw_s, cnt = _merge(xs_, po_s, fo_s, co_s, gt_s, *margs, cnt_p, n_exp, tm)

    tmf = 512
    t_all = bp * lp + bs * ls
    counts = cnt[0, :n_exp].astype(jnp.int32)
    tiles_e = (counts + tmf - 1) // tmf
    tile_end = jnp.cumsum(tiles_e)
    row_off = (tile_end - tiles_e) * tmf
    nt_max = (t_all * TOP_K + n_exp * (tmf - 1)) // tmf + 1
    n_used = tile_end[-1:]
    tile_ids = jnp.minimum(jnp.arange(nt_max, dtype=jnp.int32), n_used[0] - 1)
    tile_exp = jnp.minimum(jnp.sum(tile_ids[:, None] >= tile_end[None, :], axis=1), n_exp - 1).astype(jnp.int32)

    def positions(ri):
        e = ri[:, :TOP_K]
        r = ri[:, TOP_K:2 * TOP_K]
        off = jnp.sum(jnp.where(e[:, :, None] == jnp.arange(n_exp)[None, None, :], row_off[None, None, :], 0), axis=2)
        return (off + r).reshape(-1).astype(jnp.int32)

    pos_p, pos_s = positions(ri_p), positions(ri_s)

    wd = w_down[0].astype(BF16)
    bgu = b_gate_up[0]
    bg_e, bu_e = bgu[:, 0::2].reshape(n_exp, 1, d_ff), bgu[:, 1::2].reshape(n_exp, 1, d_ff)
    bd_e = b_down[0].reshape(n_exp, 1, d)
    pad = ((row_off + counts).astype(jnp.int32), (tiles_e * tmf - counts).astype(jnp.int32))
    if (bp * lp) // min(512, bp * lp) >= n_exp:
        xs_sorted, wg_t, wu_t = _dispatch(pos_p, xn_p, nt_max * tmf, pad=pad, w_gate_up=w_gate_up[0], tile_rows=tmf)
    else:
        wg_t, wu_t = _wprep(w_gate_up[0])
        xs_sorted = _dispatch(pos_p, xn_p, nt_max * tmf, pad=pad, tile_rows=tmf)
    xs_sorted = _dispatch(pos_s, xn_s, nt_max * tmf, xs=xs_sorted, tile_rows=tmf)
    y = _ffn(tile_exp, n_used.astype(jnp.int32), xs_sorted, wg_t, wu_t, wd, bg_e, bu_e, bd_e, tmf)
    g_fin = g_final.reshape(1, d)
    y_p = _combine(pos_p, rw_p, x1_p, g_fin, y)
    y_s = _combine(pos_s, rw_s, x1_s, g_fin, y)

    kv5 = lambda a, b, l: a.reshape(1, b, l, FOX_H, HEAD_DIM)
    return (y_p.reshape(bp, lp, d), y_s.reshape(bs, ls, d),
            kv5(k_p, bp, lp), kv5(v_p, bp, lp), lf_p.reshape(1, bp, lp, FOX_H),
            u_p.reshape(bp, lp, pool_w)[:, lp - POOL_HIST:, :][None],
            mk.reshape(1, bp, n_mem, CROSS_H, HEAD_DIM), mv.reshape(1, bp, n_mem, CROSS_H, HEAD_DIM),
            kv5(k_s, bs, ls), kv5(v_s, bs, ls), lf_s.reshape(1, bs, ls, FOX_H),
            u_s.reshape(bs, ls, pool_w)[:, ls - POOL_HIST:, :][None])
```

```python
import functools

import jax
import jax.numpy as jnp
import numpy as np
from jax import lax
from jax.experimental import pallas as pl
from jax.experimental.pallas import tpu as pltpu

F32 = jnp.float32
BF16 = jnp.bfloat16

HEAD_DIM = 64
FOX_H = 8
CROSS_H = 4
POOL_WINDOWS = (2, 4, 8, 16)
POOL_HIST = 15
N_BRANCH = 3
TOP_K = 4
SWIGLU_LIMIT = 7.0
SWIGLU_ALPHA = 1.702
RMS_EPS = 1e-5
NEG = -1e30
LOG2E = 1.4426950408889634
ZERO_EXP2 = 150.0
V_ROWS = HEAD_DIM + 16

LANES = 128
DMA_UNROLL = 4
HIST_ROWS = 16
VMEM_LIMIT = 56 * 1024 * 1024


def _cparams(n_axes=1, vmem=VMEM_LIMIT):
    return pltpu.CompilerParams(dimension_semantics=("arbitrary",) * n_axes, vmem_limit_bytes=vmem)


def _const_spec(shape):
    nd = len(shape)
    return pl.BlockSpec(shape, lambda *_: (0,) * nd)


def _split3(x):
    hi = x.astype(BF16)
    r = x - hi.astype(F32)
    mid = r.astype(BF16)
    lo = (r - mid.astype(F32)).astype(BF16)
    return hi, mid, lo


def _rms(x, g):
    ms = jnp.mean(x * x, axis=-1, keepdims=True)
    return x * lax.rsqrt(ms + RMS_EPS) * g


def _log_sigmoid(z):
    return jnp.minimum(z, 0.0) - jnp.log1p(jnp.exp(-jnp.abs(z)))


def _dot(a, b):
    return jnp.dot(a, b, preferred_element_type=F32)


def _dot_nt(a, b):
    return lax.dot_general(a, b, (((1,), (1,)), ((), ())), preferred_element_type=F32)


def _tri(n, strict):
    r = np.arange(n)
    m = (r[None, :] < r[:, None]) if strict else (r[None, :] <= r[:, None])
    return jnp.asarray(m.astype(np.float32), dtype=BF16)


def _bias_placement():
    pq = np.zeros((3 * LANES, LANES), np.float32)
    pk = np.zeros((3 * LANES, LANES), np.float32)
    oq = np.zeros((1, LANES), np.float32)
    ok = np.zeros((1, LANES), np.float32)
    for p in range(3):
        for h in range(FOX_H):
            pq[p * LANES + h, 8 * p + h] = 1.0
            pk[p * LANES + h, 24 + 8 * p + h] = -1.0
            oq[0, 24 + 8 * p + h] = 1.0
            ok[0, 8 * p + h] = 1.0
    return (jnp.asarray(pq, dtype=BF16), jnp.asarray(pk, dtype=BF16), jnp.asarray(oq), jnp.asarray(ok))


def _head_lane_masks():
    m = np.zeros((FOX_H, LANES), np.float32)
    for p in range(3):
        for h in range(FOX_H):
            m[h, 8 * p + h] = 1.0
            m[h, 24 + 8 * p + h] = 1.0
    return m


def _head_selector(fox_w):
    m = np.zeros((fox_w, LANES), np.float32)
    for h in range(fox_w // HEAD_DIM):
        m[h * HEAD_DIM:(h + 1) * HEAD_DIM, h] = 1.0
    return jnp.asarray(m, dtype=BF16)


def _pool_lane_windows(pool_w):
    gw = pool_w // len(POOL_WINDOWS)
    return jnp.asarray(np.repeat(np.asarray(POOL_WINDOWS, np.float32), gw)[None, :])


SEC_U, SEC_Q, SEC_K, SEC_V, SEC_CQ, SEC_G, SEC_F = range(7)


def _pool_mix(ext, u, row0, winl, wpool, spool):
    n = u.shape[0]
    s1 = ext + pltpu.roll(ext, 1, 0)
    s2 = s1 + pltpu.roll(s1, 2, 0)
    s3 = s2 + pltpu.roll(s2, 4, 0)
    s4 = s3 + pltpu.roll(s3, 8, 0)
    win = jnp.where(winl == 2.0, s1, jnp.where(winl == 4.0, s2, jnp.where(winl == 8.0, s3, s4)))
    win = win[HIST_ROWS:, :]
    pos = (row0 + lax.broadcasted_iota(jnp.int32, (n, 1), 0)).astype(F32)
    cnt = jnp.minimum(pos + 1.0, winl)
    pooled = win / cnt - u
    return (_dot(pooled.astype(BF16), wpool) * spool).astype(BF16)


def _inproj_kernel(x_ref, g_ref, w_ref, bf_ref, bg_ref, *rest, offs, prompt, tiles_per_batch, tm, n_gate_chunks):
    if prompt:
        (tri_ref, pq_ref, pk_ref, oq_ref, ok_ref, wpool_ref, spool_ref, winl_ref, hsel_ref,
         u_ref, po_ref, q_ref, qx_ref, kb_ref, kx_ref, vb_ref, k_ref, v_ref, lf_ref, qc_ref, gt_ref, st_ref,
         carry_ref, hist_ref) = rest
    else:
        (u_ref, q_ref, k_ref, v_ref, lf_ref, qc_ref, gt_ref) = rest

    h = _rms(x_ref[...], g_ref[...]).astype(BF16)

    def sec(s):
        return _dot(h, w_ref[:, offs[s]:offs[s + 1]])

    u = sec(SEC_U)
    u_ref[...] = u
    q = sec(SEC_Q)
    k = sec(SEC_K)
    v = sec(SEC_V)
    k_ref[...] = k
    v_ref[...] = v
    qc_ref[...] = sec(SEC_CQ).astype(BF16)
    gw = (offs[SEC_G + 1] - offs[SEC_G]) // n_gate_chunks
    for c in range(n_gate_chunks):
        a = offs[SEC_G] + c * gw
        z = _dot(h, w_ref[:, a:a + gw]) + bg_ref[:, c * gw:(c + 1) * gw]
        gt_ref[:, c * gw:(c + 1) * gw] = jax.nn.sigmoid(z).astype(BF16)
    zf = sec(SEC_F) + bf_ref[...]
    lane = lax.broadcasted_iota(jnp.int32, zf.shape, 1)
    logf = jnp.where(lane < FOX_H, _log_sigmoid(zf), 0.0)
    lf_ref[...] = logf[:, :FOX_H]

    if not prompt:
        q_ref[...] = q.astype(BF16)
        return

    tib = pl.program_id(0) % tiles_per_batch

    @pl.when(tib == 0)
    def _():
        carry_ref[...] = jnp.zeros_like(carry_ref)
        hist_ref[...] = jnp.zeros_like(hist_ref)

    n_pair = q.shape[1] // LANES
    for hp in range(n_pair):
        sl = slice(hp * LANES, (hp + 1) * LANES)
        q_ref[hp] = q[:, sl].T.astype(BF16)
        kb_ref[hp] = k[:, sl].astype(BF16)
        vt = v[:, sl].T.astype(BF16)
        for a in range(2):
            vb_ref[2 * hp + a, 0, 0:HEAD_DIM, :] = vt[a * HEAD_DIM:(a + 1) * HEAD_DIM, :]
            vb_ref[2 * hp + a, 0, HEAD_DIM:V_ROWS, :] = jnp.ones((V_ROWS - HEAD_DIM, tm), BF16)

    tri = tri_ref[...]
    hi, mid, lo = _split3(logf)
    cum = _dot(tri, hi) + _dot(tri, mid) + _dot(tri, lo) + carry_ref[...]
    carry_ref[...] = cum[tm - 1:tm, :]
    cum2 = cum * LOG2E
    cp = jnp.concatenate(_split3(cum2), axis=1)
    qx_ref[...] = (_dot(cp, pq_ref[...]) + oq_ref[...]).T.astype(BF16)
    kx_ref[...] = (_dot(cp, pk_ref[...]) + ok_ref[...]).astype(BF16)

    def max_sq_norm(a):
        ab = a.astype(BF16).astype(F32)
        sq = ab * ab
        sh = sq.astype(BF16)
        sl = (sq - sh.astype(F32)).astype(BF16)
        return jnp.max(_dot(sh, hsel_ref[...]) + _dot(sl, hsel_ref[...]), axis=0, keepdims=True)

    srow = lax.broadcasted_iota(jnp.int32, (8, LANES), 0)
    st_ref[0] = jnp.where(srow == 0, max_sq_norm(q),
                          jnp.where(srow == 1, max_sq_norm(k),
                                    jnp.where(srow == 2, cum2[0:1, :],
                                              jnp.where(srow == 3, cum2[tm - 1:tm, :], 0.0))))

    ext = jnp.concatenate([hist_ref[...], u], axis=0)
    hist_ref[...] = u[tm - HIST_ROWS:, :]
    po_ref[...] = _pool_mix(ext, u, tib * tm, winl_ref[...], wpool_ref[...], spool_ref[...])


def _inproj(x, g_mix, w_cat, b_f, b_g, offs, *, prompt, seq_len=None, consts=None, tm=512):
    t, d = x.shape
    tm = min(tm, t)
    n_tiles = t // tm
    pool_w = offs[SEC_U + 1] - offs[SEC_U]
    fox_w = offs[SEC_Q + 1] - offs[SEC_Q]
    cross_w = offs[SEC_CQ + 1] - offs[SEC_CQ]
    gate_w = offs[SEC_G + 1] - offs[SEC_G]
    n_pair = fox_w // LANES
    row = lambda w: pl.BlockSpec((tm, w), lambda i: (i, 0))
    pair = pl.BlockSpec((n_pair, tm, LANES), lambda i: (0, i, 0))
    in_specs = [row(d), _const_spec((1, d)),
                pl.BlockSpec(w_cat.shape, lambda i: (0, 0), pipeline_mode=pl.Buffered(1)),
                _const_spec((1, LANES)), _const_spec((1, gate_w))]
    args = [x, g_mix, w_cat, b_f, b_g]
    sd = jax.ShapeDtypeStruct
    if prompt:
        in_specs += [_const_spec(a.shape) for a in consts]
        args += list(consts)
        pair_t = pl.BlockSpec((n_pair, LANES, tm), lambda i: (0, 0, i))
        head_t = pl.BlockSpec((FOX_H, 1, V_ROWS, tm), lambda i: (0, i, 0, 0))
        out_shape = [sd((t, pool_w), F32), sd((t, pool_w), BF16),
                     sd((n_pair, LANES, t), BF16), sd((LANES, t), BF16),
                     sd((n_pair, t, LANES), BF16), sd((t, LANES), BF16),
                     sd((FOX_H, n_tiles, V_ROWS, tm), BF16),
                     sd((t, fox_w), F32), sd((t, fox_w), F32), sd((t, FOX_H), F32),
                     sd((t, cross_w), BF16), sd((t, gate_w), BF16), sd((n_tiles, 8, LANES), F32)]
        out_specs = [row(pool_w), row(pool_w), pair_t, pl.BlockSpec((LANES, tm), lambda i: (0, i)),
                     pair, row(LANES), head_t,
                     row(fox_w), row(fox_w), row(FOX_H), row(cross_w), row(gate_w),
                     pl.BlockSpec((1, 8, LANES), lambda i: (i, 0, 0))]
        scratch = [pltpu.VMEM((1, LANES), F32), pltpu.VMEM((HIST_ROWS, pool_w), F32)]
        tiles_per_batch = seq_len // tm
    else:
        out_shape = [sd((t, pool_w), F32), sd((t, fox_w), BF16), sd((t, fox_w), F32), sd((t, fox_w), F32),
                     sd((t, FOX_H), F32), sd((t, cross_w), BF16), sd((t, gate_w), BF16)]
        out_specs = [row(pool_w), row(fox_w), row(fox_w), row(fox_w), row(FOX_H), row(cross_w), row(gate_w)]
        scratch = []
        tiles_per_batch = 1
    kern = functools.partial(_inproj_kernel, offs=offs, prompt=prompt, tiles_per_batch=tiles_per_batch,
                             tm=tm, n_gate_chunks=N_BRANCH)
    return pl.pallas_call(
        kern, grid=(n_tiles,), in_specs=in_specs, out_specs=out_specs, out_shape=out_shape,
        scratch_shapes=scratch, compiler_params=_cparams(1),
        name="inproj_prompt" if prompt else "inproj_sample")(*args)


def _memkv_kernel(m_ref, g_ref, w_ref, k_ref, v_ref, *, cw):
    h = _rms(m_ref[...], g_ref[...]).astype(BF16)
    kv = _dot(h, w_ref[...])
    k_ref[...] = kv[:, :cw]
    v_ref[...] = kv[:, cw:]


def _memkv(mem, g_mem, w_kv, tm=256):
    t, d = mem.shape
    cw = w_kv.shape[1] // 2
    tm = min(tm, t)
    row = lambda w: pl.BlockSpec((tm, w), lambda i: (i, 0))
    return pl.pallas_call(
        functools.partial(_memkv_kernel, cw=cw), grid=(t // tm,),
        in_specs=[row(d), _const_spec((1, d)), _const_spec(w_kv.shape)],
        out_specs=[row(cw), row(cw)],
        out_shape=[jax.ShapeDtypeStruct((t, cw), F32)] * 2,
        compiler_params=_cparams(1), name="memkv")(mem, g_mem, w_kv)


def _cross_kernel(q_ref, k_ref, v_ref, o_ref):
    q = q_ref[...]
    kk = k_ref[0].astype(BF16)
    vv = v_ref[0].astype(BF16)
    lane = lax.broadcasted_iota(jnp.int32, (1, q.shape[1]), 1) // HEAD_DIM
    out = jnp.zeros(q.shape, F32)
    for h in range(q.shape[1] // HEAD_DIM):
        hm = lane == h
        s = _dot_nt(jnp.where(hm, q, jnp.zeros_like(q)), kk)
        m = jnp.max(s, axis=1, keepdims=True)
        p = jnp.exp(s - m)
        l = jnp.sum(p, axis=1, keepdims=True)
        o = _dot(p.astype(BF16), vv) / l
        out = jnp.where(hm, o, out)
    o_ref[...] = out.astype(BF16)


def _cross(qc, mk, mv, n_batch, tl=512):
    t, cw = qc.shape
    l = t // n_batch
    tl = min(tl, l)
    nl = l // tl
    m = mk.shape[1]
    return pl.pallas_call(
        _cross_kernel, grid=(n_batch, nl),
        in_specs=[pl.BlockSpec((tl, cw), lambda b, i: (b * nl + i, 0)),
                  pl.BlockSpec((1, m, cw), lambda b, i: (b, 0, 0)),
                  pl.BlockSpec((1, m, cw), lambda b, i: (b, 0, 0))],
        out_specs=pl.BlockSpec((tl, cw), lambda b, i: (b * nl + i, 0)),
        out_shape=jax.ShapeDtypeStruct((t, cw), BF16),
        compiler_params=_cparams(2), name="cross_attn")(qc, mk, mv)


def _fox_kernel(kf_ref, hm_ref, q_ref, qx_ref, k_ref, kx_ref, v_ref, o_ref,
                m0_ref, m1_ref, acc0_ref, acc1_ref, s0_ref, *, tq, n_pair, nq):
    qi = pl.program_id(2)
    qt = q_ref[0]
    qxt = qx_ref[...]
    row = lax.broadcasted_iota(jnp.int32, (LANES, 1), 0)
    zero = jnp.zeros_like(qt)
    qs = []
    for a in range(2):
        half = (row < HEAD_DIM) if a == 0 else (row >= HEAD_DIM)
        qs.append(jnp.concatenate([jnp.where(half, qt, zero), jnp.where(hm_ref[a] > 0.5, qxt, zero)], axis=0))
    ms = (m0_ref, m1_ref)
    accs = (acc0_ref, acc1_ref)
    for a in range(2):
        ms[a][...] = jnp.full_like(ms[a], NEG)
        accs[a][...] = jnp.zeros_like(accs[a])

    def scores(a, ki):
        ks = pl.multiple_of(ki * tq, tq)
        kcat = jnp.concatenate([k_ref[0, pl.ds(ks, tq), :], kx_ref[pl.ds(ks, tq), :]], axis=1)
        return _dot(kcat, qs[a])

    def softmax_pv(a, s, ki, masked):
        if masked:
            r = lax.broadcasted_iota(jnp.int32, s.shape, 0)
            c = lax.broadcasted_iota(jnp.int32, s.shape, 1)
            s = jnp.where(r <= c, s, NEG)
        m_old = ms[a][...]
        m_new = jnp.maximum(m_old, jnp.max(s, axis=0, keepdims=True))
        alpha = jnp.exp2(m_old - m_new)
        p = jnp.exp2(s - m_new).astype(BF16)
        accs[a][...] = alpha * accs[a][...] + _dot(v_ref[a, ki], p)
        ms[a][...] = m_new

    k_first = kf_ref[(pl.program_id(0) * n_pair + pl.program_id(1)) * nq + qi]
    s0_ref[...] = scores(0, qi)
    s1 = scores(1, qi)
    softmax_pv(0, s0_ref[...], qi, True)
    s0_ref[...] = scores(0, jnp.maximum(qi - 1, 0))
    softmax_pv(1, s1, qi, True)

    def body(j, carry):
        ki = qi - 1 - j

        @pl.when(ki >= k_first)
        def _():
            s1 = scores(1, ki)
            softmax_pv(0, s0_ref[...], ki, False)
            s0_ref[...] = scores(0, jnp.maximum(ki - 1, 0))
            softmax_pv(1, s1, ki, False)
        return carry

    lax.fori_loop(0, qi, body, 0)
    outs = [accs[a][0:HEAD_DIM, :] / accs[a][HEAD_DIM:HEAD_DIM + 1, :] for a in range(2)]
    o_ref[...] = jnp.concatenate(outs, axis=0).T.astype(BF16)


def _first_needed_block(stats, n_batch, n_pair):
    nt = stats.shape[0]
    nq = nt // n_batch
    st = stats[:, :4, :FOX_H].reshape(n_batch, nq, 4, FOX_H)
    slack = 1.0 + 2.0 ** -8
    qn, kn = jnp.sqrt(st[:, :, 0]) * slack, jnp.sqrt(st[:, :, 1]) * slack
    c_first, c_last = st[:, :, 2], st[:, :, 3]
    upper = qn[:, :, None, :] * kn[:, None, :, :] + (c_first[:, :, None, :] - c_last[:, None, :, :])
    lower = -(qn * kn)[:, :, None, :]
    qi = jnp.arange(nq)[None, :, None, None]
    ki = jnp.arange(nq)[None, None, :, None]
    needed = (ki == qi) | ((ki < qi) & ~(upper - lower + 2.0 < -ZERO_EXP2))
    first = jnp.min(jnp.where(needed, ki, nq), axis=2).astype(jnp.int32)
    first = jnp.min(first.reshape(n_batch, nq, n_pair, 2), axis=3)
    return jnp.transpose(first, (0, 2, 1)).reshape(-1)


def _fox_prompt(k_first, hmask, q, qx, kb, kx, vb, n_batch):
    n_pair, _, t = q.shape
    _, n_tiles, _, tq = vb.shape
    l = t // n_batch
    nq = l // tq
    once = pl.Buffered(1)
    grid_spec = pltpu.PrefetchScalarGridSpec(
        num_scalar_prefetch=1, grid=(n_batch, n_pair, nq),
        in_specs=[pl.BlockSpec((2, LANES, 1), lambda b, h, i, kf: (h, 0, 0)),
                  pl.BlockSpec((1, LANES, tq), lambda b, h, i, kf: (h, 0, b * nq + i)),
                  pl.BlockSpec((LANES, tq), lambda b, h, i, kf: (0, b * nq + i)),
                  pl.BlockSpec((1, l, LANES), lambda b, h, i, kf: (h, b, 0), pipeline_mode=once),
                  pl.BlockSpec((l, LANES), lambda b, h, i, kf: (b, 0), pipeline_mode=once),
                  pl.BlockSpec((2, nq, V_ROWS, tq), lambda b, h, i, kf: (h, b, 0, 0), pipeline_mode=once)],
        out_specs=pl.BlockSpec((tq, LANES), lambda b, h, i, kf: (b * nq + i, h)),
        scratch_shapes=[pltpu.VMEM((1, tq), F32), pltpu.VMEM((1, tq), F32),
                        pltpu.VMEM((V_ROWS, tq), F32), pltpu.VMEM((V_ROWS, tq), F32), pltpu.VMEM((tq, tq), F32)])
    return pl.pallas_call(
        functools.partial(_fox_kernel, tq=tq, n_pair=n_pair, nq=nq), grid_spec=grid_spec,
        out_shape=jax.ShapeDtypeStruct((t, n_pair * LANES), BF16),
        compiler_params=_cparams(3), name="fox_prompt")(k_first, hmask, q, qx, kb, kx, vb)


def _sample_kernel(q_ref, kn_ref, vn_ref, lfn_ref, u_ref, ck_ref, cv_ref, clf_ref, st_ref,
                   tri_ref, pq_ref, pk_ref, oq_ref, ok_ref, qmask_ref, xmask_ref, wpool_ref, spool_ref, winl_ref,
                   fo_ref, po_ref,
                   ext_ref, lf_ref, cum_ref, kcat_ref, vall_ref, *, past, ls, chunk):
    fw = q_ref.shape[1]
    lk_pad = kcat_ref.shape[0]
    n_heads = fw // HEAD_DIM

    u = u_ref[...]
    ext_ref[...] = jnp.zeros_like(ext_ref)
    ext_ref[pl.ds(HIST_ROWS - POOL_HIST, POOL_HIST), :] = st_ref[0]
    ext_ref[pl.ds(HIST_ROWS, ls), :] = u
    po_ref[...] = _pool_mix(ext_ref[...], u, past, winl_ref[...], wpool_ref[...], spool_ref[...])

    lf_ref[...] = jnp.zeros_like(lf_ref)
    lf_ref[pl.ds(0, past), pl.ds(0, FOX_H)] = clf_ref[0]
    lf_ref[pl.ds(past, ls), pl.ds(0, FOX_H)] = lfn_ref[...]
    tri = tri_ref[...]
    carry = jnp.zeros((1, LANES), F32)
    for c in range(lk_pad // chunk):
        rows = pl.ds(c * chunk, chunk)
        hi, mid, lo = _split3(lf_ref[rows, :])
        cum = _dot(tri, hi) + _dot(tri, mid) + _dot(tri, lo) + carry
        carry = cum[chunk - 1:chunk, :]
        cum_ref[rows, :] = cum
        cp = jnp.concatenate(_split3(cum * LOG2E), axis=1)
        kcat_ref[rows, pl.ds(fw, LANES)] = (_dot(cp, pk_ref[...]) + ok_ref[...]).astype(BF16)

    kcat_ref[pl.ds(0, past), pl.ds(0, fw)] = ck_ref[0].astype(BF16)
    kcat_ref[pl.ds(past, ls), pl.ds(0, fw)] = kn_ref[...].astype(BF16)
    vall_ref[pl.ds(0, past), :] = cv_ref[0].astype(BF16)
    vall_ref[pl.ds(past, ls), :] = vn_ref[...].astype(BF16)
    npad = lk_pad - past - ls
    kcat_ref[pl.ds(past + ls, npad), pl.ds(0, fw)] = jnp.zeros((npad, fw), BF16)
    vall_ref[pl.ds(past + ls, npad), :] = jnp.zeros((npad, fw), BF16)

    cq = cum_ref[pl.ds(past, ls), :]
    cpq = jnp.concatenate(_split3(cq * LOG2E), axis=1)
    qx = (_dot(cpq, pq_ref[...]) + oq_ref[...]).astype(BF16)
    q = q_ref[...]
    qbd = jnp.concatenate([jnp.concatenate([q] * n_heads, axis=0) * qmask_ref[...],
                           jnp.concatenate([qx] * n_heads, axis=0) * xmask_ref[...]], axis=1)
    s = _dot_nt(qbd, kcat_ref[...])
    r = lax.broadcasted_iota(jnp.int32, s.shape, 0) % ls
    c = lax.broadcasted_iota(jnp.int32, s.shape, 1)
    s = jnp.where(c <= r + past, s, NEG)
    m = jnp.max(s, axis=1, keepdims=True)
    p = jnp.exp2(s - m)
    l = jnp.sum(p, axis=1, keepdims=True)
    o = _dot(p.astype(BF16), vall_ref[...]) / l
    lane_h = lax.broadcasted_iota(jnp.int32, (1, fw), 1) // HEAD_DIM
    out = jnp.zeros((ls, fw), F32)
    for h in range(n_heads):
        out = jnp.where(lane_h == h, o[h * ls:(h + 1) * ls, :], out)
    fo_ref[...] = out.astype(BF16)


def _sample_mix(q, kn, vn, lfn, u, ck, cv, clf, st, consts, chunk=256):
    t, fw = q.shape
    nb, past, _ = ck.shape
    ls = t // nb
    pw = u.shape[1]
    lk_pad = -(-(past + ls) // chunk) * chunk
    if lk_pad == past + ls:
        lk_pad += chunk
    n_heads = fw // HEAD_DIM
    hm = _head_lane_masks()
    qmask = np.zeros((n_heads * ls, fw), np.float32)
    xmask = np.zeros((n_heads * ls, LANES), np.float32)
    for h in range(n_heads):
        qmask[h * ls:(h + 1) * ls, h * HEAD_DIM:(h + 1) * HEAD_DIM] = 1.0
        xmask[h * ls:(h + 1) * ls, :] = hm[h]
    tri, pq, pk, oq, ok, wpool, spool, winl = consts[:8]
    tri_c = tri[:chunk, :chunk]
    cargs = [tri_c, pq, pk, oq, ok, jnp.asarray(qmask, dtype=BF16), jnp.asarray(xmask, dtype=BF16), wpool, spool, winl]
    row = lambda w: pl.BlockSpec((ls, w), lambda b: (b, 0))
    bat = lambda a: pl.BlockSpec((1,) + a.shape[1:], lambda b: (b, 0, 0))
    return pl.pallas_call(
        functools.partial(_sample_kernel, past=past, ls=ls, chunk=chunk), grid=(nb,),
        in_specs=[row(fw), row(fw), row(fw), row(FOX_H), row(pw), bat(ck), bat(cv), bat(clf), bat(st)]
                 + [_const_spec(a.shape) for a in cargs],
        out_specs=[row(fw), row(pw)],
        out_shape=[jax.ShapeDtypeStruct((t, fw), BF16), jax.ShapeDtypeStruct((t, pw), BF16)],
        scratch_shapes=[pltpu.VMEM((HIST_ROWS + ls, pw), F32), pltpu.VMEM((lk_pad, LANES), F32),
                        pltpu.VMEM((lk_pad, LANES), F32), pltpu.VMEM((lk_pad, fw + LANES), BF16),
                        pltpu.VMEM((lk_pad, fw), BF16)],
        compiler_params=_cparams(1), name="sample_mix")(q, kn, vn, lfn, u, ck, cv, clf, st, *cargs)


def _merge_kernel(x_ref, po_ref, fo_ref, co_ref, gt_ref, wbp_ref, wbf_ref, wbc_ref, wo_ref, gf_ref,
                  wrh_ref, wrl_ref, br_ref, tri_ref, cin_ref,
                  x1_ref, xn_ref, ri_ref, rw_ref, cnt_ref, carry_ref, *, d, n_exp, tm):
    @pl.when(pl.program_id(0) == 0)
    def _():
        carry_ref[...] = cin_ref[...]

    def gate(j):
        return gt_ref[:, j * d:(j + 1) * d].astype(F32)

    merged = (gate(0) * _dot(po_ref[...], wbp_ref[...])
              + gate(1) * _dot(fo_ref[...], wbf_ref[...])
              + gate(2) * _dot(co_ref[...], wbc_ref[...]))
    x1 = x_ref[...] + _dot(merged.astype(BF16), wo_ref[...])
    x1_ref[...] = x1
    xn = _rms(x1, gf_ref[...])
    xn_ref[...] = xn
    xh = xn.astype(BF16)
    xl = (xn - xh.astype(F32)).astype(BF16)
    logits = _dot(xh, wrh_ref[...]) + _dot(xl, wrh_ref[...]) + _dot(xh, wrl_ref[...]) + br_ref[...]

    lane = lax.broadcasted_iota(jnp.int32, logits.shape, 1)
    lane_f = lane.astype(F32)
    work = jnp.where(lane < n_exp, logits, NEG)
    vals, idxs, sels = [], [], []
    for _ in range(TOP_K):
        mx = jnp.max(work, axis=1, keepdims=True)
        idx = jnp.min(jnp.where(work == mx, lane_f, float(LANES)), axis=1, keepdims=True)
        sel = lane_f == idx
        vals.append(mx)
        idxs.append(idx)
        sels.append(sel)
        work = jnp.where(sel, NEG, work)
    es = [jnp.exp(v - vals[0]) for v in vals]
    den = es[0] + es[1] + es[2] + es[3]

    onehot = jnp.zeros(logits.shape, F32)
    for sel in sels:
        onehot = jnp.where(sel, 1.0, onehot)
    cum = _dot(tri_ref[...], onehot.astype(BF16)) + carry_ref[...]
    carry_ref[...] = cum[tm - 1:tm, :] + onehot[tm - 1:tm, :]
    cnt_ref[...] = carry_ref[...]

    ri = jnp.zeros(logits.shape, jnp.int32)
    rw = jnp.zeros(logits.shape, F32)
    for j in range(TOP_K):
        rank = jnp.sum(jnp.where(sels[j], cum, 0.0), axis=1, keepdims=True)
        ri = jnp.where(lane == j, idxs[j].astype(jnp.int32), ri)
        ri = jnp.where(lane == TOP_K + j, rank.astype(jnp.int32), ri)
        rw = jnp.where(lane == j, es[j] / den, rw)
    ri_ref[...] = ri
    rw_ref[...] = rw


def _merge(x, po, fo, co, gt, wbp, wbf, wbc, wo, g_ffn, wrh, wrl, br, tri_s, cnt_in, n_exp, tm=512):
    t, d = x.shape
    tm = min(tm, t)
    row = lambda a: pl.BlockSpec((tm, a.shape[1]), lambda i: (i, 0))
    consts = [wbp, wbf, wbc, wo, g_ffn, wrh, wrl, br, tri_s, cnt_in]
    sd = jax.ShapeDtypeStruct
    rspec = lambda w: pl.BlockSpec((tm, w), lambda i: (i, 0))
    return pl.pallas_call(
        functools.partial(_merge_kernel, d=d, n_exp=n_exp, tm=tm), grid=(t // tm,),
        in_specs=[row(x), row(po), row(fo), row(co), row(gt)] + [_const_spec(a.shape) for a in consts],
        out_specs=[rspec(d), rspec(d), rspec(LANES), rspec(LANES), _const_spec((1, LANES))],
        out_shape=[sd((t, d), F32), sd((t, d), F32), sd((t, LANES), jnp.int32), sd((t, LANES), F32),
                   sd((1, LANES), F32)],
        scratch_shapes=[pltpu.VMEM((1, LANES), F32)],
        compiler_params=_cparams(1), name="merge_router")(x, po, fo, co, gt, *consts)


def _wprep_kernel(w_ref, g_ref, u_ref, t_ref):
    fc = g_ref.shape[1]
    for c in range(t_ref.shape[0]):
        cols = slice(c * LANES, (c + 1) * LANES)
        t_ref[c] = w_ref[0, cols, :].T
        g_ref[0, :, cols] = t_ref[c, pl.ds(0, fc, stride=2), :].astype(BF16)
        u_ref[0, :, cols] = t_ref[c, pl.ds(1, fc, stride=2), :].astype(BF16)


def _wprep(w_gate_up, fc=256):
    e, d, f2 = w_gate_up.shape
    f = f2 // 2
    fc = min(fc, f)
    out = pl.BlockSpec((1, fc, d), lambda i, c: (i, c, 0))
    return pl.pallas_call(
        _wprep_kernel, grid=(e, f // fc),
        in_specs=[pl.BlockSpec((1, d, 2 * fc), lambda i, c: (i, 0, c))],
        out_specs=[out, out],
        out_shape=[jax.ShapeDtypeStruct((e, f, d), BF16)] * 2,
        scratch_shapes=[pltpu.VMEM((d // LANES, 2 * fc, LANES), F32)],
        compiler_params=_cparams(2), name="expert_weight_layout")(w_gate_up)


def _zero_pad_rows(ps_ref, pl_ref, xs_ref, z_ref, sem, n_exp, n_bits):
    z_ref[...] = jnp.zeros_like(z_ref)
    sub = 8

    def each(fn):
        for e in range(n_exp):
            start = ps_ref[e]
            n = pl_ref[e]
            head = jnp.minimum((-start) & (sub - 1), n)
            for r in range(sub - 1):
                @pl.when(r < head)
                def _():
                    fn(pltpu.make_async_copy(z_ref.at[pl.ds(0, 1), :], xs_ref.at[pl.ds(start + r, 1), :], sem))
            start8 = start + head
            n8 = (n - head) // sub
            for b in range(n_bits - 3):
                rows = sub << b
                off = pl.multiple_of(start8 + sub * (n8 & ((1 << b) - 1)), sub)

                @pl.when(((n8 >> b) & 1) == 1)
                def _():
                    fn(pltpu.make_async_copy(z_ref.at[pl.ds(0, rows), :], xs_ref.at[pl.ds(off, rows), :], sem))

    each(lambda c: c.start())
    each(lambda c: c.wait())


def _dispatch_kernel(pos_ref, x_ref, *rest, tm, first, n_exp, n_bits, n_chunks):
    if first and n_chunks:
        ps_ref, pl_ref, w_ref, xs_ref, g_ref, u_ref, sem, z_ref, t_ref = rest
    elif first:
        ps_ref, pl_ref, xs_ref, sem, z_ref = rest
    else:
        _, xs_ref, sem = rest
    if first:
        @pl.when(pl.program_id(0) == 0)
        def _():
            _zero_pad_rows(ps_ref, pl_ref, xs_ref, z_ref, sem, n_exp, n_bits)

    def issue(t, c):
        for j in range(TOP_K):
            pltpu.make_async_copy(x_ref.at[pl.ds(t, 1), :],
                                  xs_ref.at[pl.ds(pos_ref[t * TOP_K + j], 1), :], sem).start()
        return c

    lax.fori_loop(0, tm, issue, 0, unroll=DMA_UNROLL)
    if first and n_chunks:
        @pl.when(pl.program_id(0) < n_chunks)
        def _():
            _wprep_kernel(w_ref, g_ref, u_ref, t_ref)
    for j in range(TOP_K):
        pltpu.make_async_copy(x_ref, xs_ref.at[pl.ds(0, tm), :], sem).wait()


def _dispatch(pos_flat, xn, n_rows, pad=None, xs=None, w_gate_up=None, tm=512, tile_rows=512):
    t, d = xn.shape
    tm = min(tm, t)
    steps = t // tm
    first = xs is None
    n_bits = (tile_rows - 1).bit_length()
    smem = lambda n: pl.BlockSpec((n,), lambda i: (0,), memory_space=pltpu.SMEM)
    in_specs = [pl.BlockSpec((tm * TOP_K,), lambda i: (i,), memory_space=pltpu.SMEM),
                pl.BlockSpec((tm, d), lambda i: (i, 0))]
    scratch = [pltpu.SemaphoreType.DMA]
    out_specs = [pl.BlockSpec(memory_space=pl.ANY)]
    out_shape = [jax.ShapeDtypeStruct((n_rows, d), F32)]
    n_chunks = 0
    if first:
        n_exp = pad[0].shape[0]
        in_specs += [smem(n_exp), smem(n_exp)]
        args = [pos_flat, xn, pad[0], pad[1]]
        scratch.append(pltpu.VMEM((1 << (n_bits - 1), d), F32))
        aliases = {}
        if w_gate_up is not None:
            e, _, f2 = w_gate_up.shape
            f = f2 // 2
            per_exp = steps // e
            assert per_exp >= 1 and f % per_exp == 0
            fc = f // per_exp
            n_chunks = e * per_exp
            chunk = lambda i: jnp.minimum(i, n_chunks - 1)
            in_specs.append(pl.BlockSpec((1, d, 2 * fc), lambda i: (chunk(i) // per_exp, 0, chunk(i) % per_exp)))
            wout = pl.BlockSpec((1, fc, d), lambda i: (chunk(i) // per_exp, chunk(i) % per_exp, 0))
            out_specs += [wout, wout]
            out_shape += [jax.ShapeDtypeStruct((e, f, d), BF16)] * 2
            scratch.append(pltpu.VMEM((d // LANES, 2 * fc, LANES), F32))
            args.append(w_gate_up)
    else:
        n_exp = 0
        in_specs.append(pl.BlockSpec(memory_space=pl.ANY))
        args = [pos_flat, xn, xs]
        aliases = {2: 0}
    out = pl.pallas_call(
        functools.partial(_dispatch_kernel, tm=tm, first=first, n_exp=n_exp, n_bits=n_bits, n_chunks=n_chunks),
        grid=(steps,), in_specs=in_specs, out_specs=out_specs, out_shape=out_shape,
        scratch_shapes=scratch, input_output_aliases=aliases,
        compiler_params=_cparams(1), name="dispatch")(*args)
    return out if n_chunks else out[0]


def _ffn_kernel(te_ref, nu_ref, x_ref, wg_ref, wu_ref, wdf_ref, bg_ref, bu_ref, bd_ref, y_ref, wd_ref):
    i = pl.program_id(0)

    @pl.when(i < nu_ref[0])
    def _():
        @pl.when((i == 0) | (te_ref[i] != te_ref[jnp.maximum(i - 1, 0)]))
        def _():
            wd_ref[0] = wdf_ref[0].astype(BF16)

        x = x_ref[...].astype(BF16)
        g = _dot_nt(x, wg_ref[0]) + bg_ref[0]
        u = _dot_nt(x, wu_ref[0]) + bu_ref[0]
        gate = jnp.minimum(g, SWIGLU_LIMIT)
        up = jnp.clip(u, -SWIGLU_LIMIT, SWIGLU_LIMIT)
        act = (up + 1.0) * gate * jax.nn.sigmoid(SWIGLU_ALPHA * gate)
        y_ref[...] = _dot(act.astype(BF16), wd_ref[0]) + bd_ref[0]


def _ffn(tile_exp, n_used, xs, wg_t, wu_t, wd, bg, bu, bd, tm):
    p, d = xs.shape
    e, f, _ = wg_t.shape
    nt = p // tm
    rowi = lambda i, te, nu: (jnp.minimum(i, nu[0] - 1), 0)
    wi = lambda i, te, nu: (te[i], 0, 0)
    grid_spec = pltpu.PrefetchScalarGridSpec(
        num_scalar_prefetch=2, grid=(nt,),
        in_specs=[pl.BlockSpec((tm, d), rowi),
                  pl.BlockSpec((1, f, d), wi), pl.BlockSpec((1, f, d), wi), pl.BlockSpec((1, f, d), wi),
                  pl.BlockSpec((1, 1, f), wi), pl.BlockSpec((1, 1, f), wi), pl.BlockSpec((1, 1, d), wi)],
        out_specs=pl.BlockSpec((tm, d), rowi),
        scratch_shapes=[pltpu.VMEM((1, f, d), BF16)])
    return pl.pallas_call(
        _ffn_kernel, grid_spec=grid_spec, out_shape=jax.ShapeDtypeStruct((p, d), F32),
        compiler_params=_cparams(1), name="expert_ffn")(tile_exp, n_used, xs, wg_t, wu_t, wd, bg, bu, bd)


def _combine_kernel(pos_ref, posn_ref, w_ref, x1_ref, gf_ref, y_ref, o_ref, buf_ref, sems, *, tm):
    i = pl.program_id(0)
    n = pl.num_programs(0)

    def gather(idx_ref, s):
        def issue(t, c):
            for j in range(TOP_K):
                pltpu.make_async_copy(y_ref.at[pl.ds(idx_ref[t * TOP_K + j], 1), :],
                                      buf_ref.at[s, j, pl.ds(t, 1), :], sems.at[s]).start()
            return c

        lax.fori_loop(0, tm, issue, 0, unroll=DMA_UNROLL)

    def step(slot):
        @pl.when(i == 0)
        def _():
            gather(pos_ref, slot)

        @pl.when(i + 1 < n)
        def _():
            gather(posn_ref, 1 - slot)

        for j in range(TOP_K):
            pltpu.make_async_copy(y_ref.at[pl.ds(0, tm), :], buf_ref.at[slot, j], sems.at[slot]).wait()
        w = w_ref[...]
        acc = x1_ref[...]
        for j in range(TOP_K):
            acc = acc + w[:, j:j + 1] * buf_ref[slot, j]
        o_ref[...] = _rms(acc, gf_ref[...])

    for parity in range(2):
        pl.when(i % 2 == parity)(functools.partial(step, parity))


def _combine(pos_flat, rw, x1, g_final, y, tm=256):
    t, d = x1.shape
    tm = min(tm, t)
    n = t // tm
    return pl.pallas_call(
        functools.partial(_combine_kernel, tm=tm), grid=(n,),
        in_specs=[pl.BlockSpec((tm * TOP_K,), lambda i: (i,), memory_space=pltpu.SMEM),
                  pl.BlockSpec((tm * TOP_K,), lambda i: (jnp.minimum(i + 1, n - 1),), memory_space=pltpu.SMEM),
                  pl.BlockSpec((tm, LANES), lambda i: (i, 0)),
                  pl.BlockSpec((tm, d), lambda i: (i, 0)),
                  _const_spec((1, d)),
                  pl.BlockSpec(memory_space=pl.ANY)],
        out_specs=pl.BlockSpec((tm, d), lambda i: (i, 0)),
        out_shape=jax.ShapeDtypeStruct((t, d), F32),
        scratch_shapes=[pltpu.VMEM((2, TOP_K, tm, d), F32), pltpu.SemaphoreType.DMA((2,))],
        compiler_params=_cparams(1), name="combine")(pos_flat, pos_flat, rw, x1, g_final, y)


def kernel(x_prompt, x_sample, mem_prompt, cache_fox_k, cache_fox_v, cache_fox_logf, state_pool, cache_mem_k, cache_mem_v, g_mix, w_in, b_f, w_pool, s_pool, w_br_pool, w_br_fox, w_br_cross, b_gates, w_out, g_mem, w_mem_kv, g_ffn, w_router, b_router, w_gate_up, b_gate_up, w_down, b_down, g_final):
    depth = w_in.shape[0]
    assert depth == 1, "single-layer model"
    bp, lp, d = x_prompt.shape
    bs, ls, _ = x_sample.shape
    past = cache_fox_k.shape[2]
    n_mem = mem_prompt.shape[1]
    n_exp = w_router.shape[2]
    d_ff = w_down.shape[2]
    pool_w = state_pool.shape[3]
    fox_w = FOX_H * HEAD_DIM
    cross_w = CROSS_H * HEAD_DIM
    scale = HEAD_DIM ** -0.5

    w = w_in[0]
    o_q = pool_w
    o_k, o_v, o_f = o_q + fox_w, o_q + 2 * fox_w, o_q + 3 * fox_w
    o_cq = o_f + FOX_H
    o_g = o_cq + cross_w
    w_cat = jnp.concatenate(
        [w[:, :o_q], w[:, o_q:o_k] * (scale * LOG2E), w[:, o_k:o_v], w[:, o_v:o_f], w[:, o_cq:o_g] * scale, w[:, o_g:],
         jnp.pad(w[:, o_f:o_cq], ((0, 0), (0, LANES - FOX_H)))], axis=1).astype(BF16)
    widths = [pool_w, fox_w, fox_w, fox_w, cross_w, N_BRANCH * d, LANES]
    offs = tuple(int(v) for v in np.concatenate([[0], np.cumsum(widths)]))
    b_f_p = jnp.pad(b_f[0], (0, LANES - FOX_H)).reshape(1, LANES)
    b_g = b_gates[0].reshape(1, -1)
    g_mix2 = g_mix[0].reshape(1, d)
    gw = pool_w // len(POOL_WINDOWS)
    wpool_bd = jnp.zeros((pool_w, pool_w), F32)
    for g in range(len(POOL_WINDOWS)):
        wpool_bd = wpool_bd.at[g * gw:(g + 1) * gw, g * gw:(g + 1) * gw].set(w_pool[0, g])
    tm = 512
    pq, pk, oq, ok = _bias_placement()
    consts = (_tri(tm, strict=False), pq, pk, oq, ok, wpool_bd.astype(BF16), s_pool[0].reshape(1, pool_w),
              _pool_lane_windows(pool_w), _head_selector(fox_w))
    hmask = jnp.asarray(_head_lane_masks()[:, :, None])

    xp = x_prompt.reshape(bp * lp, d)
    (u_p, po_p, q_p, qx_p, kb_p, kx_p, vb_p, k_p, v_p, lf_p, qc_p, gt_p, stats_p) = _inproj(
        xp, g_mix2, w_cat, b_f_p, b_g, offs, prompt=True, seq_len=lp, consts=consts, tm=tm)
    mk, mv = _memkv(mem_prompt.reshape(bp * n_mem, d), g_mem[0].reshape(1, d), w_mem_kv[0].astype(BF16))
    mk3, mv3 = mk.reshape(bp, n_mem, cross_w), mv.reshape(bp, n_mem, cross_w)
    k_first = _first_needed_block(stats_p, bp, fox_w // LANES)
    fo_p = _fox_prompt(k_first, hmask, q_p, qx_p, kb_p, kx_p, vb_p, bp)
    co_p = _cross(qc_p, mk3, mv3, bp)

    xs_ = x_sample.reshape(bs * ls, d)
    (u_s, q_s, k_s, v_s, lf_s, qc_s, gt_s) = _inproj(xs_, g_mix2, w_cat, b_f_p, b_g, offs, prompt=False, tm=tm)
    fo_s, po_s = _sample_mix(q_s, k_s, v_s, lf_s, u_s,
                             cache_fox_k[0].reshape(bs, past, fox_w), cache_fox_v[0].reshape(bs, past, fox_w),
                             cache_fox_logf[0], state_pool[0], consts)
    co_s = _cross(qc_s, cache_mem_k[0].reshape(bs, n_mem, cross_w), cache_mem_v[0].reshape(bs, n_mem, cross_w), bs)

    wbp, wbf, wbc = w_br_pool[0].astype(BF16), w_br_fox[0].astype(BF16), w_br_cross[0].astype(BF16)
    wo = w_out[0].astype(BF16)
    g_ffn2 = g_ffn[0].reshape(1, d)
    wr = jnp.pad(w_router[0], ((0, 0), (0, LANES - n_exp)))
    wrh = wr.astype(BF16)
    wrl = (wr - wrh.astype(F32)).astype(BF16)
    br = jnp.pad(b_router[0], (0, LANES - n_exp)).reshape(1, LANES)
    tri_s = _tri(tm, strict=True)
    margs = (wbp, wbf, wbc, wo, g_ffn2, wrh, wrl, br, tri_s)
    x1_p, xn_p, ri_p, rw_p, cnt_p = _merge(xp, po_p, fo_p, co_p, gt_p, *margs, jnp.zeros((1, LANES), F32), n_exp, tm)
    x1_s, xn_s, ri_s, rw_s, cnt = _merge(xs_, po_s, fo_s, co_s, gt_s, *margs, cnt_p, n_exp, tm)

    tmf = 512
    t_all = bp * lp + bs * ls
    counts = cnt[0, :n_exp].astype(jnp.int32)
    tiles_e = (counts + tmf - 1) // tmf
    tile_end = jnp.cumsum(tiles_e)
    row_off = (tile_end - tiles_e) * tmf
    nt_max = (t_all * TOP_K + n_exp * (tmf - 1)) // tmf + 1
    n_used = tile_end[-1:]
    tile_ids = jnp.minimum(jnp.arange(nt_max, dtype=jnp.int32), n_used[0] - 1)
    tile_exp = jnp.minimum(jnp.sum(tile_ids[:, None] >= tile_end[None, :], axis=1), n_exp - 1).astype(jnp.int32)

    def positions(ri):
        e = ri[:, :TOP_K]
        r = ri[:, TOP_K:2 * TOP_K]
        off = jnp.sum(jnp.where(e[:, :, None] == jnp.arange(n_exp)[None, None, :], row_off[None, None, :], 0), axis=2)
        return (off + r).reshape(-1).astype(jnp.int32)

    pos_p, pos_s = positions(ri_p), positions(ri_s)

    wd = w_down[0]
    bgu = b_gate_up[0]
    bg_e, bu_e = bgu[:, 0::2].reshape(n_exp, 1, d_ff), bgu[:, 1::2].reshape(n_exp, 1, d_ff)
    bd_e = b_down[0].reshape(n_exp, 1, d)
    pad = ((row_off + counts).astype(jnp.int32), (tiles_e * tmf - counts).astype(jnp.int32))
    if (bp * lp) // min(512, bp * lp) >= n_exp:
        xs_sorted, wg_t, wu_t = _dispatch(pos_p, xn_p, nt_max * tmf, pad=pad, w_gate_up=w_gate_up[0], tile_rows=tmf)
    else:
        wg_t, wu_t = _wprep(w_gate_up[0])
        xs_sorted = _dispatch(pos_p, xn_p, nt_max * tmf, pad=pad, tile_rows=tmf)
    xs_sorted = _dispatch(pos_s, xn_s, nt_max * tmf, xs=xs_sorted, tile_rows=tmf)
    y = _ffn(tile_exp, n_used.astype(jnp.int32), xs_sorted, wg_t, wu_t, wd, bg_e, bu_e, bd_e, tmf)
    g_fin = g_final.reshape(1, d)
    y_p = _combine(pos_p, rw_p, x1_p, g_fin, y)
    y_s = _combine(pos_s, rw_s, x1_s, g_fin, y)

    kv5 = lambda a, b, l: a.reshape(1, b, l, FOX_H, HEAD_DIM)
    return (y_p.reshape(bp, lp, d), y_s.reshape(bs, ls, d),
            kv5(k_p, bp, lp), kv5(v_p, bp, lp), lf_p.reshape(1, bp, lp, FOX_H),
            u_p.reshape(bp, lp, pool_w)[:, lp - POOL_HIST:, :][None],
            mk.reshape(1, bp, n_mem, CROSS_H, HEAD_DIM), mv.reshape(1, bp, n_mem, CROSS_H, HEAD_DIM),
            kv5(k_s, bs, ls), kv5(v_s, bs, ls), lf_s.reshape(1, bs, ls, FOX_H),
            u_s.reshape(bs, ls, pool_w)[:, ls - POOL_HIST:, :][None])
```

```python
import functools

import jax
import jax.numpy as jnp
import numpy as np
from jax import lax
from jax.experimental import pallas as pl
from jax.experimental.pallas import tpu as pltpu

F32 = jnp.float32
BF16 = jnp.bfloat16

HEAD_DIM = 64
FOX_H = 8
CROSS_H = 4
POOL_WINDOWS = (2, 4, 8, 16)
POOL_HIST = 15
N_BRANCH = 3
TOP_K = 4
SWIGLU_LIMIT = 7.0
SWIGLU_ALPHA = 1.702
RMS_EPS = 1e-5
NEG = -1e30
LOG2E = 1.4426950408889634
ZERO_EXP2 = 150.0
V_ROWS = HEAD_DIM + 16

LANES = 128
DMA_UNROLL = 4
HIST_ROWS = 16
VMEM_LIMIT = 56 * 1024 * 1024


def _cparams(n_axes=1, vmem=VMEM_LIMIT):
    return pltpu.CompilerParams(dimension_semantics=("arbitrary",) * n_axes, vmem_limit_bytes=vmem)


def _const_spec(shape):
    nd = len(shape)
    return pl.BlockSpec(shape, lambda *_: (0,) * nd)


def _split3(x):
    hi = x.astype(BF16)
    r = x - hi.astype(F32)
    mid = r.astype(BF16)
    lo = (r - mid.astype(F32)).astype(BF16)
    return hi, mid, lo


def _rms(x, g):
    ms = jnp.mean(x * x, axis=-1, keepdims=True)
    return x * lax.rsqrt(ms + RMS_EPS) * g


def _log_sigmoid(z):
    return jnp.minimum(z, 0.0) - jnp.log1p(jnp.exp(-jnp.abs(z)))


def _dot(a, b):
    return jnp.dot(a, b, preferred_element_type=F32)


def _dot_nt(a, b):
    return lax.dot_general(a, b, (((1,), (1,)), ((), ())), preferred_element_type=F32)


def _tri(n, strict):
    r = np.arange(n)
    m = (r[None, :] < r[:, None]) if strict else (r[None, :] <= r[:, None])
    return jnp.asarray(m.astype(np.float32), dtype=BF16)


def _bias_placement():
    pq = np.zeros((3 * LANES, LANES), np.float32)
    pk = np.zeros((3 * LANES, LANES), np.float32)
    oq = np.zeros((1, LANES), np.float32)
    ok = np.zeros((1, LANES), np.float32)
    for p in range(3):
        for h in range(FOX_H):
            pq[p * LANES + h, 8 * p + h] = 1.0
            pk[p * LANES + h, 24 + 8 * p + h] = -1.0
            oq[0, 24 + 8 * p + h] = 1.0
            ok[0, 8 * p + h] = 1.0
    return (jnp.asarray(pq, dtype=BF16), jnp.asarray(pk, dtype=BF16), jnp.asarray(oq), jnp.asarray(ok))


def _head_lane_masks():
    m = np.zeros((FOX_H, LANES), np.float32)
    for p in range(3):
        for h in range(FOX_H):
            m[h, 8 * p + h] = 1.0
            m[h, 24 + 8 * p + h] = 1.0
    return m


def _head_selector(fox_w):
    m = np.zeros((fox_w, LANES), np.float32)
    for h in range(fox_w // HEAD_DIM):
        m[h * HEAD_DIM:(h + 1) * HEAD_DIM, h] = 1.0
    return jnp.asarray(m, dtype=BF16)


def _pool_lane_windows(pool_w):
    gw = pool_w // len(POOL_WINDOWS)
    return jnp.asarray(np.repeat(np.asarray(POOL_WINDOWS, np.float32), gw)[None, :])


SEC_U, SEC_Q, SEC_K, SEC_V, SEC_CQ, SEC_G, SEC_F = range(7)


def _pool_mix(ext, u, row0, winl, wpool, spool):
    n = u.shape[0]
    s1 = ext + pltpu.roll(ext, 1, 0)
    s2 = s1 + pltpu.roll(s1, 2, 0)
    s3 = s2 + pltpu.roll(s2, 4, 0)
    s4 = s3 + pltpu.roll(s3, 8, 0)
    win = jnp.where(winl == 2.0, s1, jnp.where(winl == 4.0, s2, jnp.where(winl == 8.0, s3, s4)))
    win = win[HIST_ROWS:, :]
    pos = (row0 + lax.broadcasted_iota(jnp.int32, (n, 1), 0)).astype(F32)
    cnt = jnp.minimum(pos + 1.0, winl)
    pooled = win / cnt - u
    return (_dot(pooled.astype(BF16), wpool) * spool).astype(BF16)


def _inproj_kernel(x_ref, g_ref, w_ref, bf_ref, bg_ref, *rest, offs, prompt, tiles_per_batch, tm, n_gate_chunks):
    if prompt:
        (tri_ref, pq_ref, pk_ref, oq_ref, ok_ref, wpool_ref, spool_ref, winl_ref, hsel_ref,
         u_ref, po_ref, q_ref, qx_ref, kb_ref, kx_ref, vb_ref, k_ref, v_ref, lf_ref, qc_ref, gt_ref, st_ref,
         carry_ref, hist_ref) = rest
    else:
        (u_ref, q_ref, k_ref, v_ref, lf_ref, qc_ref, gt_ref) = rest

    h = _rms(x_ref[...], g_ref[...]).astype(BF16)

    def sec(s):
        return _dot(h, w_ref[:, offs[s]:offs[s + 1]])

    u = sec(SEC_U)
    u_ref[...] = u
    q = sec(SEC_Q)
    k = sec(SEC_K)
    v = sec(SEC_V)
    k_ref[...] = k
    v_ref[...] = v
    qc_ref[...] = sec(SEC_CQ).astype(BF16)
    gw = (offs[SEC_G + 1] - offs[SEC_G]) // n_gate_chunks
    for c in range(n_gate_chunks):
        a = offs[SEC_G] + c * gw
        z = _dot(h, w_ref[:, a:a + gw]) + bg_ref[:, c * gw:(c + 1) * gw]
        gt_ref[:, c * gw:(c + 1) * gw] = jax.nn.sigmoid(z).astype(BF16)
    zf = sec(SEC_F) + bf_ref[...]
    lane = lax.broadcasted_iota(jnp.int32, zf.shape, 1)
    logf = jnp.where(lane < FOX_H, _log_sigmoid(zf), 0.0)
    lf_ref[...] = logf[:, :FOX_H]

    if not prompt:
        q_ref[...] = q.astype(BF16)
        return

    tib = pl.program_id(0) % tiles_per_batch

    @pl.when(tib == 0)
    def _():
        carry_ref[...] = jnp.zeros_like(carry_ref)
        hist_ref[...] = jnp.zeros_like(hist_ref)

    n_pair = q.shape[1] // LANES
    for hp in range(n_pair):
        sl = slice(hp * LANES, (hp + 1) * LANES)
        q_ref[hp] = q[:, sl].T.astype(BF16)
        kb_ref[hp] = k[:, sl].astype(BF16)
        vt = v[:, sl].T.astype(BF16)
        for a in range(2):
            vb_ref[2 * hp + a, 0, 0:HEAD_DIM, :] = vt[a * HEAD_DIM:(a + 1) * HEAD_DIM, :]
            vb_ref[2 * hp + a, 0, HEAD_DIM:V_ROWS, :] = jnp.ones((V_ROWS - HEAD_DIM, tm), BF16)

    tri = tri_ref[...]
    hi, mid, lo = _split3(logf)
    cum = _dot(tri, hi) + _dot(tri, mid) + _dot(tri, lo) + carry_ref[...]
    carry_ref[...] = cum[tm - 1:tm, :]
    cum2 = cum * LOG2E
    cp = jnp.concatenate(_split3(cum2), axis=1)
    qx_ref[...] = (_dot(cp, pq_ref[...]) + oq_ref[...]).T.astype(BF16)
    kx_ref[...] = (_dot(cp, pk_ref[...]) + ok_ref[...]).astype(BF16)

    def max_sq_norm(a):
        ab = a.astype(BF16).astype(F32)
        sq = ab * ab
        sh = sq.astype(BF16)
        sl = (sq - sh.astype(F32)).astype(BF16)
        return jnp.max(_dot(sh, hsel_ref[...]) + _dot(sl, hsel_ref[...]), axis=0, keepdims=True)

    srow = lax.broadcasted_iota(jnp.int32, (8, LANES), 0)
    st_ref[0] = jnp.where(srow == 0, max_sq_norm(q),
                          jnp.where(srow == 1, max_sq_norm(k),
                                    jnp.where(srow == 2, cum2[0:1, :],
                                              jnp.where(srow == 3, cum2[tm - 1:tm, :], 0.0))))

    ext = jnp.concatenate([hist_ref[...], u], axis=0)
    hist_ref[...] = u[tm - HIST_ROWS:, :]
    po_ref[...] = _pool_mix(ext, u, tib * tm, winl_ref[...], wpool_ref[...], spool_ref[...])


def _inproj(x, g_mix, w_cat, b_f, b_g, offs, *, prompt, seq_len=None, consts=None, tm=512):
    t, d = x.shape
    tm = min(tm, t)
    n_tiles = t // tm
    pool_w = offs[SEC_U + 1] - offs[SEC_U]
    fox_w = offs[SEC_Q + 1] - offs[SEC_Q]
    cross_w = offs[SEC_CQ + 1] - offs[SEC_CQ]
    gate_w = offs[SEC_G + 1] - offs[SEC_G]
    n_pair = fox_w // LANES
    row = lambda w: pl.BlockSpec((tm, w), lambda i: (i, 0))
    pair = pl.BlockSpec((n_pair, tm, LANES), lambda i: (0, i, 0))
    in_specs = [row(d), _const_spec((1, d)),
                pl.BlockSpec(w_cat.shape, lambda i: (0, 0), pipeline_mode=pl.Buffered(1)),
                _const_spec((1, LANES)), _const_spec((1, gate_w))]
    args = [x, g_mix, w_cat, b_f, b_g]
    sd = jax.ShapeDtypeStruct
    if prompt:
        in_specs += [_const_spec(a.shape) for a in consts]
        args += list(consts)
        pair_t = pl.BlockSpec((n_pair, LANES, tm), lambda i: (0, 0, i))
        head_t = pl.BlockSpec((FOX_H, 1, V_ROWS, tm), lambda i: (0, i, 0, 0))
        out_shape = [sd((t, pool_w), F32), sd((t, pool_w), BF16),
                     sd((n_pair, LANES, t), BF16), sd((LANES, t), BF16),
                     sd((n_pair, t, LANES), BF16), sd((t, LANES), BF16),
                     sd((FOX_H, n_tiles, V_ROWS, tm), BF16),
                     sd((t, fox_w), F32), sd((t, fox_w), F32), sd((t, FOX_H), F32),
                     sd((t, cross_w), BF16), sd((t, gate_w), BF16), sd((n_tiles, 8, LANES), F32)]
        out_specs = [row(pool_w), row(pool_w), pair_t, pl.BlockSpec((LANES, tm), lambda i: (0, i)),
                     pair, row(LANES), head_t,
                     row(fox_w), row(fox_w), row(FOX_H), row(cross_w), row(gate_w),
                     pl.BlockSpec((1, 8, LANES), lambda i: (i, 0, 0))]
        scratch = [pltpu.VMEM((1, LANES), F32), pltpu.VMEM((HIST_ROWS, pool_w), F32)]
        tiles_per_batch = seq_len // tm
    else:
        out_shape = [sd((t, pool_w), F32), sd((t, fox_w), BF16), sd((t, fox_w), F32), sd((t, fox_w), F32),
                     sd((t, FOX_H), F32), sd((t, cross_w), BF16), sd((t, gate_w), BF16)]
        out_specs = [row(pool_w), row(fox_w), row(fox_w), row(fox_w), row(FOX_H), row(cross_w), row(gate_w)]
        scratch = []
        tiles_per_batch = 1
    kern = functools.partial(_inproj_kernel, offs=offs, prompt=prompt, tiles_per_batch=tiles_per_batch,
                             tm=tm, n_gate_chunks=N_BRANCH)
    return pl.pallas_call(
        kern, grid=(n_tiles,), in_specs=in_specs, out_specs=out_specs, out_shape=out_shape,
        scratch_shapes=scratch, compiler_params=_cparams(1),
        name="inproj_prompt" if prompt else "inproj_sample")(*args)


def _memkv_kernel(m_ref, g_ref, w_ref, k_ref, v_ref, *, cw):
    h = _rms(m_ref[...], g_ref[...]).astype(BF16)
    kv = _dot(h, w_ref[...])
    k_ref[...] = kv[:, :cw]
    v_ref[...] = kv[:, cw:]


def _memkv(mem, g_mem, w_kv, tm=256):
    t, d = mem.shape
    cw = w_kv.shape[1] // 2
    tm = min(tm, t)
    row = lambda w: pl.BlockSpec((tm, w), lambda i: (i, 0))
    return pl.pallas_call(
        functools.partial(_memkv_kernel, cw=cw), grid=(t // tm,),
        in_specs=[row(d), _const_spec((1, d)), _const_spec(w_kv.shape)],
        out_specs=[row(cw), row(cw)],
        out_shape=[jax.ShapeDtypeStruct((t, cw), F32)] * 2,
        compiler_params=_cparams(1), name="memkv")(mem, g_mem, w_kv)


def _cross_kernel(q_ref, k_ref, v_ref, o_ref):
    q = q_ref[...]
    kk = k_ref[0].astype(BF16)
    vv = v_ref[0].astype(BF16)
    lane = lax.broadcasted_iota(jnp.int32, (1, q.shape[1]), 1) // HEAD_DIM
    out = jnp.zeros(q.shape, F32)
    for h in range(q.shape[1] // HEAD_DIM):
        hm = lane == h
        s = _dot_nt(jnp.where(hm, q, jnp.zeros_like(q)), kk)
        m = jnp.max(s, axis=1, keepdims=True)
        p = jnp.exp(s - m)
        l = jnp.sum(p, axis=1, keepdims=True)
        o = _dot(p.astype(BF16), vv) / l
        out = jnp.where(hm, o, out)
    o_ref[...] = out.astype(BF16)


def _cross(qc, mk, mv, n_batch, tl=512):
    t, cw = qc.shape
    l = t // n_batch
    tl = min(tl, l)
    nl = l // tl
    m = mk.shape[1]
    return pl.pallas_call(
        _cross_kernel, grid=(n_batch, nl),
        in_specs=[pl.BlockSpec((tl, cw), lambda b, i: (b * nl + i, 0)),
                  pl.BlockSpec((1, m, cw), lambda b, i: (b, 0, 0)),
                  pl.BlockSpec((1, m, cw), lambda b, i: (b, 0, 0))],
        out_specs=pl.BlockSpec((tl, cw), lambda b, i: (b * nl + i, 0)),
        out_shape=jax.ShapeDtypeStruct((t, cw), BF16),
        compiler_params=_cparams(2), name="cross_attn")(qc, mk, mv)


def _fox_kernel(kf_ref, hm_ref, q_ref, qx_ref, k_ref, kx_ref, v_ref, o_ref,
                m0_ref, m1_ref, acc0_ref, acc1_ref, s0_ref, *, tq, n_pair, nq):
    qi = pl.program_id(2)
    qt = q_ref[0]
    qxt = qx_ref[...]
    row = lax.broadcasted_iota(jnp.int32, (LANES, 1), 0)
    zero = jnp.zeros_like(qt)
    qs = []
    for a in range(2):
        half = (row < HEAD_DIM) if a == 0 else (row >= HEAD_DIM)
        qs.append(jnp.concatenate([jnp.where(half, qt, zero), jnp.where(hm_ref[a] > 0.5, qxt, zero)], axis=0))
    ms = (m0_ref, m1_ref)
    accs = (acc0_ref, acc1_ref)
    for a in range(2):
        ms[a][...] = jnp.full_like(ms[a], NEG)
        accs[a][...] = jnp.zeros_like(accs[a])

    def scores(a, ki):
        ks = pl.multiple_of(ki * tq, tq)
        kcat = jnp.concatenate([k_ref[0, pl.ds(ks, tq), :], kx_ref[pl.ds(ks, tq), :]], axis=1)
        return _dot(kcat, qs[a])

    def softmax_pv(a, s, ki, masked):
        if masked:
            r = lax.broadcasted_iota(jnp.int32, s.shape, 0)
            c = lax.broadcasted_iota(jnp.int32, s.shape, 1)
            s = jnp.where(r <= c, s, NEG)
        m_old = ms[a][...]
        m_new = jnp.maximum(m_old, jnp.max(s, axis=0, keepdims=True))
        alpha = jnp.exp2(m_old - m_new)
        p = jnp.exp2(s - m_new).astype(BF16)
        accs[a][...] = alpha * accs[a][...] + _dot(v_ref[a, ki], p)
        ms[a][...] = m_new

    k_first = kf_ref[(pl.program_id(0) * n_pair + pl.program_id(1)) * nq + qi]
    s0_ref[...] = scores(0, qi)
    s1 = scores(1, qi)
    softmax_pv(0, s0_ref[...], qi, True)
    s0_ref[...] = scores(0, jnp.maximum(qi - 1, 0))
    softmax_pv(1, s1, qi, True)

    def body(j, carry):
        ki = qi - 1 - j

        @pl.when(ki >= k_first)
        def _():
            s1 = scores(1, ki)
            softmax_pv(0, s0_ref[...], ki, False)
            s0_ref[...] = scores(0, jnp.maximum(ki - 1, 0))
            softmax_pv(1, s1, ki, False)
        return carry

    lax.fori_loop(0, qi, body, 0)
    outs = [accs[a][0:HEAD_DIM, :] / accs[a][HEAD_DIM:HEAD_DIM + 1, :] for a in range(2)]
    o_ref[...] = jnp.concatenate(outs, axis=0).T.astype(BF16)


def _first_needed_block(stats, n_batch, n_pair):
    nt = stats.shape[0]
    nq = nt // n_batch
    st = stats[:, :4, :FOX_H].reshape(n_batch, nq, 4, FOX_H)
    slack = 1.0 + 2.0 ** -8
    qn, kn = jnp.sqrt(st[:, :, 0]) * slack, jnp.sqrt(st[:, :, 1]) * slack
    c_first, c_last = st[:, :, 2], st[:, :, 3]
    upper = qn[:, :, None, :] * kn[:, None, :, :] + (c_first[:, :, None, :] - c_last[:, None, :, :])
    lower = -(qn * kn)[:, :, None, :]
    qi = jnp.arange(nq)[None, :, None, None]
    ki = jnp.arange(nq)[None, None, :, None]
    needed = (ki == qi) | ((ki < qi) & ~(upper - lower + 2.0 < -ZERO_EXP2))
    first = jnp.min(jnp.where(needed, ki, nq), axis=2).astype(jnp.int32)
    first = jnp.min(first.reshape(n_batch, nq, n_pair, 2), axis=3)
    return jnp.transpose(first, (0, 2, 1)).reshape(-1)


def _fox_prompt(k_first, hmask, q, qx, kb, kx, vb, n_batch):
    n_pair, _, t = q.shape
    _, n_tiles, _, tq = vb.shape
    l = t // n_batch
    nq = l // tq
    grid_spec = pltpu.PrefetchScalarGridSpec(
        num_scalar_prefetch=1, grid=(n_batch, n_pair, nq),
        in_specs=[pl.BlockSpec((2, LANES, 1), lambda b, h, i, kf: (h, 0, 0)),
                  pl.BlockSpec((1, LANES, tq), lambda b, h, i, kf: (h, 0, b * nq + i)),
                  pl.BlockSpec((LANES, tq), lambda b, h, i, kf: (0, b * nq + i)),
                  pl.BlockSpec((1, l, LANES), lambda b, h, i, kf: (h, b, 0)),
                  pl.BlockSpec((l, LANES), lambda b, h, i, kf: (b, 0)),
                  pl.BlockSpec((2, nq, V_ROWS, tq), lambda b, h, i, kf: (h, b, 0, 0))],
        out_specs=pl.BlockSpec((tq, LANES), lambda b, h, i, kf: (b * nq + i, h)),
        scratch_shapes=[pltpu.VMEM((1, tq), F32), pltpu.VMEM((1, tq), F32),
                        pltpu.VMEM((V_ROWS, tq), F32), pltpu.VMEM((V_ROWS, tq), F32), pltpu.VMEM((tq, tq), F32)])
    return pl.pallas_call(
        functools.partial(_fox_kernel, tq=tq, n_pair=n_pair, nq=nq), grid_spec=grid_spec,
        out_shape=jax.ShapeDtypeStruct((t, n_pair * LANES), BF16),
        compiler_params=_cparams(3), name="fox_prompt")(k_first, hmask, q, qx, kb, kx, vb)


def _sample_kernel(q_ref, kn_ref, vn_ref, lfn_ref, u_ref, ck_ref, cv_ref, clf_ref, st_ref,
                   tri_ref, pq_ref, pk_ref, oq_ref, ok_ref, qmask_ref, xmask_ref, wpool_ref, spool_ref, winl_ref,
                   fo_ref, po_ref,
                   ext_ref, lf_ref, cum_ref, kcat_ref, vall_ref, *, past, ls, chunk):
    fw = q_ref.shape[1]
    lk_pad = kcat_ref.shape[0]
    n_heads = fw // HEAD_DIM

    u = u_ref[...]
    ext_ref[...] = jnp.zeros_like(ext_ref)
    ext_ref[pl.ds(HIST_ROWS - POOL_HIST, POOL_HIST), :] = st_ref[0]
    ext_ref[pl.ds(HIST_ROWS, ls), :] = u
    po_ref[...] = _pool_mix(ext_ref[...], u, past, winl_ref[...], wpool_ref[...], spool_ref[...])

    lf_ref[...] = jnp.zeros_like(lf_ref)
    lf_ref[pl.ds(0, past), pl.ds(0, FOX_H)] = clf_ref[0]
    lf_ref[pl.ds(past, ls), pl.ds(0, FOX_H)] = lfn_ref[...]
    tri = tri_ref[...]
    carry = jnp.zeros((1, LANES), F32)
    for c in range(lk_pad // chunk):
        rows = pl.ds(c * chunk, chunk)
        hi, mid, lo = _split3(lf_ref[rows, :])
        cum = _dot(tri, hi) + _dot(tri, mid) + _dot(tri, lo) + carry
        carry = cum[chunk - 1:chunk, :]
        cum_ref[rows, :] = cum
        cp = jnp.concatenate(_split3(cum * LOG2E), axis=1)
        kcat_ref[rows, pl.ds(fw, LANES)] = (_dot(cp, pk_ref[...]) + ok_ref[...]).astype(BF16)

    kcat_ref[pl.ds(0, past), pl.ds(0, fw)] = ck_ref[0].astype(BF16)
    kcat_ref[pl.ds(past, ls), pl.ds(0, fw)] = kn_ref[...].astype(BF16)
    vall_ref[pl.ds(0, past), :] = cv_ref[0].astype(BF16)
    vall_ref[pl.ds(past, ls), :] = vn_ref[...].astype(BF16)
    npad = lk_pad - past - ls
    kcat_ref[pl.ds(past + ls, npad), pl.ds(0, fw)] = jnp.zeros((npad, fw), BF16)
    vall_ref[pl.ds(past + ls, npad), :] = jnp.zeros((npad, fw), BF16)

    cq = cum_ref[pl.ds(past, ls), :]
    cpq = jnp.concatenate(_split3(cq * LOG2E), axis=1)
    qx = (_dot(cpq, pq_ref[...]) + oq_ref[...]).astype(BF16)
    q = q_ref[...]
    qbd = jnp.concatenate([jnp.concatenate([q] * n_heads, axis=0) * qmask_ref[...],
                           jnp.concatenate([qx] * n_heads, axis=0) * xmask_ref[...]], axis=1)
    s = _dot_nt(qbd, kcat_ref[...])
    r = lax.broadcasted_iota(jnp.int32, s.shape, 0) % ls
    c = lax.broadcasted_iota(jnp.int32, s.shape, 1)
    s = jnp.where(c <= r + past, s, NEG)
    m = jnp.max(s, axis=1, keepdims=True)
    p = jnp.exp2(s - m)
    l = jnp.sum(p, axis=1, keepdims=True)
    o = _dot(p.astype(BF16), vall_ref[...]) / l
    lane_h = lax.broadcasted_iota(jnp.int32, (1, fw), 1) // HEAD_DIM
    out = jnp.zeros((ls, fw), F32)
    for h in range(n_heads):
        out = jnp.where(lane_h == h, o[h * ls:(h + 1) * ls, :], out)
    fo_ref[...] = out.astype(BF16)


def _sample_mix(q, kn, vn, lfn, u, ck, cv, clf, st, consts, chunk=256):
    t, fw = q.shape
    nb, past, _ = ck.shape
    ls = t // nb
    pw = u.shape[1]
    lk_pad = -(-(past + ls) // chunk) * chunk
    if lk_pad == past + ls:
        lk_pad += chunk
    n_heads = fw // HEAD_DIM
    hm = _head_lane_masks()
    qmask = np.zeros((n_heads * ls, fw), np.float32)
    xmask = np.zeros((n_heads * ls, LANES), np.float32)
    for h in range(n_heads):
        qmask[h * ls:(h + 1) * ls, h * HEAD_DIM:(h + 1) * HEAD_DIM] = 1.0
        xmask[h * ls:(h + 1) * ls, :] = hm[h]
    tri, pq, pk, oq, ok, wpool, spool, winl = consts[:8]
    tri_c = tri[:chunk, :chunk]
    cargs = [tri_c, pq, pk, oq, ok, jnp.asarray(qmask, dtype=BF16), jnp.asarray(xmask, dtype=BF16), wpool, spool, winl]
    row = lambda w: pl.BlockSpec((ls, w), lambda b: (b, 0))
    bat = lambda a: pl.BlockSpec((1,) + a.shape[1:], lambda b: (b, 0, 0))
    return pl.pallas_call(
        functools.partial(_sample_kernel, past=past, ls=ls, chunk=chunk), grid=(nb,),
        in_specs=[row(fw), row(fw), row(fw), row(FOX_H), row(pw), bat(ck), bat(cv), bat(clf), bat(st)]
                 + [_const_spec(a.shape) for a in cargs],
        out_specs=[row(fw), row(pw)],
        out_shape=[jax.ShapeDtypeStruct((t, fw), BF16), jax.ShapeDtypeStruct((t, pw), BF16)],
        scratch_shapes=[pltpu.VMEM((HIST_ROWS + ls, pw), F32), pltpu.VMEM((lk_pad, LANES), F32),
                        pltpu.VMEM((lk_pad, LANES), F32), pltpu.VMEM((lk_pad, fw + LANES), BF16),
                        pltpu.VMEM((lk_pad, fw), BF16)],
        compiler_params=_cparams(1), name="sample_mix")(q, kn, vn, lfn, u, ck, cv, clf, st, *cargs)


def _merge_kernel(x_ref, po_ref, fo_ref, co_ref, gt_ref, wbp_ref, wbf_ref, wbc_ref, wo_ref, gf_ref,
                  wrh_ref, wrl_ref, br_ref, tri_ref, cin_ref,
                  x1_ref, xn_ref, ri_ref, rw_ref, cnt_ref, carry_ref, *, d, n_exp, tm):
    @pl.when(pl.program_id(0) == 0)
    def _():
        carry_ref[...] = cin_ref[...]

    def gate(j):
        return gt_ref[:, j * d:(j + 1) * d].astype(F32)

    merged = (gate(0) * _dot(po_ref[...], wbp_ref[...])
              + gate(1) * _dot(fo_ref[...], wbf_ref[...])
              + gate(2) * _dot(co_ref[...], wbc_ref[...]))
    x1 = x_ref[...] + _dot(merged.astype(BF16), wo_ref[...])
    x1_ref[...] = x1
    xn = _rms(x1, gf_ref[...])
    xn_ref[...] = xn
    xh = xn.astype(BF16)
    xl = (xn - xh.astype(F32)).astype(BF16)
    logits = _dot(xh, wrh_ref[...]) + _dot(xl, wrh_ref[...]) + _dot(xh, wrl_ref[...]) + br_ref[...]

    lane = lax.broadcasted_iota(jnp.int32, logits.shape, 1)
    lane_f = lane.astype(F32)
    work = jnp.where(lane < n_exp, logits, NEG)
    vals, idxs, sels = [], [], []
    for _ in range(TOP_K):
        mx = jnp.max(work, axis=1, keepdims=True)
        idx = jnp.min(jnp.where(work == mx, lane_f, float(LANES)), axis=1, keepdims=True)
        sel = lane_f == idx
        vals.append(mx)
        idxs.append(idx)
        sels.append(sel)
        work = jnp.where(sel, NEG, work)
    es = [jnp.exp(v - vals[0]) for v in vals]
    den = es[0] + es[1] + es[2] + es[3]

    onehot = jnp.zeros(logits.shape, F32)
    for sel in sels:
        onehot = jnp.where(sel, 1.0, onehot)
    cum = _dot(tri_ref[...], onehot.astype(BF16)) + carry_ref[...]
    carry_ref[...] = cum[tm - 1:tm, :] + onehot[tm - 1:tm, :]
    cnt_ref[...] = carry_ref[...]

    ri = jnp.zeros(logits.shape, jnp.int32)
    rw = jnp.zeros(logits.shape, F32)
    for j in range(TOP_K):
        rank = jnp.sum(jnp.where(sels[j], cum, 0.0), axis=1, keepdims=True)
        ri = jnp.where(lane == j, idxs[j].astype(jnp.int32), ri)
        ri = jnp.where(lane == TOP_K + j, rank.astype(jnp.int32), ri)
        rw = jnp.where(lane == j, es[j] / den, rw)
    ri_ref[...] = ri
    rw_ref[...] = rw


def _merge(x, po, fo, co, gt, wbp, wbf, wbc, wo, g_ffn, wrh, wrl, br, tri_s, cnt_in, n_exp, tm=512):
    t, d = x.shape
    tm = min(tm, t)
    row = lambda a: pl.BlockSpec((tm, a.shape[1]), lambda i: (i, 0))
    consts = [wbp, wbf, wbc, wo, g_ffn, wrh, wrl, br, tri_s, cnt_in]
    sd = jax.ShapeDtypeStruct
    rspec = lambda w: pl.BlockSpec((tm, w), lambda i: (i, 0))
    return pl.pallas_call(
        functools.partial(_merge_kernel, d=d, n_exp=n_exp, tm=tm), grid=(t // tm,),
        in_specs=[row(x), row(po), row(fo), row(co), row(gt)] + [_const_spec(a.shape) for a in consts],
        out_specs=[rspec(d), rspec(d), rspec(LANES), rspec(LANES), _const_spec((1, LANES))],
        out_shape=[sd((t, d), F32), sd((t, d), F32), sd((t, LANES), jnp.int32), sd((t, LANES), F32),
                   sd((1, LANES), F32)],
        scratch_shapes=[pltpu.VMEM((1, LANES), F32)],
        compiler_params=_cparams(1), name="merge_router")(x, po, fo, co, gt, *consts)


def _wprep_kernel(w_ref, g_ref, u_ref, t_ref):
    fc = g_ref.shape[1]
    for c in range(t_ref.shape[0]):
        cols = slice(c * LANES, (c + 1) * LANES)
        t_ref[c] = w_ref[0, cols, :].T
        g_ref[0, :, cols] = t_ref[c, pl.ds(0, fc, stride=2), :].astype(BF16)
        u_ref[0, :, cols] = t_ref[c, pl.ds(1, fc, stride=2), :].astype(BF16)


def _wprep(w_gate_up, fc=256):
    e, d, f2 = w_gate_up.shape
    f = f2 // 2
    fc = min(fc, f)
    out = pl.BlockSpec((1, fc, d), lambda i, c: (i, c, 0))
    return pl.pallas_call(
        _wprep_kernel, grid=(e, f // fc),
        in_specs=[pl.BlockSpec((1, d, 2 * fc), lambda i, c: (i, 0, c))],
        out_specs=[out, out],
        out_shape=[jax.ShapeDtypeStruct((e, f, d), BF16)] * 2,
        scratch_shapes=[pltpu.VMEM((d // LANES, 2 * fc, LANES), F32)],
        compiler_params=_cparams(2), name="expert_weight_layout")(w_gate_up)


def _zero_pad_rows(ps_ref, pl_ref, xs_ref, z_ref, sem, n_exp, n_bits):
    z_ref[...] = jnp.zeros_like(z_ref)
    sub = 8

    def each(fn):
        for e in range(n_exp):
            start = ps_ref[e]
            n = pl_ref[e]
            head = jnp.minimum((-start) & (sub - 1), n)
            for r in range(sub - 1):
                @pl.when(r < head)
                def _():
                    fn(pltpu.make_async_copy(z_ref.at[pl.ds(0, 1), :], xs_ref.at[pl.ds(start + r, 1), :], sem))
            start8 = start + head
            n8 = (n - head) // sub
            for b in range(n_bits - 3):
                rows = sub << b
                off = pl.multiple_of(start8 + sub * (n8 & ((1 << b) - 1)), sub)

                @pl.when(((n8 >> b) & 1) == 1)
                def _():
                    fn(pltpu.make_async_copy(z_ref.at[pl.ds(0, rows), :], xs_ref.at[pl.ds(off, rows), :], sem))

    each(lambda c: c.start())
    each(lambda c: c.wait())


def _dispatch_kernel(pos_ref, x_ref, *rest, tm, first, n_exp, n_bits, n_chunks):
    if first and n_chunks:
        ps_ref, pl_ref, w_ref, xs_ref, g_ref, u_ref, sem, z_ref, t_ref = rest
    elif first:
        ps_ref, pl_ref, xs_ref, sem, z_ref = rest
    else:
        _, xs_ref, sem = rest
    if first:
        @pl.when(pl.program_id(0) == 0)
        def _():
            _zero_pad_rows(ps_ref, pl_ref, xs_ref, z_ref, sem, n_exp, n_bits)

    def issue(t, c):
        for j in range(TOP_K):
            pltpu.make_async_copy(x_ref.at[pl.ds(t, 1), :],
                                  xs_ref.at[pl.ds(pos_ref[t * TOP_K + j], 1), :], sem).start()
        return c

    lax.fori_loop(0, tm, issue, 0, unroll=DMA_UNROLL)
    if first and n_chunks:
        @pl.when(pl.program_id(0) < n_chunks)
        def _():
            _wprep_kernel(w_ref, g_ref, u_ref, t_ref)
    for j in range(TOP_K):
        pltpu.make_async_copy(x_ref, xs_ref.at[pl.ds(0, tm), :], sem).wait()


def _dispatch(pos_flat, xn, n_rows, pad=None, xs=None, w_gate_up=None, tm=512, tile_rows=512):
    t, d = xn.shape
    tm = min(tm, t)
    steps = t // tm
    first = xs is None
    n_bits = (tile_rows - 1).bit_length()
    smem = lambda n: pl.BlockSpec((n,), lambda i: (0,), memory_space=pltpu.SMEM)
    in_specs = [pl.BlockSpec((tm * TOP_K,), lambda i: (i,), memory_space=pltpu.SMEM),
                pl.BlockSpec((tm, d), lambda i: (i, 0))]
    scratch = [pltpu.SemaphoreType.DMA]
    out_specs = [pl.BlockSpec(memory_space=pl.ANY)]
    out_shape = [jax.ShapeDtypeStruct((n_rows, d), F32)]
    n_chunks = 0
    if first:
        n_exp = pad[0].shape[0]
        in_specs += [smem(n_exp), smem(n_exp)]
        args = [pos_flat, xn, pad[0], pad[1]]
        scratch.append(pltpu.VMEM((1 << (n_bits - 1), d), F32))
        aliases = {}
        if w_gate_up is not None:
            e, _, f2 = w_gate_up.shape
            f = f2 // 2
            per_exp = steps // e
            assert per_exp >= 1 and f % per_exp == 0
            fc = f // per_exp
            n_chunks = e * per_exp
            chunk = lambda i: jnp.minimum(i, n_chunks - 1)
            in_specs.append(pl.BlockSpec((1, d, 2 * fc), lambda i: (chunk(i) // per_exp, 0, chunk(i) % per_exp)))
            wout = pl.BlockSpec((1, fc, d), lambda i: (chunk(i) // per_exp, chunk(i) % per_exp, 0))
            out_specs += [wout, wout]
            out_shape += [jax.ShapeDtypeStruct((e, f, d), BF16)] * 2
            scratch.append(pltpu.VMEM((d // LANES, 2 * fc, LANES), F32))
            args.append(w_gate_up)
    else:
        n_exp = 0
        in_specs.append(pl.BlockSpec(memory_space=pl.ANY))
        args = [pos_flat, xn, xs]
        aliases = {2: 0}
    out = pl.pallas_call(
        functools.partial(_dispatch_kernel, tm=tm, first=first, n_exp=n_exp, n_bits=n_bits, n_chunks=n_chunks),
        grid=(steps,), in_specs=in_specs, out_specs=out_specs, out_shape=out_shape,
        scratch_shapes=scratch, input_output_aliases=aliases,
        compiler_params=_cparams(1), name="dispatch")(*args)
    return out if n_chunks else out[0]


def _ffn_kernel(te_ref, nu_ref, x_ref, wg_ref, wu_ref, wdf_ref, bg_ref, bu_ref, bd_ref, y_ref, wd_ref):
    i = pl.program_id(0)

    @pl.when(i < nu_ref[0])
    def _():
        @pl.when((i == 0) | (te_ref[i] != te_ref[jnp.maximum(i - 1, 0)]))
        def _():
            wd_ref[0] = wdf_ref[0].astype(BF16)

        x = x_ref[...].astype(BF16)
        g = _dot_nt(x, wg_ref[0]) + bg_ref[0]
        u = _dot_nt(x, wu_ref[0]) + bu_ref[0]
        gate = jnp.minimum(g, SWIGLU_LIMIT)
        up = jnp.clip(u, -SWIGLU_LIMIT, SWIGLU_LIMIT)
        act = (up + 1.0) * gate * jax.nn.sigmoid(SWIGLU_ALPHA * gate)
        y_ref[...] = _dot(act.astype(BF16), wd_ref[0]) + bd_ref[0]


def _ffn(tile_exp, n_used, xs, wg_t, wu_t, wd, bg, bu, bd, tm):
    p, d = xs.shape
    e, f, _ = wg_t.shape
    nt = p // tm
    rowi = lambda i, te, nu: (jnp.minimum(i, nu[0] - 1), 0)
    wi = lambda i, te, nu: (te[i], 0, 0)
    grid_spec = pltpu.PrefetchScalarGridSpec(
        num_scalar_prefetch=2, grid=(nt,),
        in_specs=[pl.BlockSpec((tm, d), rowi),
                  pl.BlockSpec((1, f, d), wi), pl.BlockSpec((1, f, d), wi), pl.BlockSpec((1, f, d), wi),
                  pl.BlockSpec((1, 1, f), wi), pl.BlockSpec((1, 1, f), wi), pl.BlockSpec((1, 1, d), wi)],
        out_specs=pl.BlockSpec((tm, d), rowi),
        scratch_shapes=[pltpu.VMEM((1, f, d), BF16)])
    return pl.pallas_call(
        _ffn_kernel, grid_spec=grid_spec, out_shape=jax.ShapeDtypeStruct((p, d), F32),
        compiler_params=_cparams(1), name="expert_ffn")(tile_exp, n_used, xs, wg_t, wu_t, wd, bg, bu, bd)


def _combine_kernel(pos_ref, posn_ref, w_ref, x1_ref, gf_ref, y_ref, o_ref, buf_ref, sems, *, tm):
    i = pl.program_id(0)
    n = pl.num_programs(0)

    def gather(idx_ref, s):
        def issue(t, c):
            for j in range(TOP_K):
                pltpu.make_async_copy(y_ref.at[pl.ds(idx_ref[t * TOP_K + j], 1), :],
                                      buf_ref.at[s, j, pl.ds(t, 1), :], sems.at[s]).start()
            return c

        lax.fori_loop(0, tm, issue, 0, unroll=DMA_UNROLL)

    def step(slot):
        @pl.when(i == 0)
        def _():
            gather(pos_ref, slot)

        @pl.when(i + 1 < n)
        def _():
            gather(posn_ref, 1 - slot)

        for j in range(TOP_K):
            pltpu.make_async_copy(y_ref.at[pl.ds(0, tm), :], buf_ref.at[slot, j], sems.at[slot]).wait()
        w = w_ref[...]
        acc = x1_ref[...]
        for j in range(TOP_K):
            acc = acc + w[:, j:j + 1] * buf_ref[slot, j]
        o_ref[...] = _rms(acc, gf_ref[...])

    for parity in range(2):
        pl.when(i % 2 == parity)(functools.partial(step, parity))


def _combine(pos_flat, rw, x1, g_final, y, tm=512):
    t, d = x1.shape
    tm = min(tm, t)
    n = t // tm
    return pl.pallas_call(
        functools.partial(_combine_kernel, tm=tm), grid=(n,),
        in_specs=[pl.BlockSpec((tm * TOP_K,), lambda i: (i,), memory_space=pltpu.SMEM),
                  pl.BlockSpec((tm * TOP_K,), lambda i: (jnp.minimum(i + 1, n - 1),), memory_space=pltpu.SMEM),
                  pl.BlockSpec((tm, LANES), lambda i: (i, 0)),
                  pl.BlockSpec((tm, d), lambda i: (i, 0)),
                  _const_spec((1, d)),
                  pl.BlockSpec(memory_space=pl.ANY)],
        out_specs=pl.BlockSpec((tm, d), lambda i: (i, 0)),
        out_shape=jax.ShapeDtypeStruct((t, d), F32),
        scratch_shapes=[pltpu.VMEM((2, TOP_K, tm, d), F32), pltpu.SemaphoreType.DMA((2,))],
        compiler_params=_cparams(1), name="combine")(pos_flat, pos_flat, rw, x1, g_final, y)


def kernel(x_prompt, x_sample, mem_prompt, cache_fox_k, cache_fox_v, cache_fox_logf, state_pool, cache_mem_k, cache_mem_v, g_mix, w_in, b_f, w_pool, s_pool, w_br_pool, w_br_fox, w_br_cross, b_gates, w_out, g_mem, w_mem_kv, g_ffn, w_router, b_router, w_gate_up, b_gate_up, w_down, b_down, g_final):
    depth = w_in.shape[0]
    assert depth == 1, "single-layer model"
    bp, lp, d = x_prompt.shape
    bs, ls, _ = x_sample.shape
    past = cache_fox_k.shape[2]
    n_mem = mem_prompt.shape[1]
    n_exp = w_router.shape[2]
    d_ff = w_down.shape[2]
    pool_w = state_pool.shape[3]
    fox_w = FOX_H * HEAD_DIM
    cross_w = CROSS_H * HEAD_DIM
    scale = HEAD_DIM ** -0.5

    w = w_in[0]
    o_q = pool_w
    o_k, o_v, o_f = o_q + fox_w, o_q + 2 * fox_w, o_q + 3 * fox_w
    o_cq = o_f + FOX_H
    o_g = o_cq + cross_w
    w_cat = jnp.concatenate(
        [w[:, :o_q], w[:, o_q:o_k] * (scale * LOG2E), w[:, o_k:o_v], w[:, o_v:o_f], w[:, o_cq:o_g] * scale, w[:, o_g:],
         jnp.pad(w[:, o_f:o_cq], ((0, 0), (0, LANES - FOX_H)))], axis=1).astype(BF16)
    widths = [pool_w, fox_w, fox_w, fox_w, cross_w, N_BRANCH * d, LANES]
    offs = tuple(int(v) for v in np.concatenate([[0], np.cumsum(widths)]))
    b_f_p = jnp.pad(b_f[0], (0, LANES - FOX_H)).reshape(1, LANES)
    b_g = b_gates[0].reshape(1, -1)
    g_mix2 = g_mix[0].reshape(1, d)
    gw = pool_w // len(POOL_WINDOWS)
    wpool_bd = jnp.zeros((pool_w, pool_w), F32)
    for g in range(len(POOL_WINDOWS)):
        wpool_bd = wpool_bd.at[g * gw:(g + 1) * gw, g * gw:(g + 1) * gw].set(w_pool[0, g])
    tm = 512
    pq, pk, oq, ok = _bias_placement()
    consts = (_tri(tm, strict=False), pq, pk, oq, ok, wpool_bd.astype(BF16), s_pool[0].reshape(1, pool_w),
              _pool_lane_windows(pool_w), _head_selector(fox_w))
    hmask = jnp.asarray(_head_lane_masks()[:, :, None])

    xp = x_prompt.reshape(bp * lp, d)
    (u_p, po_p, q_p, qx_p, kb_p, kx_p, vb_p, k_p, v_p, lf_p, qc_p, gt_p, stats_p) = _inproj(
        xp, g_mix2, w_cat, b_f_p, b_g, offs, prompt=True, seq_len=lp, consts=consts, tm=tm)
    mk, mv = _memkv(mem_prompt.reshape(bp * n_mem, d), g_mem[0].reshape(1, d), w_mem_kv[0].astype(BF16))
    mk3, mv3 = mk.reshape(bp, n_mem, cross_w), mv.reshape(bp, n_mem, cross_w)
    k_first = _first_needed_block(stats_p, bp, fox_w // LANES)
    fo_p = _fox_prompt(k_first, hmask, q_p, qx_p, kb_p, kx_p, vb_p, bp)
    co_p = _cross(qc_p, mk3, mv3, bp)

    xs_ = x_sample.reshape(bs * ls, d)
    (u_s, q_s, k_s, v_s, lf_s, qc_s, gt_s) = _inproj(xs_, g_mix2, w_cat, b_f_p, b_g, offs, prompt=False, tm=tm)
    fo_s, po_s = _sample_mix(q_s, k_s, v_s, lf_s, u_s,
                             cache_fox_k[0].reshape(bs, past, fox_w), cache_fox_v[0].reshape(bs, past, fox_w),
                             cache_fox_logf[0], state_pool[0], consts)
    co_s = _cross(qc_s, cache_mem_k[0].reshape(bs, n_mem, cross_w), cache_mem_v[0].reshape(bs, n_mem, cross_w), bs)

    wbp, wbf, wbc = w_br_pool[0].astype(BF16), w_br_fox[0].astype(BF16), w_br_cross[0].astype(BF16)
    wo = w_out[0].astype(BF16)
    g_ffn2 = g_ffn[0].reshape(1, d)
    wr = jnp.pad(w_router[0], ((0, 0), (0, LANES - n_exp)))
    wrh = wr.astype(BF16)
    wrl = (wr - wrh.astype(F32)).astype(BF16)
    br = jnp.pad(b_router[0], (0, LANES - n_exp)).reshape(1, LANES)
    tri_s = _tri(tm, strict=True)
    margs = (wbp, wbf, wbc, wo, g_ffn2, wrh, wrl, br, tri_s)
    x1_p, xn_p, ri_p, rw_p, cnt_p = _merge(xp, po_p, fo_p, co_p, gt_p, *margs, jnp.zeros((1, LANES), F32), n_exp, tm)
    x1_s, xn_s, ri_s, rw_s, cnt = _merge(xs_, po_s, fo_s, co_s, gt_s, *margs, cnt_p, n_exp, tm)

    tmf = 512
    t_all = bp * lp + bs * ls
    counts = cnt[0, :n_exp].astype(jnp.int32)
    tiles_e = (counts + tmf - 1) // tmf
    tile_end = jnp.cumsum(tiles_e)
    row_off = (tile_end - tiles_e) * tmf
    nt_max = (t_all * TOP_K + n_exp * (tmf - 1)) // tmf + 1
    n_used = tile_end[-1:]
    tile_ids = jnp.minimum(jnp.arange(nt_max, dtype=jnp.int32), n_used[0] - 1)
    tile_exp = jnp.minimum(jnp.sum(tile_ids[:, None] >= tile_end[None, :], axis=1), n_exp - 1).astype(jnp.int32)

    def positions(ri):
        e = ri[:, :TOP_K]
        r = ri[:, TOP_K:2 * TOP_K]
        off = jnp.sum(jnp.where(e[:, :, None] == jnp.arange(n_exp)[None, None, :], row_off[None, None, :], 0), axis=2)
        return (off + r).reshape(-1).astype(jnp.int32)

    pos_p, pos_s = positions(ri_p), positions(ri_s)

    wd = w_down[0]
    bgu = b_gate_up[0]
    bg_e, bu_e = bgu[:, 0::2].reshape(n_exp, 1, d_ff), bgu[:, 1::2].reshape(n_exp, 1, d_ff)
    bd_e = b_down[0].reshape(n_exp, 1, d)
    pad = ((row_off + counts).astype(jnp.int32), (tiles_e * tmf - counts).astype(jnp.int32))
    if (bp * lp) // min(512, bp * lp) >= n_exp:
        xs_sorted, wg_t, wu_t = _dispatch(pos_p, xn_p, nt_max * tmf, pad=pad, w_gate_up=w_gate_up[0], tile_rows=tmf)
    else:
        wg_t, wu_t = _wprep(w_gate_up[0])
        xs_sorted = _dispatch(pos_p, xn_p, nt_max * tmf, pad=pad, tile_rows=tmf)
    xs_sorted = _dispatch(pos_s, xn_s, nt_max * tmf, xs=xs_sorted, tile_rows=tmf)
    y = _ffn(tile_exp, n_used.astype(jnp.int32), xs_sorted, wg_t, wu_t, wd, bg_e, bu_e, bd_e, tmf)
    g_fin = g_final.reshape(1, d)
    y_p = _combine(pos_p, rw_p, x1_p, g_fin, y)
    y_s = _combine(pos_s, rw_s, x1_s, g_fin, y)

    kv5 = lambda a, b, l: a.reshape(1, b, l, FOX_H, HEAD_DIM)
    return (y_p.reshape(bp, lp, d), y_s.reshape(bs, ls, d),
            kv5(k_p, bp, lp), kv5(v_p, bp, lp), lf_p.reshape(1, bp, lp, FOX_H),
            u_p.reshape(bp, lp, pool_w)[:, lp - POOL_HIST:, :][None],
            mk.reshape(1, bp, n_mem, CROSS_H, HEAD_DIM), mv.reshape(1, bp, n_mem, CROSS_H, HEAD_DIM),
            kv5(k_s, bs, ls), kv5(v_s, bs, ls), lf_s.reshape(1, bs, ls, FOX_H),
            u_s.reshape(bs, ls, pool_w)[:, ls - POOL_HIST:, :][None])
```

```python
import functools

import jax
import jax.numpy as jnp
import numpy as np
from jax import lax
from jax.experimental import pallas as pl
from jax.experimental.pallas import tpu as pltpu

F32 = jnp.float32
BF16 = jnp.bfloat16

HEAD_DIM = 64
FOX_H = 8
CROSS_H = 4
POOL_WINDOWS = (2, 4, 8, 16)
POOL_HIST = 15
N_BRANCH = 3
TOP_K = 4
SWIGLU_LIMIT = 7.0
SWIGLU_ALPHA = 1.702
RMS_EPS = 1e-5
NEG = -1e30
LOG2E = 1.4426950408889634
ZERO_EXP2 = 150.0
V_ROWS = HEAD_DIM + 16

LANES = 128
DISPATCH_ROWS = 1024
DMA_UNROLL = 4
HIST_ROWS = 16
VMEM_LIMIT = 56 * 1024 * 1024


def _cparams(n_axes=1, vmem=VMEM_LIMIT):
    return pltpu.CompilerParams(dimension_semantics=("arbitrary",) * n_axes, vmem_limit_bytes=vmem)


def _const_spec(shape):
    nd = len(shape)
    return pl.BlockSpec(shape, lambda *_: (0,) * nd)


def _split3(x):
    hi = x.astype(BF16)
    r = x - hi.astype(F32)
    mid = r.astype(BF16)
    lo = (r - mid.astype(F32)).astype(BF16)
    return hi, mid, lo


def _rms(x, g):
    ms = jnp.mean(x * x, axis=-1, keepdims=True)
    return x * lax.rsqrt(ms + RMS_EPS) * g


def _log_sigmoid(z):
    return jnp.minimum(z, 0.0) - jnp.log1p(jnp.exp(-jnp.abs(z)))


def _dot(a, b):
    return jnp.dot(a, b, preferred_element_type=F32)


def _dot_nt(a, b):
    return lax.dot_general(a, b, (((1,), (1,)), ((), ())), preferred_element_type=F32)


def _tri(n, strict):
    r = np.arange(n)
    m = (r[None, :] < r[:, None]) if strict else (r[None, :] <= r[:, None])
    return jnp.asarray(m.astype(np.float32), dtype=BF16)


def _bias_placement():
    pq = np.zeros((3 * LANES, LANES), np.float32)
    pk = np.zeros((3 * LANES, LANES), np.float32)
    oq = np.zeros((1, LANES), np.float32)
    ok = np.zeros((1, LANES), np.float32)
    for p in range(3):
        for h in range(FOX_H):
            pq[p * LANES + h, 8 * p + h] = 1.0
            pk[p * LANES + h, 24 + 8 * p + h] = -1.0
            oq[0, 24 + 8 * p + h] = 1.0
            ok[0, 8 * p + h] = 1.0
    return (jnp.asarray(pq, dtype=BF16), jnp.asarray(pk, dtype=BF16), jnp.asarray(oq), jnp.asarray(ok))


def _head_lane_masks():
    m = np.zeros((FOX_H, LANES), np.float32)
    for p in range(3):
        for h in range(FOX_H):
            m[h, 8 * p + h] = 1.0
            m[h, 24 + 8 * p + h] = 1.0
    return m


def _head_selector(fox_w):
    m = np.zeros((fox_w, LANES), np.float32)
    for h in range(fox_w // HEAD_DIM):
        m[h * HEAD_DIM:(h + 1) * HEAD_DIM, h] = 1.0
    return jnp.asarray(m, dtype=BF16)


def _pool_lane_windows(pool_w):
    gw = pool_w // len(POOL_WINDOWS)
    return jnp.asarray(np.repeat(np.asarray(POOL_WINDOWS, np.float32), gw)[None, :])


SEC_U, SEC_Q, SEC_K, SEC_V, SEC_CQ, SEC_G, SEC_F = range(7)


def _pool_mix(ext, u, row0, winl, wpool, spool):
    n = u.shape[0]
    s1 = ext + pltpu.roll(ext, 1, 0)
    s2 = s1 + pltpu.roll(s1, 2, 0)
    s3 = s2 + pltpu.roll(s2, 4, 0)
    s4 = s3 + pltpu.roll(s3, 8, 0)
    win = jnp.where(winl == 2.0, s1, jnp.where(winl == 4.0, s2, jnp.where(winl == 8.0, s3, s4)))
    win = win[HIST_ROWS:, :]
    pos = (row0 + lax.broadcasted_iota(jnp.int32, (n, 1), 0)).astype(F32)
    cnt = jnp.minimum(pos + 1.0, winl)
    pooled = win / cnt - u
    return (_dot(pooled.astype(BF16), wpool) * spool).astype(BF16)


def _inproj_kernel(x_ref, g_ref, w_ref, bf_ref, bg_ref, *rest, offs, prompt, tiles_per_batch, tm, n_gate_chunks):
    if prompt:
        (tri_ref, pq_ref, pk_ref, oq_ref, ok_ref, wpool_ref, spool_ref, winl_ref, hsel_ref,
         u_ref, po_ref, q_ref, qx_ref, kb_ref, kx_ref, vb_ref, k_ref, v_ref, lf_ref, qc_ref, gt_ref, st_ref,
         carry_ref, hist_ref) = rest
    else:
        (u_ref, q_ref, k_ref, v_ref, lf_ref, qc_ref, gt_ref) = rest

    h = _rms(x_ref[...], g_ref[...]).astype(BF16)

    def sec(s):
        return _dot(h, w_ref[:, offs[s]:offs[s + 1]])

    u = sec(SEC_U)
    u_ref[...] = u
    q = sec(SEC_Q)
    k = sec(SEC_K)
    v = sec(SEC_V)
    k_ref[...] = k
    v_ref[...] = v
    qc_ref[...] = sec(SEC_CQ).astype(BF16)
    gw = (offs[SEC_G + 1] - offs[SEC_G]) // n_gate_chunks
    for c in range(n_gate_chunks):
        a = offs[SEC_G] + c * gw
        z = _dot(h, w_ref[:, a:a + gw]) + bg_ref[:, c * gw:(c + 1) * gw]
        gt_ref[:, c * gw:(c + 1) * gw] = jax.nn.sigmoid(z).astype(BF16)
    zf = sec(SEC_F) + bf_ref[...]
    lane = lax.broadcasted_iota(jnp.int32, zf.shape, 1)
    logf = jnp.where(lane < FOX_H, _log_sigmoid(zf), 0.0)
    lf_ref[...] = logf[:, :FOX_H]

    if not prompt:
        q_ref[...] = q.astype(BF16)
        return

    tib = pl.program_id(0) % tiles_per_batch

    @pl.when(tib == 0)
    def _():
        carry_ref[...] = jnp.zeros_like(carry_ref)
        hist_ref[...] = jnp.zeros_like(hist_ref)

    n_pair = q.shape[1] // LANES
    for hp in range(n_pair):
        sl = slice(hp * LANES, (hp + 1) * LANES)
        q_ref[hp] = q[:, sl].T.astype(BF16)
        kb_ref[hp] = k[:, sl].astype(BF16)
        vt = v[:, sl].T.astype(BF16)
        for a in range(2):
            vb_ref[2 * hp + a, 0, 0:HEAD_DIM, :] = vt[a * HEAD_DIM:(a + 1) * HEAD_DIM, :]
            vb_ref[2 * hp + a, 0, HEAD_DIM:V_ROWS, :] = jnp.ones((V_ROWS - HEAD_DIM, tm), BF16)

    tri = tri_ref[...]
    hi, mid, lo = _split3(logf)
    cum = _dot(tri, hi) + _dot(tri, mid) + _dot(tri, lo) + carry_ref[...]
    carry_ref[...] = cum[tm - 1:tm, :]
    cum2 = cum * LOG2E
    cp = jnp.concatenate(_split3(cum2), axis=1)
    qx_ref[...] = (_dot(cp, pq_ref[...]) + oq_ref[...]).T.astype(BF16)
    kx_ref[...] = (_dot(cp, pk_ref[...]) + ok_ref[...]).astype(BF16)

    def max_sq_norm(a):
        ab = a.astype(BF16).astype(F32)
        sq = ab * ab
        sh = sq.astype(BF16)
        sl = (sq - sh.astype(F32)).astype(BF16)
        return jnp.max(_dot(sh, hsel_ref[...]) + _dot(sl, hsel_ref[...]), axis=0, keepdims=True)

    srow = lax.broadcasted_iota(jnp.int32, (8, LANES), 0)
    st_ref[0] = jnp.where(srow == 0, max_sq_norm(q),
                          jnp.where(srow == 1, max_sq_norm(k),
                                    jnp.where(srow == 2, cum2[0:1, :],
                                              jnp.where(srow == 3, cum2[tm - 1:tm, :], 0.0))))

    ext = jnp.concatenate([hist_ref[...], u], axis=0)
    hist_ref[...] = u[tm - HIST_ROWS:, :]
    po_ref[...] = _pool_mix(ext, u, tib * tm, winl_ref[...], wpool_ref[...], spool_ref[...])


def _inproj(x, g_mix, w_cat, b_f, b_g, offs, *, prompt, seq_len=None, consts=None, tm=512):
    t, d = x.shape
    tm = min(tm, t)
    n_tiles = t // tm
    pool_w = offs[SEC_U + 1] - offs[SEC_U]
    fox_w = offs[SEC_Q + 1] - offs[SEC_Q]
    cross_w = offs[SEC_CQ + 1] - offs[SEC_CQ]
    gate_w = offs[SEC_G + 1] - offs[SEC_G]
    n_pair = fox_w // LANES
    row = lambda w: pl.BlockSpec((tm, w), lambda i: (i, 0))
    pair = pl.BlockSpec((n_pair, tm, LANES), lambda i: (0, i, 0))
    in_specs = [row(d), _const_spec((1, d)),
                pl.BlockSpec(w_cat.shape, lambda i: (0, 0), pipeline_mode=pl.Buffered(1)),
                _const_spec((1, LANES)), _const_spec((1, gate_w))]
    args = [x, g_mix, w_cat, b_f, b_g]
    sd = jax.ShapeDtypeStruct
    if prompt:
        in_specs += [_const_spec(a.shape) for a in consts]
        args += list(consts)
        pair_t = pl.BlockSpec((n_pair, LANES, tm), lambda i: (0, 0, i))
        head_t = pl.BlockSpec((FOX_H, 1, V_ROWS, tm), lambda i: (0, i, 0, 0))
        out_shape = [sd((t, pool_w), F32), sd((t, pool_w), BF16),
                     sd((n_pair, LANES, t), BF16), sd((LANES, t), BF16),
                     sd((n_pair, t, LANES), BF16), sd((t, LANES), BF16),
                     sd((FOX_H, n_tiles, V_ROWS, tm), BF16),
                     sd((t, fox_w), F32), sd((t, fox_w), F32), sd((t, FOX_H), F32),
                     sd((t, cross_w), BF16), sd((t, gate_w), BF16), sd((n_tiles, 8, LANES), F32)]
        out_specs = [row(pool_w), row(pool_w), pair_t, pl.BlockSpec((LANES, tm), lambda i: (0, i)),
                     pair, row(LANES), head_t,
                     row(fox_w), row(fox_w), row(FOX_H), row(cross_w), row(gate_w),
                     pl.BlockSpec((1, 8, LANES), lambda i: (i, 0, 0))]
        scratch = [pltpu.VMEM((1, LANES), F32), pltpu.VMEM((HIST_ROWS, pool_w), F32)]
        tiles_per_batch = seq_len // tm
    else:
        out_shape = [sd((t, pool_w), F32), sd((t, fox_w), BF16), sd((t, fox_w), F32), sd((t, fox_w), F32),
                     sd((t, FOX_H), F32), sd((t, cross_w), BF16), sd((t, gate_w), BF16)]
        out_specs = [row(pool_w), row(fox_w), row(fox_w), row(fox_w), row(FOX_H), row(cross_w), row(gate_w)]
        scratch = []
        tiles_per_batch = 1
    kern = functools.partial(_inproj_kernel, offs=offs, prompt=prompt, tiles_per_batch=tiles_per_batch,
                             tm=tm, n_gate_chunks=N_BRANCH)
    return pl.pallas_call(
        kern, grid=(n_tiles,), in_specs=in_specs, out_specs=out_specs, out_shape=out_shape,
        scratch_shapes=scratch, compiler_params=_cparams(1),
        name="inproj_prompt" if prompt else "inproj_sample")(*args)


def _memkv_kernel(m_ref, g_ref, w_ref, k_ref, v_ref, *, cw):
    h = _rms(m_ref[...], g_ref[...]).astype(BF16)
    kv = _dot(h, w_ref[...])
    k_ref[...] = kv[:, :cw]
    v_ref[...] = kv[:, cw:]


def _memkv(mem, g_mem, w_kv, tm=256):
    t, d = mem.shape
    cw = w_kv.shape[1] // 2
    tm = min(tm, t)
    row = lambda w: pl.BlockSpec((tm, w), lambda i: (i, 0))
    return pl.pallas_call(
        functools.partial(_memkv_kernel, cw=cw), grid=(t // tm,),
        in_specs=[row(d), _const_spec((1, d)), _const_spec(w_kv.shape)],
        out_specs=[row(cw), row(cw)],
        out_shape=[jax.ShapeDtypeStruct((t, cw), F32)] * 2,
        compiler_params=_cparams(1), name="memkv")(mem, g_mem, w_kv)


def _cross_kernel(q_ref, k_ref, v_ref, o_ref):
    q = q_ref[...]
    kk = k_ref[0].astype(BF16)
    vv = v_ref[0].astype(BF16)
    lane = lax.broadcasted_iota(jnp.int32, (1, q.shape[1]), 1) // HEAD_DIM
    out = jnp.zeros(q.shape, F32)
    for h in range(q.shape[1] // HEAD_DIM):
        hm = lane == h
        s = _dot_nt(jnp.where(hm, q, jnp.zeros_like(q)), kk)
        m = jnp.max(s, axis=1, keepdims=True)
        p = jnp.exp(s - m)
        l = jnp.sum(p, axis=1, keepdims=True)
        o = _dot(p.astype(BF16), vv) / l
        out = jnp.where(hm, o, out)
    o_ref[...] = out.astype(BF16)


def _cross(qc, mk, mv, n_batch, tl=512):
    t, cw = qc.shape
    l = t // n_batch
    tl = min(tl, l)
    nl = l // tl
    m = mk.shape[1]
    return pl.pallas_call(
        _cross_kernel, grid=(n_batch, nl),
        in_specs=[pl.BlockSpec((tl, cw), lambda b, i: (b * nl + i, 0)),
                  pl.BlockSpec((1, m, cw), lambda b, i: (b, 0, 0)),
                  pl.BlockSpec((1, m, cw), lambda b, i: (b, 0, 0))],
        out_specs=pl.BlockSpec((tl, cw), lambda b, i: (b * nl + i, 0)),
        out_shape=jax.ShapeDtypeStruct((t, cw), BF16),
        compiler_params=_cparams(2), name="cross_attn")(qc, mk, mv)


def _fox_kernel(kf_ref, hm_ref, q_ref, qx_ref, k_ref, kx_ref, v_ref, o_ref,
                m0_ref, m1_ref, acc0_ref, acc1_ref, s0_ref, *, tq, n_pair, nq):
    qi = pl.program_id(2)
    qt = q_ref[0]
    qxt = qx_ref[...]
    row = lax.broadcasted_iota(jnp.int32, (LANES, 1), 0)
    zero = jnp.zeros_like(qt)
    qs = []
    for a in range(2):
        half = (row < HEAD_DIM) if a == 0 else (row >= HEAD_DIM)
        qs.append(jnp.concatenate([jnp.where(half, qt, zero), jnp.where(hm_ref[a] > 0.5, qxt, zero)], axis=0))
    ms = (m0_ref, m1_ref)
    accs = (acc0_ref, acc1_ref)
    for a in range(2):
        ms[a][...] = jnp.full_like(ms[a], NEG)
        accs[a][...] = jnp.zeros_like(accs[a])

    def scores(a, ki):
        ks = pl.multiple_of(ki * tq, tq)
        kcat = jnp.concatenate([k_ref[0, pl.ds(ks, tq), :], kx_ref[pl.ds(ks, tq), :]], axis=1)
        return _dot(kcat, qs[a])

    def softmax_pv(a, s, ki, masked):
        if masked:
            r = lax.broadcasted_iota(jnp.int32, s.shape, 0)
            c = lax.broadcasted_iota(jnp.int32, s.shape, 1)
            s = jnp.where(r <= c, s, NEG)
        m_old = ms[a][...]
        m_new = jnp.maximum(m_old, jnp.max(s, axis=0, keepdims=True))
        alpha = jnp.exp2(m_old - m_new)
        p = jnp.exp2(s - m_new).astype(BF16)
        accs[a][...] = alpha * accs[a][...] + _dot(v_ref[a, ki], p)
        ms[a][...] = m_new

    k_first = kf_ref[(pl.program_id(0) * n_pair + pl.program_id(1)) * nq + qi]
    s0_ref[...] = scores(0, qi)
    s1 = scores(1, qi)
    softmax_pv(0, s0_ref[...], qi, True)
    s0_ref[...] = scores(0, jnp.maximum(qi - 1, 0))
    softmax_pv(1, s1, qi, True)

    def body(j, carry):
        ki = qi - 1 - j

        @pl.when(ki >= k_first)
        def _():
            s1 = scores(1, ki)
            softmax_pv(0, s0_ref[...], ki, False)
            s0_ref[...] = scores(0, jnp.maximum(ki - 1, 0))
            softmax_pv(1, s1, ki, False)
        return carry

    lax.fori_loop(0, qi, body, 0)
    outs = [accs[a][0:HEAD_DIM, :] / accs[a][HEAD_DIM:HEAD_DIM + 1, :] for a in range(2)]
    o_ref[...] = jnp.concatenate(outs, axis=0).T.astype(BF16)


def _first_needed_block(stats, n_batch, n_pair):
    nt = stats.shape[0]
    nq = nt // n_batch
    st = stats[:, :4, :FOX_H].reshape(n_batch, nq, 4, FOX_H)
    slack = 1.0 + 2.0 ** -8
    qn, kn = jnp.sqrt(st[:, :, 0]) * slack, jnp.sqrt(st[:, :, 1]) * slack
    c_first, c_last = st[:, :, 2], st[:, :, 3]
    upper = qn[:, :, None, :] * kn[:, None, :, :] + (c_first[:, :, None, :] - c_last[:, None, :, :])
    lower = -(qn * kn)[:, :, None, :]
    qi = jnp.arange(nq)[None, :, None, None]
    ki = jnp.arange(nq)[None, None, :, None]
    needed = (ki == qi) | ((ki < qi) & ~(upper - lower + 2.0 < -ZERO_EXP2))
    first = jnp.min(jnp.where(needed, ki, nq), axis=2).astype(jnp.int32)
    first = jnp.min(first.reshape(n_batch, nq, n_pair, 2), axis=3)
    return jnp.transpose(first, (0, 2, 1)).reshape(-1)


def _fox_prompt(k_first, hmask, q, qx, kb, kx, vb, n_batch):
    n_pair, _, t = q.shape
    _, n_tiles, _, tq = vb.shape
    l = t // n_batch
    nq = l // tq
    grid_spec = pltpu.PrefetchScalarGridSpec(
        num_scalar_prefetch=1, grid=(n_batch, n_pair, nq),
        in_specs=[pl.BlockSpec((2, LANES, 1), lambda b, h, i, kf: (h, 0, 0)),
                  pl.BlockSpec((1, LANES, tq), lambda b, h, i, kf: (h, 0, b * nq + i)),
                  pl.BlockSpec((LANES, tq), lambda b, h, i, kf: (0, b * nq + i)),
                  pl.BlockSpec((1, l, LANES), lambda b, h, i, kf: (h, b, 0)),
                  pl.BlockSpec((l, LANES), lambda b, h, i, kf: (b, 0)),
                  pl.BlockSpec((2, nq, V_ROWS, tq), lambda b, h, i, kf: (h, b, 0, 0))],
        out_specs=pl.BlockSpec((tq, LANES), lambda b, h, i, kf: (b * nq + i, h)),
        scratch_shapes=[pltpu.VMEM((1, tq), F32), pltpu.VMEM((1, tq), F32),
                        pltpu.VMEM((V_ROWS, tq), F32), pltpu.VMEM((V_ROWS, tq), F32), pltpu.VMEM((tq, tq), F32)])
    return pl.pallas_call(
        functools.partial(_fox_kernel, tq=tq, n_pair=n_pair, nq=nq), grid_spec=grid_spec,
        out_shape=jax.ShapeDtypeStruct((t, n_pair * LANES), BF16),
        compiler_params=_cparams(3), name="fox_prompt")(k_first, hmask, q, qx, kb, kx, vb)


def _sample_kernel(q_ref, kn_ref, vn_ref, lfn_ref, u_ref, ck_ref, cv_ref, clf_ref, st_ref,
                   tri_ref, pq_ref, pk_ref, oq_ref, ok_ref, qmask_ref, xmask_ref, wpool_ref, spool_ref, winl_ref,
                   fo_ref, po_ref,
                   ext_ref, lf_ref, cum_ref, kcat_ref, vall_ref, *, past, ls, chunk):
    fw = q_ref.shape[1]
    lk_pad = kcat_ref.shape[0]
    n_heads = fw // HEAD_DIM

    u = u_ref[...]
    ext_ref[...] = jnp.zeros_like(ext_ref)
    ext_ref[pl.ds(HIST_ROWS - POOL_HIST, POOL_HIST), :] = st_ref[0]
    ext_ref[pl.ds(HIST_ROWS, ls), :] = u
    po_ref[...] = _pool_mix(ext_ref[...], u, past, winl_ref[...], wpool_ref[...], spool_ref[...])

    lf_ref[...] = jnp.zeros_like(lf_ref)
    lf_ref[pl.ds(0, past), pl.ds(0, FOX_H)] = clf_ref[0]
    lf_ref[pl.ds(past, ls), pl.ds(0, FOX_H)] = lfn_ref[...]
    tri = tri_ref[...]
    carry = jnp.zeros((1, LANES), F32)
    for c in range(lk_pad // chunk):
        rows = pl.ds(c * chunk, chunk)
        hi, mid, lo = _split3(lf_ref[rows, :])
        cum = _dot(tri, hi) + _dot(tri, mid) + _dot(tri, lo) + carry
        carry = cum[chunk - 1:chunk, :]
        cum_ref[rows, :] = cum
        cp = jnp.concatenate(_split3(cum * LOG2E), axis=1)
        kcat_ref[rows, pl.ds(fw, LANES)] = (_dot(cp, pk_ref[...]) + ok_ref[...]).astype(BF16)

    kcat_ref[pl.ds(0, past), pl.ds(0, fw)] = ck_ref[0].astype(BF16)
    kcat_ref[pl.ds(past, ls), pl.ds(0, fw)] = kn_ref[...].astype(BF16)
    vall_ref[pl.ds(0, past), :] = cv_ref[0].astype(BF16)
    vall_ref[pl.ds(past, ls), :] = vn_ref[...].astype(BF16)
    npad = lk_pad - past - ls
    kcat_ref[pl.ds(past + ls, npad), pl.ds(0, fw)] = jnp.zeros((npad, fw), BF16)
    vall_ref[pl.ds(past + ls, npad), :] = jnp.zeros((npad, fw), BF16)

    cq = cum_ref[pl.ds(past, ls), :]
    cpq = jnp.concatenate(_split3(cq * LOG2E), axis=1)
    qx = (_dot(cpq, pq_ref[...]) + oq_ref[...]).astype(BF16)
    q = q_ref[...]
    qbd = jnp.concatenate([jnp.concatenate([q] * n_heads, axis=0) * qmask_ref[...],
                           jnp.concatenate([qx] * n_heads, axis=0) * xmask_ref[...]], axis=1)
    s = _dot_nt(qbd, kcat_ref[...])
    r = lax.broadcasted_iota(jnp.int32, s.shape, 0) % ls
    c = lax.broadcasted_iota(jnp.int32, s.shape, 1)
    s = jnp.where(c <= r + past, s, NEG)
    m = jnp.max(s, axis=1, keepdims=True)
    p = jnp.exp2(s - m)
    l = jnp.sum(p, axis=1, keepdims=True)
    o = _dot(p.astype(BF16), vall_ref[...]) / l
    lane_h = lax.broadcasted_iota(jnp.int32, (1, fw), 1) // HEAD_DIM
    out = jnp.zeros((ls, fw), F32)
    for h in range(n_heads):
        out = jnp.where(lane_h == h, o[h * ls:(h + 1) * ls, :], out)
    fo_ref[...] = out.astype(BF16)


def _sample_mix(q, kn, vn, lfn, u, ck, cv, clf, st, consts, chunk=256):
    t, fw = q.shape
    nb, past, _ = ck.shape
    ls = t // nb
    pw = u.shape[1]
    lk_pad = -(-(past + ls) // chunk) * chunk
    if lk_pad == past + ls:
        lk_pad += chunk
    n_heads = fw // HEAD_DIM
    hm = _head_lane_masks()
    qmask = np.zeros((n_heads * ls, fw), np.float32)
    xmask = np.zeros((n_heads * ls, LANES), np.float32)
    for h in range(n_heads):
        qmask[h * ls:(h + 1) * ls, h * HEAD_DIM:(h + 1) * HEAD_DIM] = 1.0
        xmask[h * ls:(h + 1) * ls, :] = hm[h]
    tri, pq, pk, oq, ok, wpool, spool, winl = consts[:8]
    tri_c = tri[:chunk, :chunk]
    cargs = [tri_c, pq, pk, oq, ok, jnp.asarray(qmask, dtype=BF16), jnp.asarray(xmask, dtype=BF16), wpool, spool, winl]
    row = lambda w: pl.BlockSpec((ls, w), lambda b: (b, 0))
    bat = lambda a: pl.BlockSpec((1,) + a.shape[1:], lambda b: (b, 0, 0))
    return pl.pallas_call(
        functools.partial(_sample_kernel, past=past, ls=ls, chunk=chunk), grid=(nb,),
        in_specs=[row(fw), row(fw), row(fw), row(FOX_H), row(pw), bat(ck), bat(cv), bat(clf), bat(st)]
                 + [_const_spec(a.shape) for a in cargs],
        out_specs=[row(fw), row(pw)],
        out_shape=[jax.ShapeDtypeStruct((t, fw), BF16), jax.ShapeDtypeStruct((t, pw), BF16)],
        scratch_shapes=[pltpu.VMEM((HIST_ROWS + ls, pw), F32), pltpu.VMEM((lk_pad, LANES), F32),
                        pltpu.VMEM((lk_pad, LANES), F32), pltpu.VMEM((lk_pad, fw + LANES), BF16),
                        pltpu.VMEM((lk_pad, fw), BF16)],
        compiler_params=_cparams(1), name="sample_mix")(q, kn, vn, lfn, u, ck, cv, clf, st, *cargs)


def _merge_kernel(x_ref, po_ref, fo_ref, co_ref, gt_ref, wbp_ref, wbf_ref, wbc_ref, wo_ref, gf_ref,
                  wrh_ref, wrl_ref, br_ref, tri_ref, cin_ref,
                  x1_ref, xn_ref, ri_ref, rw_ref, cnt_ref, carry_ref, *, d, n_exp, tm):
    @pl.when(pl.program_id(0) == 0)
    def _():
        carry_ref[...] = cin_ref[...]

    def gate(j):
        return gt_ref[:, j * d:(j + 1) * d].astype(F32)

    merged = (gate(0) * _dot(po_ref[...], wbp_ref[...])
              + gate(1) * _dot(fo_ref[...], wbf_ref[...])
              + gate(2) * _dot(co_ref[...], wbc_ref[...]))
    x1 = x_ref[...] + _dot(merged.astype(BF16), wo_ref[...])
    x1_ref[...] = x1
    xn = _rms(x1, gf_ref[...])
    xn_ref[...] = xn
    xh = xn.astype(BF16)
    xl = (xn - xh.astype(F32)).astype(BF16)
    logits = _dot(xh, wrh_ref[...]) + _dot(xl, wrh_ref[...]) + _dot(xh, wrl_ref[...]) + br_ref[...]

    lane = lax.broadcasted_iota(jnp.int32, logits.shape, 1)
    lane_f = lane.astype(F32)
    work = jnp.where(lane < n_exp, logits, NEG)
    vals, idxs, sels = [], [], []
    for _ in range(TOP_K):
        mx = jnp.max(work, axis=1, keepdims=True)
        idx = jnp.min(jnp.where(work == mx, lane_f, float(LANES)), axis=1, keepdims=True)
        sel = lane_f == idx
        vals.append(mx)
        idxs.append(idx)
        sels.append(sel)
        work = jnp.where(sel, NEG, work)
    es = [jnp.exp(v - vals[0]) for v in vals]
    den = es[0] + es[1] + es[2] + es[3]

    onehot = jnp.zeros(logits.shape, F32)
    for sel in sels:
        onehot = jnp.where(sel, 1.0, onehot)
    cum = _dot(tri_ref[...], onehot.astype(BF16)) + carry_ref[...]
    carry_ref[...] = cum[tm - 1:tm, :] + onehot[tm - 1:tm, :]
    cnt_ref[...] = carry_ref[...]

    ri = jnp.zeros(logits.shape, jnp.int32)
    rw = jnp.zeros(logits.shape, F32)
    for j in range(TOP_K):
        rank = jnp.sum(jnp.where(sels[j], cum, 0.0), axis=1, keepdims=True)
        ri = jnp.where(lane == j, idxs[j].astype(jnp.int32), ri)
        ri = jnp.where(lane == TOP_K + j, rank.astype(jnp.int32), ri)
        rw = jnp.where(lane == j, es[j] / den, rw)
    ri_ref[...] = ri
    rw_ref[...] = rw


def _merge(x, po, fo, co, gt, wbp, wbf, wbc, wo, g_ffn, wrh, wrl, br, tri_s, cnt_in, n_exp, tm=512):
    t, d = x.shape
    tm = min(tm, t)
    row = lambda a: pl.BlockSpec((tm, a.shape[1]), lambda i: (i, 0))
    consts = [wbp, wbf, wbc, wo, g_ffn, wrh, wrl, br, tri_s, cnt_in]
    sd = jax.ShapeDtypeStruct
    rspec = lambda w: pl.BlockSpec((tm, w), lambda i: (i, 0))
    return pl.pallas_call(
        functools.partial(_merge_kernel, d=d, n_exp=n_exp, tm=tm), grid=(t // tm,),
        in_specs=[row(x), row(po), row(fo), row(co), row(gt)] + [_const_spec(a.shape) for a in consts],
        out_specs=[rspec(d), rspec(d), rspec(LANES), rspec(LANES), _const_spec((1, LANES))],
        out_shape=[sd((t, d), F32), sd((t, d), F32), sd((t, LANES), jnp.int32), sd((t, LANES), F32),
                   sd((1, LANES), F32)],
        scratch_shapes=[pltpu.VMEM((1, LANES), F32)],
        compiler_params=_cparams(1), name="merge_router")(x, po, fo, co, gt, *consts)


def _wprep_kernel(w_ref, g_ref, u_ref, t_ref):
    fc = g_ref.shape[1]
    for c in range(t_ref.shape[0]):
        cols = slice(c * LANES, (c + 1) * LANES)
        t_ref[c] = w_ref[0, cols, :].T
        g_ref[0, :, cols] = t_ref[c, pl.ds(0, fc, stride=2), :].astype(BF16)
        u_ref[0, :, cols] = t_ref[c, pl.ds(1, fc, stride=2), :].astype(BF16)


def _wprep(w_gate_up, fc=256):
    e, d, f2 = w_gate_up.shape
    f = f2 // 2
    fc = min(fc, f)
    out = pl.BlockSpec((1, fc, d), lambda i, c: (i, c, 0))
    return pl.pallas_call(
        _wprep_kernel, grid=(e, f // fc),
        in_specs=[pl.BlockSpec((1, d, 2 * fc), lambda i, c: (i, 0, c))],
        out_specs=[out, out],
        out_shape=[jax.ShapeDtypeStruct((e, f, d), BF16)] * 2,
        scratch_shapes=[pltpu.VMEM((d // LANES, 2 * fc, LANES), F32)],
        compiler_params=_cparams(2), name="expert_weight_layout")(w_gate_up)


def _zero_pad_rows(ps_ref, pl_ref, xs_ref, z_ref, sem, n_exp, n_bits):
    z_ref[...] = jnp.zeros_like(z_ref)
    sub = 8

    def each(fn):
        for e in range(n_exp):
            start = ps_ref[e]
            n = pl_ref[e]
            head = jnp.minimum((-start) & (sub - 1), n)
            for r in range(sub - 1):
                @pl.when(r < head)
                def _():
                    fn(pltpu.make_async_copy(z_ref.at[pl.ds(0, 1), :], xs_ref.at[pl.ds(start + r, 1), :], sem))
            start8 = start + head
            n8 = (n - head) // sub
            for b in range(n_bits - 3):
                rows = sub << b
                off = pl.multiple_of(start8 + sub * (n8 & ((1 << b) - 1)), sub)

                @pl.when(((n8 >> b) & 1) == 1)
                def _():
                    fn(pltpu.make_async_copy(z_ref.at[pl.ds(0, rows), :], xs_ref.at[pl.ds(off, rows), :], sem))

    each(lambda c: c.start())
    each(lambda c: c.wait())


def _dispatch_kernel(pos_ref, x_ref, *rest, tm, first, n_exp, n_bits, n_chunks):
    if first and n_chunks:
        ps_ref, pl_ref, w_ref, xs_ref, g_ref, u_ref, sem, z_ref, t_ref = rest
    elif first:
        ps_ref, pl_ref, xs_ref, sem, z_ref = rest
    else:
        _, xs_ref, sem = rest
    if first:
        @pl.when(pl.program_id(0) == 0)
        def _():
            _zero_pad_rows(ps_ref, pl_ref, xs_ref, z_ref, sem, n_exp, n_bits)

    def issue(t, c):
        for j in range(TOP_K):
            pltpu.make_async_copy(x_ref.at[pl.ds(t, 1), :],
                                  xs_ref.at[pl.ds(pos_ref[t * TOP_K + j], 1), :], sem).start()
        return c

    lax.fori_loop(0, tm, issue, 0, unroll=DMA_UNROLL)
    if first and n_chunks:
        @pl.when(pl.program_id(0) < n_chunks)
        def _():
            _wprep_kernel(w_ref, g_ref, u_ref, t_ref)
    for j in range(TOP_K):
        pltpu.make_async_copy(x_ref, xs_ref.at[pl.ds(0, tm), :], sem).wait()


def _dispatch(pos_flat, xn, n_rows, pad=None, xs=None, w_gate_up=None, tm=DISPATCH_ROWS, tile_rows=512):
    t, d = xn.shape
    tm = min(tm, t)
    steps = t // tm
    first = xs is None
    n_bits = (tile_rows - 1).bit_length()
    smem = lambda n: pl.BlockSpec((n,), lambda i: (0,), memory_space=pltpu.SMEM)
    in_specs = [pl.BlockSpec((tm * TOP_K,), lambda i: (i,), memory_space=pltpu.SMEM),
                pl.BlockSpec((tm, d), lambda i: (i, 0))]
    scratch = [pltpu.SemaphoreType.DMA]
    out_specs = [pl.BlockSpec(memory_space=pl.ANY)]
    out_shape = [jax.ShapeDtypeStruct((n_rows, d), F32)]
    n_chunks = 0
    if first:
        n_exp = pad[0].shape[0]
        in_specs += [smem(n_exp), smem(n_exp)]
        args = [pos_flat, xn, pad[0], pad[1]]
        scratch.append(pltpu.VMEM((1 << (n_bits - 1), d), F32))
        aliases = {}
        if w_gate_up is not None:
            e, _, f2 = w_gate_up.shape
            f = f2 // 2
            per_exp = steps // e
            assert per_exp >= 1 and f % per_exp == 0
            fc = f // per_exp
            n_chunks = e * per_exp
            chunk = lambda i: jnp.minimum(i, n_chunks - 1)
            in_specs.append(pl.BlockSpec((1, d, 2 * fc), lambda i: (chunk(i) // per_exp, 0, chunk(i) % per_exp)))
            wout = pl.BlockSpec((1, fc, d), lambda i: (chunk(i) // per_exp, chunk(i) % per_exp, 0))
            out_specs += [wout, wout]
            out_shape += [jax.ShapeDtypeStruct((e, f, d), BF16)] * 2
            scratch.append(pltpu.VMEM((d // LANES, 2 * fc, LANES), F32))
            args.append(w_gate_up)
    else:
        n_exp = 0
        in_specs.append(pl.BlockSpec(memory_space=pl.ANY))
        args = [pos_flat, xn, xs]
        aliases = {2: 0}
    out = pl.pallas_call(
        functools.partial(_dispatch_kernel, tm=tm, first=first, n_exp=n_exp, n_bits=n_bits, n_chunks=n_chunks),
        grid=(steps,), in_specs=in_specs, out_specs=out_specs, out_shape=out_shape,
        scratch_shapes=scratch, input_output_aliases=aliases,
        compiler_params=_cparams(1), name="dispatch")(*args)
    return out if n_chunks else out[0]


def _ffn_kernel(te_ref, nu_ref, x_ref, wg_ref, wu_ref, wdf_ref, bg_ref, bu_ref, bd_ref, y_ref, wd_ref):
    i = pl.program_id(0)

    @pl.when(i < nu_ref[0])
    def _():
        @pl.when((i == 0) | (te_ref[i] != te_ref[jnp.maximum(i - 1, 0)]))
        def _():
            wd_ref[0] = wdf_ref[0].astype(BF16)

        x = x_ref[...].astype(BF16)
        g = _dot_nt(x, wg_ref[0]) + bg_ref[0]
        u = _dot_nt(x, wu_ref[0]) + bu_ref[0]
        gate = jnp.minimum(g, SWIGLU_LIMIT)
        up = jnp.clip(u, -SWIGLU_LIMIT, SWIGLU_LIMIT)
        act = (up + 1.0) * gate * jax.nn.sigmoid(SWIGLU_ALPHA * gate)
        y_ref[...] = _dot(act.astype(BF16), wd_ref[0]) + bd_ref[0]


def _ffn(tile_exp, n_used, xs, wg_t, wu_t, wd, bg, bu, bd, tm):
    p, d = xs.shape
    e, f, _ = wg_t.shape
    nt = p // tm
    rowi = lambda i, te, nu: (jnp.minimum(i, nu[0] - 1), 0)
    wi = lambda i, te, nu: (te[i], 0, 0)
    grid_spec = pltpu.PrefetchScalarGridSpec(
        num_scalar_prefetch=2, grid=(nt,),
        in_specs=[pl.BlockSpec((tm, d), rowi),
                  pl.BlockSpec((1, f, d), wi), pl.BlockSpec((1, f, d), wi), pl.BlockSpec((1, f, d), wi),
                  pl.BlockSpec((1, 1, f), wi), pl.BlockSpec((1, 1, f), wi), pl.BlockSpec((1, 1, d), wi)],
        out_specs=pl.BlockSpec((tm, d), rowi),
        scratch_shapes=[pltpu.VMEM((1, f, d), BF16)])
    return pl.pallas_call(
        _ffn_kernel, grid_spec=grid_spec, out_shape=jax.ShapeDtypeStruct((p, d), F32),
        compiler_params=_cparams(1), name="expert_ffn")(tile_exp, n_used, xs, wg_t, wu_t, wd, bg, bu, bd)


def _combine_kernel(pos_ref, posn_ref, w_ref, x1_ref, gf_ref, y_ref, o_ref, buf_ref, sems, *, tm):
    i = pl.program_id(0)
    n = pl.num_programs(0)

    def gather(idx_ref, s):
        def issue(t, c):
            for j in range(TOP_K):
                pltpu.make_async_copy(y_ref.at[pl.ds(idx_ref[t * TOP_K + j], 1), :],
                                      buf_ref.at[s, j, pl.ds(t, 1), :], sems.at[s]).start()
            return c

        lax.fori_loop(0, tm, issue, 0, unroll=DMA_UNROLL)

    def step(slot):
        @pl.when(i == 0)
        def _():
            gather(pos_ref, slot)

        @pl.when(i + 1 < n)
        def _():
            gather(posn_ref, 1 - slot)

        for j in range(TOP_K):
            pltpu.make_async_copy(y_ref.at[pl.ds(0, tm), :], buf_ref.at[slot, j], sems.at[slot]).wait()
        w = w_ref[...]
        acc = x1_ref[...]
        for j in range(TOP_K):
            acc = acc + w[:, j:j + 1] * buf_ref[slot, j]
        o_ref[...] = _rms(acc, gf_ref[...])

    for parity in range(2):
        pl.when(i % 2 == parity)(functools.partial(step, parity))


def _combine(pos_flat, rw, x1, g_final, y, tm=512):
    t, d = x1.shape
    tm = min(tm, t)
    n = t // tm
    return pl.pallas_call(
        functools.partial(_combine_kernel, tm=tm), grid=(n,),
        in_specs=[pl.BlockSpec((tm * TOP_K,), lambda i: (i,), memory_space=pltpu.SMEM),
                  pl.BlockSpec((tm * TOP_K,), lambda i: (jnp.minimum(i + 1, n - 1),), memory_space=pltpu.SMEM),
                  pl.BlockSpec((tm, LANES), lambda i: (i, 0)),
                  pl.BlockSpec((tm, d), lambda i: (i, 0)),
                  _const_spec((1, d)),
                  pl.BlockSpec(memory_space=pl.ANY)],
        out_specs=pl.BlockSpec((tm, d), lambda i: (i, 0)),
        out_shape=jax.ShapeDtypeStruct((t, d), F32),
        scratch_shapes=[pltpu.VMEM((2, TOP_K, tm, d), F32), pltpu.SemaphoreType.DMA((2,))],
        compiler_params=_cparams(1), name="combine")(pos_flat, pos_flat, rw, x1, g_final, y)


def kernel(x_prompt, x_sample, mem_prompt, cache_fox_k, cache_fox_v, cache_fox_logf, state_pool, cache_mem_k, cache_mem_v, g_mix, w_in, b_f, w_pool, s_pool, w_br_pool, w_br_fox, w_br_cross, b_gates, w_out, g_mem, w_mem_kv, g_ffn, w_router, b_router, w_gate_up, b_gate_up, w_down, b_down, g_final):
    depth = w_in.shape[0]
    assert depth == 1, "single-layer model"
    bp, lp, d = x_prompt.shape
    bs, ls, _ = x_sample.shape
    past = cache_fox_k.shape[2]
    n_mem = mem_prompt.shape[1]
    n_exp = w_router.shape[2]
    d_ff = w_down.shape[2]
    pool_w = state_pool.shape[3]
    fox_w = FOX_H * HEAD_DIM
    cross_w = CROSS_H * HEAD_DIM
    scale = HEAD_DIM ** -0.5

    w = w_in[0]
    o_q = pool_w
    o_k, o_v, o_f = o_q + fox_w, o_q + 2 * fox_w, o_q + 3 * fox_w
    o_cq = o_f + FOX_H
    o_g = o_cq + cross_w
    w_cat = jnp.concatenate(
        [w[:, :o_q], w[:, o_q:o_k] * (scale * LOG2E), w[:, o_k:o_v], w[:, o_v:o_f], w[:, o_cq:o_g] * scale, w[:, o_g:],
         jnp.pad(w[:, o_f:o_cq], ((0, 0), (0, LANES - FOX_H)))], axis=1).astype(BF16)
    widths = [pool_w, fox_w, fox_w, fox_w, cross_w, N_BRANCH * d, LANES]
    offs = tuple(int(v) for v in np.concatenate([[0], np.cumsum(widths)]))
    b_f_p = jnp.pad(b_f[0], (0, LANES - FOX_H)).reshape(1, LANES)
    b_g = b_gates[0].reshape(1, -1)
    g_mix2 = g_mix[0].reshape(1, d)
    gw = pool_w // len(POOL_WINDOWS)
    wpool_bd = jnp.zeros((pool_w, pool_w), F32)
    for g in range(len(POOL_WINDOWS)):
        wpool_bd = wpool_bd.at[g * gw:(g + 1) * gw, g * gw:(g + 1) * gw].set(w_pool[0, g])
    tm = 512
    pq, pk, oq, ok = _bias_placement()
    consts = (_tri(tm, strict=False), pq, pk, oq, ok, wpool_bd.astype(BF16), s_pool[0].reshape(1, pool_w),
              _pool_lane_windows(pool_w), _head_selector(fox_w))
    hmask = jnp.asarray(_head_lane_masks()[:, :, None])

    xp = x_prompt.reshape(bp * lp, d)
    (u_p, po_p, q_p, qx_p, kb_p, kx_p, vb_p, k_p, v_p, lf_p, qc_p, gt_p, stats_p) = _inproj(
        xp, g_mix2, w_cat, b_f_p, b_g, offs, prompt=True, seq_len=lp, consts=consts, tm=tm)
    mk, mv = _memkv(mem_prompt.reshape(bp * n_mem, d), g_mem[0].reshape(1, d), w_mem_kv[0].astype(BF16))
    mk3, mv3 = mk.reshape(bp, n_mem, cross_w), mv.reshape(bp, n_mem, cross_w)
    k_first = _first_needed_block(stats_p, bp, fox_w // LANES)
    fo_p = _fox_prompt(k_first, hmask, q_p, qx_p, kb_p, kx_p, vb_p, bp)
    co_p = _cross(qc_p, mk3, mv3, bp)

    xs_ = x_sample.reshape(bs * ls, d)
    (u_s, q_s, k_s, v_s, lf_s, qc_s, gt_s) = _inproj(xs_, g_mix2, w_cat, b_f_p, b_g, offs, prompt=False, tm=tm)
    fo_s, po_s = _sample_mix(q_s, k_s, v_s, lf_s, u_s,
                             cache_fox_k[0].reshape(bs, past, fox_w), cache_fox_v[0].reshape(bs, past, fox_w),
                             cache_fox_logf[0], state_pool[0], consts)
    co_s = _cross(qc_s, cache_mem_k[0].reshape(bs, n_mem, cross_w), cache_mem_v[0].reshape(bs, n_mem, cross_w), bs)

    wbp, wbf, wbc = w_br_pool[0].astype(BF16), w_br_fox[0].astype(BF16), w_br_cross[0].astype(BF16)
    wo = w_out[0].astype(BF16)
    g_ffn2 = g_ffn[0].reshape(1, d)
    wr = jnp.pad(w_router[0], ((0, 0), (0, LANES - n_exp)))
    wrh = wr.astype(BF16)
    wrl = (wr - wrh.astype(F32)).astype(BF16)
    br = jnp.pad(b_router[0], (0, LANES - n_exp)).reshape(1, LANES)
    tri_s = _tri(tm, strict=True)
    margs = (wbp, wbf, wbc, wo, g_ffn2, wrh, wrl, br, tri_s)
    x1_p, xn_p, ri_p, rw_p, cnt_p = _merge(xp, po_p, fo_p, co_p, gt_p, *margs, jnp.zeros((1, LANES), F32), n_exp, tm)
    x1_s, xn_s, ri_s, rw_s, cnt = _merge(xs_, po_s, fo_s, co_s, gt_s, *margs, cnt_p, n_exp, tm)

    tmf = 512
    t_all = bp * lp + bs * ls
    counts = cnt[0, :n_exp].astype(jnp.int32)
    tiles_e = (counts + tmf - 1) // tmf
    tile_end = jnp.cumsum(tiles_e)
    row_off = (tile_end - tiles_e) * tmf
    nt_max = (t_all * TOP_K + n_exp * (tmf - 1)) // tmf + 1
    n_used = tile_end[-1:]
    tile_ids = jnp.minimum(jnp.arange(nt_max, dtype=jnp.int32), n_used[0] - 1)
    tile_exp = jnp.minimum(jnp.sum(tile_ids[:, None] >= tile_end[None, :], axis=1), n_exp - 1).astype(jnp.int32)

    def positions(ri):
        e = ri[:, :TOP_K]
        r = ri[:, TOP_K:2 * TOP_K]
        off = jnp.sum(jnp.where(e[:, :, None] == jnp.arange(n_exp)[None, None, :], row_off[None, None, :], 0), axis=2)
        return (off + r).reshape(-1).astype(jnp.int32)

    pos_p, pos_s = positions(ri_p), positions(ri_s)

    wd = w_down[0]
    bgu = b_gate_up[0]
    bg_e, bu_e = bgu[:, 0::2].reshape(n_exp, 1, d_ff), bgu[:, 1::2].reshape(n_exp, 1, d_ff)
    bd_e = b_down[0].reshape(n_exp, 1, d)
    pad = ((row_off + counts).astype(jnp.int32), (tiles_e * tmf - counts).astype(jnp.int32))
    if (bp * lp) // min(DISPATCH_ROWS, bp * lp) >= n_exp:
        xs_sorted, wg_t, wu_t = _dispatch(pos_p, xn_p, nt_max * tmf, pad=pad, w_gate_up=w_gate_up[0], tile_rows=tmf)
    else:
        wg_t, wu_t = _wprep(w_gate_up[0])
        xs_sorted = _dispatch(pos_p, xn_p, nt_max * tmf, pad=pad, tile_rows=tmf)
    xs_sorted = _dispatch(pos_s, xn_s, nt_max * tmf, xs=xs_sorted, tile_rows=tmf)
    y = _ffn(tile_exp, n_used.astype(jnp.int32), xs_sorted, wg_t, wu_t, wd, bg_e, bu_e, bd_e, tmf)
    g_fin = g_final.reshape(1, d)
    y_p = _combine(pos_p, rw_p, x1_p, g_fin, y)
    y_s = _combine(pos_s, rw_s, x1_s, g_fin, y)

    kv5 = lambda a, b, l: a.reshape(1, b, l, FOX_H, HEAD_DIM)
    return (y_p.reshape(bp, lp, d), y_s.reshape(bs, ls, d),
            kv5(k_p, bp, lp), kv5(v_p, bp, lp), lf_p.reshape(1, bp, lp, FOX_H),
            u_p.reshape(bp, lp, pool_w)[:, lp - POOL_HIST:, :][None],
            mk.reshape(1, bp, n_mem, CROSS_H, HEAD_DIM), mv.reshape(1, bp, n_mem, CROSS_H, HEAD_DIM),
            kv5(k_s, bs, ls), kv5(v_s, bs, ls), lf_s.reshape(1, bs, ls, FOX_H),
            u_s.reshape(bs, ls, pool_w)[:, ls - POOL_HIST:, :][None])
```

```python
import functools

import jax
import jax.numpy as jnp
import numpy as np
from jax import lax
from jax.experimental import pallas as pl
from jax.experimental.pallas import tpu as pltpu

F32 = jnp.float32
BF16 = jnp.bfloat16

HEAD_DIM = 64
FOX_H = 8
CROSS_H = 4
POOL_WINDOWS = (2, 4, 8, 16)
POOL_HIST = 15
N_BRANCH = 3
TOP_K = 4
SWIGLU_LIMIT = 7.0
SWIGLU_ALPHA = 1.702
RMS_EPS = 1e-5
NEG = -1e30
LOG2E = 1.4426950408889634
ZERO_EXP2 = 150.0
V_ROWS = HEAD_DIM + 16

LANES = 128
DMA_UNROLL = 4
HIST_ROWS = 16
VMEM_LIMIT = 56 * 1024 * 1024


def _cparams(n_axes=1, vmem=VMEM_LIMIT):
    return pltpu.CompilerParams(dimension_semantics=("arbitrary",) * n_axes, vmem_limit_bytes=vmem)


def _const_spec(shape):
    nd = len(shape)
    return pl.BlockSpec(shape, lambda *_: (0,) * nd)


def _split3(x):
    hi = x.astype(BF16)
    r = x - hi.astype(F32)
    mid = r.astype(BF16)
    lo = (r - mid.astype(F32)).astype(BF16)
    return hi, mid, lo


def _rms(x, g):
    ms = jnp.mean(x * x, axis=-1, keepdims=True)
    return x * lax.rsqrt(ms + RMS_EPS) * g


def _log_sigmoid(z):
    return jnp.minimum(z, 0.0) - jnp.log1p(jnp.exp(-jnp.abs(z)))


def _dot(a, b):
    return jnp.dot(a, b, preferred_element_type=F32)


def _dot_nt(a, b):
    return lax.dot_general(a, b, (((1,), (1,)), ((), ())), preferred_element_type=F32)


def _tri(n, strict):
    r = np.arange(n)
    m = (r[None, :] < r[:, None]) if strict else (r[None, :] <= r[:, None])
    return jnp.asarray(m.astype(np.float32), dtype=BF16)


def _bias_placement():
    pq = np.zeros((3 * LANES, LANES), np.float32)
    pk = np.zeros((3 * LANES, LANES), np.float32)
    oq = np.zeros((1, LANES), np.float32)
    ok = np.zeros((1, LANES), np.float32)
    for p in range(3):
        for h in range(FOX_H):
            pq[p * LANES + h, 8 * p + h] = 1.0
            pk[p * LANES + h, 24 + 8 * p + h] = -1.0
            oq[0, 24 + 8 * p + h] = 1.0
            ok[0, 8 * p + h] = 1.0
    return (jnp.asarray(pq, dtype=BF16), jnp.asarray(pk, dtype=BF16), jnp.asarray(oq), jnp.asarray(ok))


def _head_lane_masks():
    m = np.zeros((FOX_H, LANES), np.float32)
    for p in range(3):
        for h in range(FOX_H):
            m[h, 8 * p + h] = 1.0
            m[h, 24 + 8 * p + h] = 1.0
    return m


def _head_selector(fox_w):
    m = np.zeros((fox_w, LANES), np.float32)
    for h in range(fox_w // HEAD_DIM):
        m[h * HEAD_DIM:(h + 1) * HEAD_DIM, h] = 1.0
    return jnp.asarray(m, dtype=BF16)


def _pool_lane_windows(pool_w):
    gw = pool_w // len(POOL_WINDOWS)
    return jnp.asarray(np.repeat(np.asarray(POOL_WINDOWS, np.float32), gw)[None, :])


SEC_U, SEC_Q, SEC_K, SEC_V, SEC_CQ, SEC_G, SEC_F = range(7)


def _pool_mix(ext, u, row0, winl, wpool, spool):
    n = u.shape[0]
    s1 = ext + pltpu.roll(ext, 1, 0)
    s2 = s1 + pltpu.roll(s1, 2, 0)
    s3 = s2 + pltpu.roll(s2, 4, 0)
    s4 = s3 + pltpu.roll(s3, 8, 0)
    win = jnp.where(winl == 2.0, s1, jnp.where(winl == 4.0, s2, jnp.where(winl == 8.0, s3, s4)))
    win = win[HIST_ROWS:, :]
    pos = (row0 + lax.broadcasted_iota(jnp.int32, (n, 1), 0)).astype(F32)
    cnt = jnp.minimum(pos + 1.0, winl)
    pooled = win / cnt - u
    return (_dot(pooled.astype(BF16), wpool) * spool).astype(BF16)


def _inproj_kernel(x_ref, g_ref, w_ref, bf_ref, bg_ref, *rest, offs, prompt, tiles_per_batch, tm, n_gate_chunks):
    if prompt:
        (tri_ref, pq_ref, pk_ref, oq_ref, ok_ref, wpool_ref, spool_ref, winl_ref, hsel_ref,
         u_ref, po_ref, q_ref, qx_ref, kb_ref, kx_ref, vb_ref, k_ref, v_ref, lf_ref, qc_ref, gt_ref, st_ref,
         carry_ref, hist_ref) = rest
    else:
        (u_ref, q_ref, k_ref, v_ref, lf_ref, qc_ref, gt_ref) = rest

    h = _rms(x_ref[...], g_ref[...]).astype(BF16)

    def sec(s):
        return _dot(h, w_ref[:, offs[s]:offs[s + 1]])

    u = sec(SEC_U)
    u_ref[...] = u
    q = sec(SEC_Q)
    k = sec(SEC_K)
    v = sec(SEC_V)
    k_ref[...] = k
    v_ref[...] = v
    qc_ref[...] = sec(SEC_CQ).astype(BF16)
    gw = (offs[SEC_G + 1] - offs[SEC_G]) // n_gate_chunks
    for c in range(n_gate_chunks):
        a = offs[SEC_G] + c * gw
        z = _dot(h, w_ref[:, a:a + gw]) + bg_ref[:, c * gw:(c + 1) * gw]
        gt_ref[:, c * gw:(c + 1) * gw] = jax.nn.sigmoid(z).astype(BF16)
    zf = sec(SEC_F) + bf_ref[...]
    lane = lax.broadcasted_iota(jnp.int32, zf.shape, 1)
    logf = jnp.where(lane < FOX_H, _log_sigmoid(zf), 0.0)
    lf_ref[...] = logf[:, :FOX_H]

    if not prompt:
        q_ref[...] = q.astype(BF16)
        return

    tib = pl.program_id(0) % tiles_per_batch

    @pl.when(tib == 0)
    def _():
        carry_ref[...] = jnp.zeros_like(carry_ref)
        hist_ref[...] = jnp.zeros_like(hist_ref)

    n_pair = q.shape[1] // LANES
    for hp in range(n_pair):
        sl = slice(hp * LANES, (hp + 1) * LANES)
        q_ref[hp] = q[:, sl].T.astype(BF16)
        kb_ref[hp] = k[:, sl].astype(BF16)
        vt = v[:, sl].T.astype(BF16)
        for a in range(2):
            vb_ref[2 * hp + a, 0, 0:HEAD_DIM, :] = vt[a * HEAD_DIM:(a + 1) * HEAD_DIM, :]
            vb_ref[2 * hp + a, 0, HEAD_DIM:V_ROWS, :] = jnp.ones((V_ROWS - HEAD_DIM, tm), BF16)

    tri = tri_ref[...]
    hi, mid, lo = _split3(logf)
    cum = _dot(tri, hi) + _dot(tri, mid) + _dot(tri, lo) + carry_ref[...]
    carry_ref[...] = cum[tm - 1:tm, :]
    cum2 = cum * LOG2E
    cp = jnp.concatenate(_split3(cum2), axis=1)
    qx_ref[...] = (_dot(cp, pq_ref[...]) + oq_ref[...]).T.astype(BF16)
    kx_ref[...] = (_dot(cp, pk_ref[...]) + ok_ref[...]).astype(BF16)

    def max_sq_norm(a):
        ab = a.astype(BF16).astype(F32)
        sq = ab * ab
        sh = sq.astype(BF16)
        sl = (sq - sh.astype(F32)).astype(BF16)
        return jnp.max(_dot(sh, hsel_ref[...]) + _dot(sl, hsel_ref[...]), axis=0, keepdims=True)

    srow = lax.broadcasted_iota(jnp.int32, (8, LANES), 0)
    st_ref[0] = jnp.where(srow == 0, max_sq_norm(q),
                          jnp.where(srow == 1, max_sq_norm(k),
                                    jnp.where(srow == 2, cum2[0:1, :],
                                              jnp.where(srow == 3, cum2[tm - 1:tm, :], 0.0))))

    ext = jnp.concatenate([hist_ref[...], u], axis=0)
    hist_ref[...] = u[tm - HIST_ROWS:, :]
    po_ref[...] = _pool_mix(ext, u, tib * tm, winl_ref[...], wpool_ref[...], spool_ref[...])


def _inproj(x, g_mix, w_cat, b_f, b_g, offs, *, prompt, seq_len=None, consts=None, tm=512):
    t, d = x.shape
    tm = min(tm, t)
    n_tiles = t // tm
    pool_w = offs[SEC_U + 1] - offs[SEC_U]
    fox_w = offs[SEC_Q + 1] - offs[SEC_Q]
    cross_w = offs[SEC_CQ + 1] - offs[SEC_CQ]
    gate_w = offs[SEC_G + 1] - offs[SEC_G]
    n_pair = fox_w // LANES
    row = lambda w: pl.BlockSpec((tm, w), lambda i: (i, 0))
    pair = pl.BlockSpec((n_pair, tm, LANES), lambda i: (0, i, 0))
    in_specs = [row(d), _const_spec((1, d)),
                pl.BlockSpec(w_cat.shape, lambda i: (0, 0), pipeline_mode=pl.Buffered(1)),
                _const_spec((1, LANES)), _const_spec((1, gate_w))]
    args = [x, g_mix, w_cat, b_f, b_g]
    sd = jax.ShapeDtypeStruct
    if prompt:
        in_specs += [_const_spec(a.shape) for a in consts]
        args += list(consts)
        pair_t = pl.BlockSpec((n_pair, LANES, tm), lambda i: (0, 0, i))
        head_t = pl.BlockSpec((FOX_H, 1, V_ROWS, tm), lambda i: (0, i, 0, 0))
        out_shape = [sd((t, pool_w), F32), sd((t, pool_w), BF16),
                     sd((n_pair, LANES, t), BF16), sd((LANES, t), BF16),
                     sd((n_pair, t, LANES), BF16), sd((t, LANES), BF16),
                     sd((FOX_H, n_tiles, V_ROWS, tm), BF16),
                     sd((t, fox_w), F32), sd((t, fox_w), F32), sd((t, FOX_H), F32),
                     sd((t, cross_w), BF16), sd((t, gate_w), BF16), sd((n_tiles, 8, LANES), F32)]
        out_specs = [row(pool_w), row(pool_w), pair_t, pl.BlockSpec((LANES, tm), lambda i: (0, i)),
                     pair, row(LANES), head_t,
                     row(fox_w), row(fox_w), row(FOX_H), row(cross_w), row(gate_w),
                     pl.BlockSpec((1, 8, LANES), lambda i: (i, 0, 0))]
        scratch = [pltpu.VMEM((1, LANES), F32), pltpu.VMEM((HIST_ROWS, pool_w), F32)]
        tiles_per_batch = seq_len // tm
    else:
        out_shape = [sd((t, pool_w), F32), sd((t, fox_w), BF16), sd((t, fox_w), F32), sd((t, fox_w), F32),
                     sd((t, FOX_H), F32), sd((t, cross_w), BF16), sd((t, gate_w), BF16)]
        out_specs = [row(pool_w), row(fox_w), row(fox_w), row(fox_w), row(FOX_H), row(cross_w), row(gate_w)]
        scratch = []
        tiles_per_batch = 1
    kern = functools.partial(_inproj_kernel, offs=offs, prompt=prompt, tiles_per_batch=tiles_per_batch,
                             tm=tm, n_gate_chunks=N_BRANCH)
    return pl.pallas_call(
        kern, grid=(n_tiles,), in_specs=in_specs, out_specs=out_specs, out_shape=out_shape,
        scratch_shapes=scratch, compiler_params=_cparams(1),
        name="inproj_prompt" if prompt else "inproj_sample")(*args)


def _memkv_kernel(m_ref, g_ref, w_ref, k_ref, v_ref, *, cw):
    h = _rms(m_ref[...], g_ref[...]).astype(BF16)
    kv = _dot(h, w_ref[...])
    k_ref[...] = kv[:, :cw]
    v_ref[...] = kv[:, cw:]


def _memkv(mem, g_mem, w_kv, tm=256):
    t, d = mem.shape
    cw = w_kv.shape[1] // 2
    tm = min(tm, t)
    row = lambda w: pl.BlockSpec((tm, w), lambda i: (i, 0))
    return pl.pallas_call(
        functools.partial(_memkv_kernel, cw=cw), grid=(t // tm,),
        in_specs=[row(d), _const_spec((1, d)), _const_spec(w_kv.shape)],
        out_specs=[row(cw), row(cw)],
        out_shape=[jax.ShapeDtypeStruct((t, cw), F32)] * 2,
        compiler_params=_cparams(1), name="memkv")(mem, g_mem, w_kv)


def _cross_kernel(q_ref, k_ref, v_ref, o_ref):
    q = q_ref[...]
    kk = k_ref[0].astype(BF16)
    vv = v_ref[0].astype(BF16)
    lane = lax.broadcasted_iota(jnp.int32, (1, q.shape[1]), 1) // HEAD_DIM
    out = jnp.zeros(q.shape, F32)
    for h in range(q.shape[1] // HEAD_DIM):
        hm = lane == h
        s = _dot_nt(jnp.where(hm, q, jnp.zeros_like(q)), kk)
        m = jnp.max(s, axis=1, keepdims=True)
        p = jnp.exp(s - m)
        l = jnp.sum(p, axis=1, keepdims=True)
        o = _dot(p.astype(BF16), vv) / l
        out = jnp.where(hm, o, out)
    o_ref[...] = out.astype(BF16)


def _cross(qc, mk, mv, n_batch, tl=512):
    t, cw = qc.shape
    l = t // n_batch
    tl = min(tl, l)
    nl = l // tl
    m = mk.shape[1]
    return pl.pallas_call(
        _cross_kernel, grid=(n_batch, nl),
        in_specs=[pl.BlockSpec((tl, cw), lambda b, i: (b * nl + i, 0)),
                  pl.BlockSpec((1, m, cw), lambda b, i: (b, 0, 0)),
                  pl.BlockSpec((1, m, cw), lambda b, i: (b, 0, 0))],
        out_specs=pl.BlockSpec((tl, cw), lambda b, i: (b * nl + i, 0)),
        out_shape=jax.ShapeDtypeStruct((t, cw), BF16),
        compiler_params=_cparams(2), name="cross_attn")(qc, mk, mv)


def _fox_kernel(kf_ref, hm_ref, q_ref, qx_ref, k_ref, kx_ref, v_ref, o_ref,
                m0_ref, m1_ref, acc0_ref, acc1_ref, s0_ref, *, tq, n_pair, nq):
    qi = pl.program_id(2)
    qt = q_ref[0]
    qxt = qx_ref[...]
    row = lax.broadcasted_iota(jnp.int32, (LANES, 1), 0)
    zero = jnp.zeros_like(qt)
    qs = []
    for a in range(2):
        half = (row < HEAD_DIM) if a == 0 else (row >= HEAD_DIM)
        qs.append(jnp.concatenate([jnp.where(half, qt, zero), jnp.where(hm_ref[a] > 0.5, qxt, zero)], axis=0))
    ms = (m0_ref, m1_ref)
    accs = (acc0_ref, acc1_ref)
    for a in range(2):
        ms[a][...] = jnp.full_like(ms[a], NEG)
        accs[a][...] = jnp.zeros_like(accs[a])

    def scores(a, ki):
        ks = pl.multiple_of(ki * tq, tq)
        kcat = jnp.concatenate([k_ref[0, pl.ds(ks, tq), :], kx_ref[pl.ds(ks, tq), :]], axis=1)
        return _dot(kcat, qs[a])

    def softmax_pv(a, s, ki, masked):
        if masked:
            r = lax.broadcasted_iota(jnp.int32, s.shape, 0)
            c = lax.broadcasted_iota(jnp.int32, s.shape, 1)
            s = jnp.where(r <= c, s, NEG)
        m_old = ms[a][...]
        m_new = jnp.maximum(m_old, jnp.max(s, axis=0, keepdims=True))
        alpha = jnp.exp2(m_old - m_new)
        p = jnp.exp2(s - m_new).astype(BF16)
        accs[a][...] = alpha * accs[a][...] + _dot(v_ref[a, ki], p)
        ms[a][...] = m_new

    k_first = kf_ref[(pl.program_id(0) * n_pair + pl.program_id(1)) * nq + qi]
    s0_ref[...] = scores(0, qi)
    s1 = scores(1, qi)
    softmax_pv(0, s0_ref[...], qi, True)
    s0_ref[...] = scores(0, jnp.maximum(qi - 1, 0))
    softmax_pv(1, s1, qi, True)

    def body(j, carry):
        ki = qi - 1 - j

        @pl.when(ki >= k_first)
        def _():
            s1 = scores(1, ki)
            softmax_pv(0, s0_ref[...], ki, False)
            s0_ref[...] = scores(0, jnp.maximum(ki - 1, 0))
            softmax_pv(1, s1, ki, False)
        return carry

    lax.fori_loop(0, qi, body, 0)
    outs = [accs[a][0:HEAD_DIM, :] / accs[a][HEAD_DIM:HEAD_DIM + 1, :] for a in range(2)]
    o_ref[...] = jnp.concatenate(outs, axis=0).T.astype(BF16)


def _first_needed_block(stats, n_batch, n_pair):
    nt = stats.shape[0]
    nq = nt // n_batch
    st = stats[:, :4, :FOX_H].reshape(n_batch, nq, 4, FOX_H)
    slack = 1.0 + 2.0 ** -8
    qn, kn = jnp.sqrt(st[:, :, 0]) * slack, jnp.sqrt(st[:, :, 1]) * slack
    c_first, c_last = st[:, :, 2], st[:, :, 3]
    upper = qn[:, :, None, :] * kn[:, None, :, :] + (c_first[:, :, None, :] - c_last[:, None, :, :])
    lower = -(qn * kn)[:, :, None, :]
    qi = jnp.arange(nq)[None, :, None, None]
    ki = jnp.arange(nq)[None, None, :, None]
    needed = (ki == qi) | ((ki < qi) & ~(upper - lower + 2.0 < -ZERO_EXP2))
    first = jnp.min(jnp.where(needed, ki, nq), axis=2).astype(jnp.int32)
    first = jnp.min(first.reshape(n_batch, nq, n_pair, 2), axis=3)
    return jnp.transpose(first, (0, 2, 1)).reshape(-1)


def _fox_prompt(k_first, hmask, q, qx, kb, kx, vb, n_batch):
    n_pair, _, t = q.shape
    _, n_tiles, _, tq = vb.shape
    l = t // n_batch
    nq = l // tq
    grid_spec = pltpu.PrefetchScalarGridSpec(
        num_scalar_prefetch=1, grid=(n_batch, n_pair, nq),
        in_specs=[pl.BlockSpec((2, LANES, 1), lambda b, h, i, kf: (h, 0, 0)),
                  pl.BlockSpec((1, LANES, tq), lambda b, h, i, kf: (h, 0, b * nq + i)),
                  pl.BlockSpec((LANES, tq), lambda b, h, i, kf: (0, b * nq + i)),
                  pl.BlockSpec((1, l, LANES), lambda b, h, i, kf: (h, b, 0)),
                  pl.BlockSpec((l, LANES), lambda b, h, i, kf: (b, 0)),
                  pl.BlockSpec((2, nq, V_ROWS, tq), lambda b, h, i, kf: (h, b, 0, 0))],
        out_specs=pl.BlockSpec((tq, LANES), lambda b, h, i, kf: (b * nq + i, h)),
        scratch_shapes=[pltpu.VMEM((1, tq), F32), pltpu.VMEM((1, tq), F32),
                        pltpu.VMEM((V_ROWS, tq), F32), pltpu.VMEM((V_ROWS, tq), F32), pltpu.VMEM((tq, tq), F32)])
    return pl.pallas_call(
        functools.partial(_fox_kernel, tq=tq, n_pair=n_pair, nq=nq), grid_spec=grid_spec,
        out_shape=jax.ShapeDtypeStruct((t, n_pair * LANES), BF16),
        compiler_params=_cparams(3), name="fox_prompt")(k_first, hmask, q, qx, kb, kx, vb)


def _sample_kernel(q_ref, kn_ref, vn_ref, lfn_ref, u_ref, ck_ref, cv_ref, clf_ref, st_ref,
                   tri_ref, pq_ref, pk_ref, oq_ref, ok_ref, qmask_ref, xmask_ref, wpool_ref, spool_ref, winl_ref,
                   fo_ref, po_ref,
                   ext_ref, lf_ref, cum_ref, kcat_ref, vall_ref, *, past, ls, chunk):
    fw = q_ref.shape[1]
    lk_pad = kcat_ref.shape[0]
    n_heads = fw // HEAD_DIM

    u = u_ref[...]
    ext_ref[...] = jnp.zeros_like(ext_ref)
    ext_ref[pl.ds(HIST_ROWS - POOL_HIST, POOL_HIST), :] = st_ref[0]
    ext_ref[pl.ds(HIST_ROWS, ls), :] = u
    po_ref[...] = _pool_mix(ext_ref[...], u, past, winl_ref[...], wpool_ref[...], spool_ref[...])

    lf_ref[...] = jnp.zeros_like(lf_ref)
    lf_ref[pl.ds(0, past), pl.ds(0, FOX_H)] = clf_ref[0]
    lf_ref[pl.ds(past, ls), pl.ds(0, FOX_H)] = lfn_ref[...]
    tri = tri_ref[...]
    carry = jnp.zeros((1, LANES), F32)
    for c in range(lk_pad // chunk):
        rows = pl.ds(c * chunk, chunk)
        hi, mid, lo = _split3(lf_ref[rows, :])
        cum = _dot(tri, hi) + _dot(tri, mid) + _dot(tri, lo) + carry
        carry = cum[chunk - 1:chunk, :]
        cum_ref[rows, :] = cum
        cp = jnp.concatenate(_split3(cum * LOG2E), axis=1)
        kcat_ref[rows, pl.ds(fw, LANES)] = (_dot(cp, pk_ref[...]) + ok_ref[...]).astype(BF16)

    kcat_ref[pl.ds(0, past), pl.ds(0, fw)] = ck_ref[0].astype(BF16)
    kcat_ref[pl.ds(past, ls), pl.ds(0, fw)] = kn_ref[...].astype(BF16)
    vall_ref[pl.ds(0, past), :] = cv_ref[0].astype(BF16)
    vall_ref[pl.ds(past, ls), :] = vn_ref[...].astype(BF16)
    npad = lk_pad - past - ls
    kcat_ref[pl.ds(past + ls, npad), pl.ds(0, fw)] = jnp.zeros((npad, fw), BF16)
    vall_ref[pl.ds(past + ls, npad), :] = jnp.zeros((npad, fw), BF16)

    cq = cum_ref[pl.ds(past, ls), :]
    cpq = jnp.concatenate(_split3(cq * LOG2E), axis=1)
    qx = (_dot(cpq, pq_ref[...]) + oq_ref[...]).astype(BF16)
    q = q_ref[...]
    qbd = jnp.concatenate([jnp.concatenate([q] * n_heads, axis=0) * qmask_ref[...],
                           jnp.concatenate([qx] * n_heads, axis=0) * xmask_ref[...]], axis=1)
    s = _dot_nt(qbd, kcat_ref[...])
    r = lax.broadcasted_iota(jnp.int32, s.shape, 0) % ls
    c = lax.broadcasted_iota(jnp.int32, s.shape, 1)
    s = jnp.where(c <= r + past, s, NEG)
    m = jnp.max(s, axis=1, keepdims=True)
    p = jnp.exp2(s - m)
    l = jnp.sum(p, axis=1, keepdims=True)
    o = _dot(p.astype(BF16), vall_ref[...]) / l
    lane_h = lax.broadcasted_iota(jnp.int32, (1, fw), 1) // HEAD_DIM
    out = jnp.zeros((ls, fw), F32)
    for h in range(n_heads):
        out = jnp.where(lane_h == h, o[h * ls:(h + 1) * ls, :], out)
    fo_ref[...] = out.astype(BF16)


def _sample_mix(q, kn, vn, lfn, u, ck, cv, clf, st, consts, chunk=256):
    t, fw = q.shape
    nb, past, _ = ck.shape
    ls = t // nb
    pw = u.shape[1]
    lk_pad = -(-(past + ls) // chunk) * chunk
    if lk_pad == past + ls:
        lk_pad += chunk
    n_heads = fw // HEAD_DIM
    hm = _head_lane_masks()
    qmask = np.zeros((n_heads * ls, fw), np.float32)
    xmask = np.zeros((n_heads * ls, LANES), np.float32)
    for h in range(n_heads):
        qmask[h * ls:(h + 1) * ls, h * HEAD_DIM:(h + 1) * HEAD_DIM] = 1.0
        xmask[h * ls:(h + 1) * ls, :] = hm[h]
    tri, pq, pk, oq, ok, wpool, spool, winl = consts[:8]
    tri_c = tri[:chunk, :chunk]
    cargs = [tri_c, pq, pk, oq, ok, jnp.asarray(qmask, dtype=BF16), jnp.asarray(xmask, dtype=BF16), wpool, spool, winl]
    row = lambda w: pl.BlockSpec((ls, w), lambda b: (b, 0))
    bat = lambda a: pl.BlockSpec((1,) + a.shape[1:], lambda b: (b, 0, 0))
    return pl.pallas_call(
        functools.partial(_sample_kernel, past=past, ls=ls, chunk=chunk), grid=(nb,),
        in_specs=[row(fw), row(fw), row(fw), row(FOX_H), row(pw), bat(ck), bat(cv), bat(clf), bat(st)]
                 + [_const_spec(a.shape) for a in cargs],
        out_specs=[row(fw), row(pw)],
        out_shape=[jax.ShapeDtypeStruct((t, fw), BF16), jax.ShapeDtypeStruct((t, pw), BF16)],
        scratch_shapes=[pltpu.VMEM((HIST_ROWS + ls, pw), F32), pltpu.VMEM((lk_pad, LANES), F32),
                        pltpu.VMEM((lk_pad, LANES), F32), pltpu.VMEM((lk_pad, fw + LANES), BF16),
                        pltpu.VMEM((lk_pad, fw), BF16)],
        compiler_params=_cparams(1), name="sample_mix")(q, kn, vn, lfn, u, ck, cv, clf, st, *cargs)


def _merge_kernel(x_ref, po_ref, fo_ref, co_ref, gt_ref, wbp_ref, wbf_ref, wbc_ref, wo_ref, gf_ref,
                  wrh_ref, wrl_ref, br_ref, tri_ref, cin_ref,
                  x1_ref, xn_ref, ri_ref, rw_ref, cnt_ref, carry_ref, *, d, n_exp, tm):
    @pl.when(pl.program_id(0) == 0)
    def _():
        carry_ref[...] = cin_ref[...]

    def gate(j):
        return gt_ref[:, j * d:(j + 1) * d].astype(F32)

    merged = (gate(0) * _dot(po_ref[...], wbp_ref[...])
              + gate(1) * _dot(fo_ref[...], wbf_ref[...])
              + gate(2) * _dot(co_ref[...], wbc_ref[...]))
    x1 = x_ref[...] + _dot(merged.astype(BF16), wo_ref[...])
    x1_ref[...] = x1
    xn = _rms(x1, gf_ref[...])
    xn_ref[...] = xn
    xh = xn.astype(BF16)
    xl = (xn - xh.astype(F32)).astype(BF16)
    logits = _dot(xh, wrh_ref[...]) + _dot(xl, wrh_ref[...]) + _dot(xh, wrl_ref[...]) + br_ref[...]

    lane = lax.broadcasted_iota(jnp.int32, logits.shape, 1)
    lane_f = lane.astype(F32)
    work = jnp.where(lane < n_exp, logits, NEG)
    vals, idxs, sels = [], [], []
    for _ in range(TOP_K):
        mx = jnp.max(work, axis=1, keepdims=True)
        idx = jnp.min(jnp.where(work == mx, lane_f, float(LANES)), axis=1, keepdims=True)
        sel = lane_f == idx
        vals.append(mx)
        idxs.append(idx)
        sels.append(sel)
        work = jnp.where(sel, NEG, work)
    es = [jnp.exp(v - vals[0]) for v in vals]
    den = es[0] + es[1] + es[2] + es[3]

    onehot = jnp.zeros(logits.shape, F32)
    for sel in sels:
        onehot = jnp.where(sel, 1.0, onehot)
    cum = _dot(tri_ref[...], onehot.astype(BF16)) + carry_ref[...]
    carry_ref[...] = cum[tm - 1:tm, :] + onehot[tm - 1:tm, :]
    cnt_ref[...] = carry_ref[...]

    ri = jnp.zeros(logits.shape, jnp.int32)
    rw = jnp.zeros(logits.shape, F32)
    for j in range(TOP_K):
        rank = jnp.sum(jnp.where(sels[j], cum, 0.0), axis=1, keepdims=True)
        ri = jnp.where(lane == j, idxs[j].astype(jnp.int32), ri)
        ri = jnp.where(lane == TOP_K + j, rank.astype(jnp.int32), ri)
        rw = jnp.where(lane == j, es[j] / den, rw)
    ri_ref[...] = ri
    rw_ref[...] = rw


def _merge(x, po, fo, co, gt, wbp, wbf, wbc, wo, g_ffn, wrh, wrl, br, tri_s, cnt_in, n_exp, tm=512):
    t, d = x.shape
    tm = min(tm, t)
    row = lambda a: pl.BlockSpec((tm, a.shape[1]), lambda i: (i, 0))
    consts = [wbp, wbf, wbc, wo, g_ffn, wrh, wrl, br, tri_s, cnt_in]
    sd = jax.ShapeDtypeStruct
    rspec = lambda w: pl.BlockSpec((tm, w), lambda i: (i, 0))
    return pl.pallas_call(
        functools.partial(_merge_kernel, d=d, n_exp=n_exp, tm=tm), grid=(t // tm,),
        in_specs=[row(x), row(po), row(fo), row(co), row(gt)] + [_const_spec(a.shape) for a in consts],
        out_specs=[rspec(d), rspec(d), rspec(LANES), rspec(LANES), _const_spec((1, LANES))],
        out_shape=[sd((t, d), F32), sd((t, d), F32), sd((t, LANES), jnp.int32), sd((t, LANES), F32),
                   sd((1, LANES), F32)],
        scratch_shapes=[pltpu.VMEM((1, LANES), F32)],
        compiler_params=_cparams(1), name="merge_router")(x, po, fo, co, gt, *consts)


def _wprep_kernel(w_ref, g_ref, u_ref, t_ref):
    fc = g_ref.shape[1]
    for c in range(t_ref.shape[0]):
        cols = slice(c * LANES, (c + 1) * LANES)
        t_ref[c] = w_ref[0, cols, :].T
        g_ref[0, :, cols] = t_ref[c, pl.ds(0, fc, stride=2), :].astype(BF16)
        u_ref[0, :, cols] = t_ref[c, pl.ds(1, fc, stride=2), :].astype(BF16)


def _wprep(w_gate_up, fc=256):
    e, d, f2 = w_gate_up.shape
    f = f2 // 2
    fc = min(fc, f)
    out = pl.BlockSpec((1, fc, d), lambda i, c: (i, c, 0))
    return pl.pallas_call(
        _wprep_kernel, grid=(e, f // fc),
        in_specs=[pl.BlockSpec((1, d, 2 * fc), lambda i, c: (i, 0, c))],
        out_specs=[out, out],
        out_shape=[jax.ShapeDtypeStruct((e, f, d), BF16)] * 2,
        scratch_shapes=[pltpu.VMEM((d // LANES, 2 * fc, LANES), F32)],
        compiler_params=_cparams(2), name="expert_weight_layout")(w_gate_up)


def _zero_pad_rows(ps_ref, pl_ref, xs_ref, z_ref, sem, n_exp, n_bits):
    z_ref[...] = jnp.zeros_like(z_ref)
    sub = 8

    def each(fn):
        for e in range(n_exp):
            start = ps_ref[e]
            n = pl_ref[e]
            head = jnp.minimum((-start) & (sub - 1), n)
            for r in range(sub - 1):
                @pl.when(r < head)
                def _():
                    fn(pltpu.make_async_copy(z_ref.at[pl.ds(0, 1), :], xs_ref.at[pl.ds(start + r, 1), :], sem))
            start8 = start + head
            n8 = (n - head) // sub
            for b in range(n_bits - 3):
                rows = sub << b
                off = pl.multiple_of(start8 + sub * (n8 & ((1 << b) - 1)), sub)

                @pl.when(((n8 >> b) & 1) == 1)
                def _():
                    fn(pltpu.make_async_copy(z_ref.at[pl.ds(0, rows), :], xs_ref.at[pl.ds(off, rows), :], sem))

    each(lambda c: c.start())
    each(lambda c: c.wait())


def _dispatch_kernel(pos_ref, x_ref, *rest, tm, first, n_exp, n_bits, n_chunks):
    if first and n_chunks:
        ps_ref, pl_ref, w_ref, xs_ref, g_ref, u_ref, sem, z_ref, t_ref = rest
    elif first:
        ps_ref, pl_ref, xs_ref, sem, z_ref = rest
    else:
        _, xs_ref, sem = rest
    if first:
        @pl.when(pl.program_id(0) == 0)
        def _():
            _zero_pad_rows(ps_ref, pl_ref, xs_ref, z_ref, sem, n_exp, n_bits)

    def issue(t, c):
        for j in range(TOP_K):
            pltpu.make_async_copy(x_ref.at[pl.ds(t, 1), :],
                                  xs_ref.at[pl.ds(pos_ref[t * TOP_K + j], 1), :], sem).start(priority=j % 2)
        return c

    lax.fori_loop(0, tm, issue, 0, unroll=DMA_UNROLL)
    if first and n_chunks:
        @pl.when(pl.program_id(0) < n_chunks)
        def _():
            _wprep_kernel(w_ref, g_ref, u_ref, t_ref)
    for j in range(TOP_K):
        pltpu.make_async_copy(x_ref, xs_ref.at[pl.ds(0, tm), :], sem).wait()


def _dispatch(pos_flat, xn, n_rows, pad=None, xs=None, w_gate_up=None, tm=512, tile_rows=512):
    t, d = xn.shape
    tm = min(tm, t)
    steps = t // tm
    first = xs is None
    n_bits = (tile_rows - 1).bit_length()
    smem = lambda n: pl.BlockSpec((n,), lambda i: (0,), memory_space=pltpu.SMEM)
    in_specs = [pl.BlockSpec((tm * TOP_K,), lambda i: (i,), memory_space=pltpu.SMEM),
                pl.BlockSpec((tm, d), lambda i: (i, 0))]
    scratch = [pltpu.SemaphoreType.DMA]
    out_specs = [pl.BlockSpec(memory_space=pl.ANY)]
    out_shape = [jax.ShapeDtypeStruct((n_rows, d), F32)]
    n_chunks = 0
    if first:
        n_exp = pad[0].shape[0]
        in_specs += [smem(n_exp), smem(n_exp)]
        args = [pos_flat, xn, pad[0], pad[1]]
        scratch.append(pltpu.VMEM((1 << (n_bits - 1), d), F32))
        aliases = {}
        if w_gate_up is not None:
            e, _, f2 = w_gate_up.shape
            f = f2 // 2
            per_exp = steps // e
            assert per_exp >= 1 and f % per_exp == 0
            fc = f // per_exp
            n_chunks = e * per_exp
            chunk = lambda i: jnp.minimum(i, n_chunks - 1)
            in_specs.append(pl.BlockSpec((1, d, 2 * fc), lambda i: (chunk(i) // per_exp, 0, chunk(i) % per_exp)))
            wout = pl.BlockSpec((1, fc, d), lambda i: (chunk(i) // per_exp, chunk(i) % per_exp, 0))
            out_specs += [wout, wout]
            out_shape += [jax.ShapeDtypeStruct((e, f, d), BF16)] * 2
            scratch.append(pltpu.VMEM((d // LANES, 2 * fc, LANES), F32))
            args.append(w_gate_up)
    else:
        n_exp = 0
        in_specs.append(pl.BlockSpec(memory_space=pl.ANY))
        args = [pos_flat, xn, xs]
        aliases = {2: 0}
    out = pl.pallas_call(
        functools.partial(_dispatch_kernel, tm=tm, first=first, n_exp=n_exp, n_bits=n_bits, n_chunks=n_chunks),
        grid=(steps,), in_specs=in_specs, out_specs=out_specs, out_shape=out_shape,
        scratch_shapes=scratch, input_output_aliases=aliases,
        compiler_params=_cparams(1), name="dispatch")(*args)
    return out if n_chunks else out[0]


def _ffn_kernel(te_ref, nu_ref, x_ref, wg_ref, wu_ref, wdf_ref, bg_ref, bu_ref, bd_ref, y_ref, wd_ref):
    i = pl.program_id(0)

    @pl.when(i < nu_ref[0])
    def _():
        @pl.when((i == 0) | (te_ref[i] != te_ref[jnp.maximum(i - 1, 0)]))
        def _():
            wd_ref[0] = wdf_ref[0].astype(BF16)

        x = x_ref[...].astype(BF16)
        g = _dot_nt(x, wg_ref[0]) + bg_ref[0]
        u = _dot_nt(x, wu_ref[0]) + bu_ref[0]
        gate = jnp.minimum(g, SWIGLU_LIMIT)
        up = jnp.clip(u, -SWIGLU_LIMIT, SWIGLU_LIMIT)
        act = (up + 1.0) * gate * jax.nn.sigmoid(SWIGLU_ALPHA * gate)
        y_ref[...] = _dot(act.astype(BF16), wd_ref[0]) + bd_ref[0]


def _ffn(tile_exp, n_used, xs, wg_t, wu_t, wd, bg, bu, bd, tm):
    p, d = xs.shape
    e, f, _ = wg_t.shape
    nt = p // tm
    rowi = lambda i, te, nu: (jnp.minimum(i, nu[0] - 1), 0)
    wi = lambda i, te, nu: (te[i], 0, 0)
    grid_spec = pltpu.PrefetchScalarGridSpec(
        num_scalar_prefetch=2, grid=(nt,),
        in_specs=[pl.BlockSpec((tm, d), rowi),
                  pl.BlockSpec((1, f, d), wi), pl.BlockSpec((1, f, d), wi), pl.BlockSpec((1, f, d), wi),
                  pl.BlockSpec((1, 1, f), wi), pl.BlockSpec((1, 1, f), wi), pl.BlockSpec((1, 1, d), wi)],
        out_specs=pl.BlockSpec((tm, d), rowi),
        scratch_shapes=[pltpu.VMEM((1, f, d), BF16)])
    return pl.pallas_call(
        _ffn_kernel, grid_spec=grid_spec, out_shape=jax.ShapeDtypeStruct((p, d), F32),
        compiler_params=_cparams(1), name="expert_ffn")(tile_exp, n_used, xs, wg_t, wu_t, wd, bg, bu, bd)


def _combine_kernel(pos_ref, posn_ref, w_ref, x1_ref, gf_ref, y_ref, o_ref, buf_ref, sems, *, tm):
    i = pl.program_id(0)
    n = pl.num_programs(0)

    def gather(idx_ref, s):
        def issue(t, c):
            for j in range(TOP_K):
                pltpu.make_async_copy(y_ref.at[pl.ds(idx_ref[t * TOP_K + j], 1), :],
                                      buf_ref.at[s, j, pl.ds(t, 1), :], sems.at[s]).start(priority=j % 2)
            return c

        lax.fori_loop(0, tm, issue, 0, unroll=DMA_UNROLL)

    def step(slot):
        @pl.when(i == 0)
        def _():
            gather(pos_ref, slot)

        @pl.when(i + 1 < n)
        def _():
            gather(posn_ref, 1 - slot)

        for j in range(TOP_K):
            pltpu.make_async_copy(y_ref.at[pl.ds(0, tm), :], buf_ref.at[slot, j], sems.at[slot]).wait()
        w = w_ref[...]
        acc = x1_ref[...]
        for j in range(TOP_K):
            acc = acc + w[:, j:j + 1] * buf_ref[slot, j]
        o_ref[...] = _rms(acc, gf_ref[...])

    for parity in range(2):
        pl.when(i % 2 == parity)(functools.partial(step, parity))


def _combine(pos_flat, rw, x1, g_final, y, tm=512):
    t, d = x1.shape
    tm = min(tm, t)
    n = t // tm
    return pl.pallas_call(
        functools.partial(_combine_kernel, tm=tm), grid=(n,),
        in_specs=[pl.BlockSpec((tm * TOP_K,), lambda i: (i,), memory_space=pltpu.SMEM),
                  pl.BlockSpec((tm * TOP_K,), lambda i: (jnp.minimum(i + 1, n - 1),), memory_space=pltpu.SMEM),
                  pl.BlockSpec((tm, LANES), lambda i: (i, 0)),
                  pl.BlockSpec((tm, d), lambda i: (i, 0)),
                  _const_spec((1, d)),
                  pl.BlockSpec(memory_space=pl.ANY)],
        out_specs=pl.BlockSpec((tm, d), lambda i: (i, 0)),
        out_shape=jax.ShapeDtypeStruct((t, d), F32),
        scratch_shapes=[pltpu.VMEM((2, TOP_K, tm, d), F32), pltpu.SemaphoreType.DMA((2,))],
        compiler_params=_cparams(1), name="combine")(pos_flat, pos_flat, rw, x1, g_final, y)


def kernel(x_prompt, x_sample, mem_prompt, cache_fox_k, cache_fox_v, cache_fox_logf, state_pool, cache_mem_k, cache_mem_v, g_mix, w_in, b_f, w_pool, s_pool, w_br_pool, w_br_fox, w_br_cross, b_gates, w_out, g_mem, w_mem_kv, g_ffn, w_router, b_router, w_gate_up, b_gate_up, w_down, b_down, g_final):
    depth = w_in.shape[0]
    assert depth == 1, "single-layer model"
    bp, lp, d = x_prompt.shape
    bs, ls, _ = x_sample.shape
    past = cache_fox_k.shape[2]
    n_mem = mem_prompt.shape[1]
    n_exp = w_router.shape[2]
    d_ff = w_down.shape[2]
    pool_w = state_pool.shape[3]
    fox_w = FOX_H * HEAD_DIM
    cross_w = CROSS_H * HEAD_DIM
    scale = HEAD_DIM ** -0.5

    w = w_in[0]
    o_q = pool_w
    o_k, o_v, o_f = o_q + fox_w, o_q + 2 * fox_w, o_q + 3 * fox_w
    o_cq = o_f + FOX_H
    o_g = o_cq + cross_w
    w_cat = jnp.concatenate(
        [w[:, :o_q], w[:, o_q:o_k] * (scale * LOG2E), w[:, o_k:o_v], w[:, o_v:o_f], w[:, o_cq:o_g] * scale, w[:, o_g:],
         jnp.pad(w[:, o_f:o_cq], ((0, 0), (0, LANES - FOX_H)))], axis=1).astype(BF16)
    widths = [pool_w, fox_w, fox_w, fox_w, cross_w, N_BRANCH * d, LANES]
    offs = tuple(int(v) for v in np.concatenate([[0], np.cumsum(widths)]))
    b_f_p = jnp.pad(b_f[0], (0, LANES - FOX_H)).reshape(1, LANES)
    b_g = b_gates[0].reshape(1, -1)
    g_mix2 = g_mix[0].reshape(1, d)
    gw = pool_w // len(POOL_WINDOWS)
    wpool_bd = jnp.zeros((pool_w, pool_w), F32)
    for g in range(len(POOL_WINDOWS)):
        wpool_bd = wpool_bd.at[g * gw:(g + 1) * gw, g * gw:(g + 1) * gw].set(w_pool[0, g])
    tm = 512
    pq, pk, oq, ok = _bias_placement()
    consts = (_tri(tm, strict=False), pq, pk, oq, ok, wpool_bd.astype(BF16), s_pool[0].reshape(1, pool_w),
              _pool_lane_windows(pool_w), _head_selector(fox_w))
    hmask = jnp.asarray(_head_lane_masks()[:, :, None])

    xp = x_prompt.reshape(bp * lp, d)
    (u_p, po_p, q_p, qx_p, kb_p, kx_p, vb_p, k_p, v_p, lf_p, qc_p, gt_p, stats_p) = _inproj(
        xp, g_mix2, w_cat, b_f_p, b_g, offs, prompt=True, seq_len=lp, consts=consts, tm=tm)
    mk, mv = _memkv(mem_prompt.reshape(bp * n_mem, d), g_mem[0].reshape(1, d), w_mem_kv[0].astype(BF16))
    mk3, mv3 = mk.reshape(bp, n_mem, cross_w), mv.reshape(bp, n_mem, cross_w)
    k_first = _first_needed_block(stats_p, bp, fox_w // LANES)
    fo_p = _fox_prompt(k_first, hmask, q_p, qx_p, kb_p, kx_p, vb_p, bp)
    co_p = _cross(qc_p, mk3, mv3, bp)

    xs_ = x_sample.reshape(bs * ls, d)
    (u_s, q_s, k_s, v_s, lf_s, qc_s, gt_s) = _inproj(xs_, g_mix2, w_cat, b_f_p, b_g, offs, prompt=False, tm=tm)
    fo_s, po_s = _sample_mix(q_s, k_s, v_s, lf_s, u_s,
                             cache_fox_k[0].reshape(bs, past, fox_w), cache_fox_v[0].reshape(bs, past, fox_w),
                             cache_fox_logf[0], state_pool[0], consts)
    co_s = _cross(qc_s, cache_mem_k[0].reshape(bs, n_mem, cross_w), cache_mem_v[0].reshape(bs, n_mem, cross_w), bs)

    wbp, wbf, wbc = w_br_pool[0].astype(BF16), w_br_fox[0].astype(BF16), w_br_cross[0].astype(BF16)
    wo = w_out[0].astype(BF16)
    g_ffn2 = g_ffn[0].reshape(1, d)
    wr = jnp.pad(w_router[0], ((0, 0), (0, LANES - n_exp)))
    wrh = wr.astype(BF16)
    wrl = (wr - wrh.astype(F32)).astype(BF16)
    br = jnp.pad(b_router[0], (0, LANES - n_exp)).reshape(1, LANES)
    tri_s = _tri(tm, strict=True)
    margs = (wbp, wbf, wbc, wo, g_ffn2, wrh, wrl, br, tri_s)
    x1_p, xn_p, ri_p, rw_p, cnt_p = _merge(xp, po_p, fo_p, co_p, gt_p, *margs, jnp.zeros((1, LANES), F32), n_exp, tm)
    x1_s, xn_s, ri_s, rw_s, cnt = _merge(xs_, po_s, fo_s, co_s, gt_s, *margs, cnt_p, n_exp, tm)

    tmf = 512
    t_all = bp * lp + bs * ls
    counts = cnt[0, :n_exp].astype(jnp.int32)
    tiles_e = (counts + tmf - 1) // tmf
    tile_end = jnp.cumsum(tiles_e)
    row_off = (tile_end - tiles_e) * tmf
    nt_max = (t_all * TOP_K + n_exp * (tmf - 1)) // tmf + 1
    n_used = tile_end[-1:]
    tile_ids = jnp.minimum(jnp.arange(nt_max, dtype=jnp.int32), n_used[0] - 1)
    tile_exp = jnp.minimum(jnp.sum(tile_ids[:, None] >= tile_end[None, :], axis=1), n_exp - 1).astype(jnp.int32)

    def positions(ri):
        e = ri[:, :TOP_K]
        r = ri[:, TOP_K:2 * TOP_K]
        off = jnp.sum(jnp.where(e[:, :, None] == jnp.arange(n_exp)[None, None, :], row_off[None, None, :], 0), axis=2)
        return (off + r).reshape(-1).astype(jnp.int32)

    pos_p, pos_s = positions(ri_p), positions(ri_s)

    wd = w_down[0]
    bgu = b_gate_up[0]
    bg_e, bu_e = bgu[:, 0::2].reshape(n_exp, 1, d_ff), bgu[:, 1::2].reshape(n_exp, 1, d_ff)
    bd_e = b_down[0].reshape(n_exp, 1, d)
    pad = ((row_off + counts).astype(jnp.int32), (tiles_e * tmf - counts).astype(jnp.int32))
    if (bp * lp) // min(512, bp * lp) >= n_exp:
        xs_sorted, wg_t, wu_t = _dispatch(pos_p, xn_p, nt_max * tmf, pad=pad, w_gate_up=w_gate_up[0], tile_rows=tmf)
    else:
        wg_t, wu_t = _wprep(w_gate_up[0])
        xs_sorted = _dispatch(pos_p, xn_p, nt_max * tmf, pad=pad, tile_rows=tmf)
    xs_sorted = _dispatch(pos_s, xn_s, nt_max * tmf, xs=xs_sorted, tile_rows=tmf)
    y = _ffn(tile_exp, n_used.astype(jnp.int32), xs_sorted, wg_t, wu_t, wd, bg_e, bu_e, bd_e, tmf)
    g_fin = g_final.reshape(1, d)
    y_p = _combine(pos_p, rw_p, x1_p, g_fin, y)
    y_s = _combine(pos_s, rw_s, x1_s, g_fin, y)

    kv5 = lambda a, b, l: a.reshape(1, b, l, FOX_H, HEAD_DIM)
    return (y_p.reshape(bp, lp, d), y_s.reshape(bs, ls, d),
            kv5(k_p, bp, lp), kv5(v_p, bp, lp), lf_p.reshape(1, bp, lp, FOX_H),
            u_p.reshape(bp, lp, pool_w)[:, lp - POOL_HIST:, :][None],
            mk.reshape(1, bp, n_mem, CROSS_H, HEAD_DIM), mv.reshape(1, bp, n_mem, CROSS_H, HEAD_DIM),
            kv5(k_s, bs, ls), kv5(v_s, bs, ls), lf_s.reshape(1, bs, ls, FOX_H),
            u_s.reshape(bs, ls, pool_w)[:, ls - POOL_HIST:, :][None])
```
